```python
import math
import jax, jax.numpy as jnp
from jax import lax
import numpy as np

D_MODEL = 1024
BATCH = 4
SEQ = 8192
DEPTH = 2
DEC_BATCH = 8
DEC_SEQ = 4096
PAST_LEN = 128

MIX_W = 2 * D_MODEL
CONV_K = 5
SSD_W = MIX_W // 2
SSD_P = 64
SSD_H = SSD_W // SSD_P
SSD_G = 2
SSD_HPG = SSD_H // SSD_G
SSD_N = 128
SSD_CHUNK = 128
SSD_CONV_DIM = SSD_W + 2 * SSD_G * SSD_N
DN_W = MIX_W // 4
DN_HD = 128
DN_H = DN_W // DN_HD
DN_CHUNK = 64
DN_CONV_DIM = 3 * DN_W
ATT_W = MIX_W // 4
ATT_HD = 128
ATT_H = ATT_W // ATT_HD
ATT_KV = 2
ATT_REP = ATT_H // ATT_KV
ATT_KV_W = ATT_KV * ATT_HD
Q_BLOCK = 128
GRID_W = 64
ROPE_THETA = 10000.0
N_EXPERTS = 16
EC_CAPACITY = 2
EXPERT_FF = 2 * D_MODEL
ADA_CHUNKS = 6
EPS = 1e-6
IN_WIDTHS = (SSD_W, SSD_CONV_DIM, SSD_H, DN_CONV_DIM, DN_W, DN_H, DN_H, ATT_W, ATT_KV_W, ATT_KV_W)
IN_DIM = sum(IN_WIDTHS)

kernel_name = "hybrid_bidir_ssd_deltanet_axialgqa_ecmoe_encoder"


def rmsnorm(x, w):
    xf = x.astype(jnp.float32)
    y = xf * lax.rsqrt(jnp.mean(xf * xf, axis=-1, keepdims=True) + EPS)
    return (y * w.astype(jnp.float32)).astype(x.dtype)


def l2norm(x):
    return x * lax.rsqrt(jnp.sum(x * x, axis=-1, keepdims=True) + EPS)


def flip(t):
    return jnp.flip(t, axis=1)


def dwconv_centred(x, w):
    k = w.shape[0]
    return lax.conv_general_dilated(x, w[:, None, :].astype(x.dtype), window_strides=(1,),
                                    padding=((k // 2, k // 2),),
                                    dimension_numbers=("NWC", "WIO", "NWC"),
                                    feature_group_count=x.shape[-1])


def ssd_chunked_scan(x, dt, a, bm, cm):
    b, L, G, J, P = x.shape
    N = bm.shape[-1]
    Q = SSD_CHUNK
    nc = L // Q
    x = x.reshape(b, nc, Q, G, J, P)
    dt = dt.reshape(b, nc, Q, G, J)
    bm = bm.reshape(b, nc, Q, G, N)
    cm = cm.reshape(b, nc, Q, G, N)
    acs = jnp.cumsum(dt * a, axis=2)
    xdt = x * dt[..., None]
    causal = jnp.tril(jnp.ones((Q, Q), bool))[:, :, None, None]
    seg = acs[:, :, :, None] - acs[:, :, None, :]
    lmat = jnp.exp(jnp.where(causal, seg, -jnp.inf))
    cb = jnp.einsum("bclgn,bcsgn->bclsg", cm, bm)
    y_diag = jnp.einsum("bclsg,bclsgj,bcsgjp->bclgjp", cb, lmat, xdt)
    decay_to_end = jnp.exp(acs[:, :, -1:] - acs)
    chunk_states = jnp.einsum("bcsgn,bcsgj,bcsgjp->bcgjpn", bm, decay_to_end, xdt)
    chunk_decay = jnp.exp(acs[:, :, -1])

    def step(h, inp):
        st, dec = inp
        return h * dec[..., None, None] + st, h

    h0 = jnp.zeros((b, G, J, P, N), x.dtype)
    _, h_in = lax.scan(step, h0, (jnp.moveaxis(chunk_states, 1, 0), jnp.moveaxis(chunk_decay, 1, 0)))
    h_in = jnp.moveaxis(h_in, 0, 1)
    y_off = jnp.einsum("bclgn,bcgjpn,bclgj->bclgjp", cm, h_in, jnp.exp(acs))
    return (y_diag + y_off).reshape(b, L, G, J, P)


def ssd_mixer(z, xbc, dt_raw, conv_w, conv_b, dt_bias, a_log, d_skip, norm_w):
    b, L, _ = z.shape
    f32 = jnp.float32
    xbc = jax.nn.silu(dwconv_centred(xbc, conv_w) + conv_b).astype(f32)
    xs, bm, cm = jnp.split(xbc, [SSD_W, SSD_W + SSD_G * SSD_N], axis=-1)
    xs = xs.reshape(b, L, SSD_G, SSD_HPG, SSD_P)
    bm = bm.reshape(b, L, SSD_G, SSD_N)
    cm = cm.reshape(b, L, SSD_G, SSD_N)
    a = -jnp.exp(a_log.astype(f32)).reshape(2, SSD_G, SSD_HPG)
    dt_raw = dt_raw.astype(f32)
    dt_bias = dt_bias.astype(f32)
    dt_f = jax.nn.softplus(dt_raw + dt_bias[0]).reshape(b, L, SSD_G, SSD_HPG)
    dt_b = jax.nn.softplus(dt_raw + dt_bias[1]).reshape(b, L, SSD_G, SSD_HPG)
    y = d_skip.astype(f32).reshape(SSD_G, SSD_HPG)[..., None] * xs
    y = y + ssd_chunked_scan(xs, dt_f, a[0], bm, cm)
    y = y + flip(ssd_chunked_scan(flip(xs), flip(dt_b), a[1], flip(bm), flip(cm)))
    y = y.reshape(b, L, SSD_W) * jax.nn.silu(z.astype(f32))
    y = rmsnorm(y.reshape(b, L, SSD_G, SSD_W // SSD_G), norm_w.reshape(SSD_G, SSD_W // SSD_G))
    return y.reshape(b, L, SSD_W)


def unit_lower_inverse(n):
    eye = jnp.eye(n.shape[-1], dtype=n.dtype)
    p = -n
    inv = eye + p
    for _ in range(int(math.log2(n.shape[-1])) - 1):
        p = p @ p
        inv = inv @ (eye + p)
    return inv


def gated_delta_chunked(q, k, v, g, beta):
    b, L, H, dk = q.shape
    dv = v.shape[-1]
    C = DN_CHUNK
    nc = L // C

    def chunks(t):
        return jnp.moveaxis(t.reshape(b, nc, C, *t.shape[2:]), 2, 3)

    q, k, v, g, beta = (chunks(t) for t in (q, k, v, g, beta))
    gcs = jnp.cumsum(g, axis=-1)
    incl = jnp.tril(jnp.ones((C, C), bool))
    strict = jnp.tril(jnp.ones((C, C), bool), -1)
    decay = jnp.exp(jnp.where(incl, gcs[..., :, None] - gcs[..., None, :], -jnp.inf))
    kb = k * beta[..., None]
    n_mat = jnp.where(strict, jnp.einsum("bnhld,bnhsd->bnhls", kb, k) * decay, 0.0)
    t_mat = unit_lower_inverse(n_mat)
    u = t_mat @ (v * beta[..., None])
    w = t_mat @ (kb * jnp.exp(gcs)[..., None])
    qk = jnp.einsum("bnhld,bnhsd->bnhls", q, k) * decay
    qg = q * jnp.exp(gcs)[..., None]
    kd = k * jnp.exp(gcs[..., -1:] - gcs)[..., None]
    dec = jnp.exp(gcs[..., -1])

    def step(S, inp):
        u_c, w_c, qg_c, qk_c, kd_c, dec_c = inp
        v_new = u_c - jnp.einsum("bhck,bhkv->bhcv", w_c, S)
        o_c = jnp.einsum("bhck,bhkv->bhcv", qg_c, S) + jnp.einsum("bhcs,bhsv->bhcv", qk_c, v_new)
        S = S * dec_c[..., None, None] + jnp.einsum("bhsk,bhsv->bhkv", kd_c, v_new)
        return S, o_c

    S0 = jnp.zeros((b, H, dk, dv), q.dtype)
    _, o = lax.scan(step, S0, tuple(jnp.moveaxis(t, 1, 0) for t in (u, w, qg, qk, kd, dec)))
    return jnp.moveaxis(o, (0, 2), (1, 3)).reshape(b, L, H, dv)


def gated_deltanet_mixer(qkv, z, b_raw, a_raw, conv_w, dt_bias, a_log, norm_w):
    b, L, _ = qkv.shape
    f32 = jnp.float32
    qkv = jax.nn.silu(dwconv_centred(qkv, conv_w)).astype(f32)
    q, k, v = (t.reshape(b, L, DN_H, DN_HD) for t in jnp.split(qkv, 3, axis=-1))
    q = l2norm(q) * (DN_HD ** -0.5)
    k = l2norm(k)
    beta = jax.nn.sigmoid(b_raw.astype(f32))
    a_raw = a_raw.astype(f32)
    dt_bias = dt_bias.astype(f32)
    a_mag = jnp.exp(a_log.astype(f32))
    g_f = -a_mag[0] * jax.nn.softplus(a_raw + dt_bias[0])
    g_b = -a_mag[1] * jax.nn.softplus(a_raw + dt_bias[1])
    o = gated_delta_chunked(q, k, v, g_f, beta)
    o = o + flip(gated_delta_chunked(flip(q), flip(k), flip(v), flip(g_b), flip(beta)))
    o = rmsnorm(o, norm_w) * jax.nn.silu(z.astype(f32).reshape(b, L, DN_H, DN_HD))
    return o.reshape(b, L, DN_W)


def axial_angles(n_tokens):
    rows = n_tokens // GRID_W
    row_idx = jnp.repeat(jnp.arange(rows, dtype=jnp.float32), GRID_W)
    col_idx = jnp.tile(jnp.arange(GRID_W, dtype=jnp.float32), rows)
    axis_dim = ATT_HD // 2
    inv_freq = jnp.power(ROPE_THETA, -jnp.arange(0, axis_dim, 2, dtype=jnp.float32) / axis_dim)
    return row_idx[:, None] * inv_freq, col_idx[:, None] * inv_freq


def rotate_axis(x, ang):
    x1, x2 = jnp.split(x, 2, axis=-1)
    cos = jnp.cos(ang)[:, None, :].astype(x.dtype)
    sin = jnp.sin(ang)[:, None, :].astype(x.dtype)
    return jnp.concatenate([x1 * cos - x2 * sin, x1 * sin + x2 * cos], axis=-1)


def axial_rope(x, row_ang, col_ang):
    half = x.shape[-1] // 2
    return jnp.concatenate([rotate_axis(x[..., :half], row_ang), rotate_axis(x[..., half:], col_ang)], axis=-1)


def axial_gqa_mixer(q, k, v, q_norm, k_norm, row_ang, col_ang):
    b, L, _ = q.shape
    q = axial_rope(rmsnorm(q.reshape(b, L, ATT_H, ATT_HD), q_norm), row_ang, col_ang)
    k = axial_rope(rmsnorm(k.reshape(b, L, ATT_KV, ATT_HD), k_norm), row_ang, col_ang)
    v = v.reshape(b, L, ATT_KV, ATT_HD)
    nq = L // Q_BLOCK
    qb = jnp.moveaxis(q.reshape(b, nq, Q_BLOCK, ATT_KV, ATT_REP, ATT_HD), 1, 0)
    scale = ATT_HD ** -0.5

    def block(qi):
        s = jnp.einsum("bqgrd,bkgd->bgrqk", qi, k, preferred_element_type=jnp.float32) * scale
        p = jax.nn.softmax(s, axis=-1).astype(v.dtype)
        return jnp.einsum("bgrqk,bkgd->bqgrd", p, v)

    o = lax.map(block, qb)
    return jnp.moveaxis(o, 0, 1).reshape(b, L, ATT_W)


def parallel_head_groups(h, w_in, conv_ssd_w, conv_ssd_b, ssd_dt_bias, ssd_a_log, ssd_d, ssd_norm,
                         conv_dn_w, dn_dt_bias, dn_a_log, dn_norm, q_norm, k_norm, w_out,
                         row_ang, col_ang):
    proj = h @ w_in
    (z_ssd, xbc, dt_ssd, qkv_dn, z_dn, beta_dn, a_dn, q_att, k_att, v_att) = jnp.split(
        proj, np.cumsum(IN_WIDTHS)[:-1].tolist(), axis=-1)
    y_ssd = ssd_mixer(z_ssd, xbc, dt_ssd, conv_ssd_w, conv_ssd_b, ssd_dt_bias, ssd_a_log, ssd_d, ssd_norm)
    y_dn = gated_deltanet_mixer(qkv_dn, z_dn, beta_dn, a_dn, conv_dn_w, dn_dt_bias, dn_a_log, dn_norm)
    y_att = axial_gqa_mixer(q_att, k_att, v_att, q_norm, k_norm, row_ang, col_ang)
    y = jnp.concatenate([y_ssd.astype(h.dtype), y_dn.astype(h.dtype), y_att.astype(h.dtype)], axis=-1)
    return y @ w_out


def expert_choice_ffn(h, w_router, w_gate, w_up, w_down):
    b, L, d = h.shape
    n_tok = b * L
    cap = EC_CAPACITY * n_tok // N_EXPERTS
    hf = h.reshape(n_tok, d)
    aff = jax.nn.softmax((hf @ w_router).astype(jnp.float32), axis=-1)
    gate, idx = lax.top_k(aff.T, cap)
    xe = hf[idx]
    hidden = jax.nn.silu(jnp.einsum("ecd,edf->ecf", xe, w_gate)) * jnp.einsum("ecd,edf->ecf", xe, w_up)
    ye = jnp.einsum("ecf,efd->ecd", hidden, w_down) * gate[..., None].astype(h.dtype)
    out = jnp.zeros((n_tok, d), ye.dtype).at[idx.reshape(-1)].add(ye.reshape(-1, d))
    return out.reshape(b, L, d)


def encoder_trunk(x, c, p):
    row_ang, col_ang = axial_angles(x.shape[1])
    for l in range(DEPTH):
        mod = jax.nn.silu(c) @ p["w_ada"][l] + p["b_ada"][l]
        sh1, sc1, g1, sh2, sc2, g2 = jnp.split(mod[:, None, :], ADA_CHUNKS, axis=-1)
        h = rmsnorm(x, p["norm_mix_pre"][l]) * (1 + sc1) + sh1
        m = parallel_head_groups(h, p["w_in"][l], p["conv_ssd_w"][l], p["conv_ssd_b"][l],
                                 p["ssd_dt_bias"][l], p["ssd_a_log"][l], p["ssd_d"][l], p["ssd_norm"][l],
                                 p["conv_dn_w"][l], p["dn_dt_bias"][l], p["dn_a_log"][l], p["dn_norm"][l],
                                 p["q_norm"][l], p["k_norm"][l], p["w_out"][l], row_ang, col_ang)
        x = x + g1 * rmsnorm(m, p["norm_mix_post"][l])
        h = rmsnorm(x, p["norm_ffn_pre"][l]) * (1 + sc2) + sh2
        f = expert_choice_ffn(h, p["w_router"][l], p["w_gate"][l], p["w_up"][l], p["w_down"][l])
        x = x + g2 * rmsnorm(f, p["norm_ffn_post"][l])
    return x


def setup_inputs(seed: int = 0) -> dict:
    key = jax.random.key(seed)
    ks = iter(jax.random.split(key, 40))
    f32 = jnp.float32

    def nrm(shape, std):
        return jax.random.normal(next(ks), shape, f32) * std

    def gain(shape):
        return 1.0 + nrm(shape, 0.02)

    def dt_bias(shape):
        dt = jnp.exp(jax.random.uniform(next(ks), shape, f32, math.log(1e-3), math.log(1e-1)))
        return dt + jnp.log(-jnp.expm1(-dt))

    def a_log(shape):
        return jnp.log(jax.random.uniform(next(ks), shape, f32, 1.0, 16.0))

    D = D_MODEL
    return {
        "x_prompt": nrm((BATCH, SEQ, D), 1.0),
        "x_sample": nrm((DEC_BATCH, DEC_SEQ, D), 1.0),
        "c_prompt": nrm((BATCH, D), 1.0),
        "c_sample": nrm((DEC_BATCH, D), 1.0),
        "w_ada": nrm((DEPTH, D, ADA_CHUNKS * D), 0.5 * D ** -0.5),
        "b_ada": nrm((DEPTH, ADA_CHUNKS * D), 0.02),
        "norm_mix_pre": gain((DEPTH, D)),
        "norm_mix_post": gain((DEPTH, D)),
        "w_in": nrm((DEPTH, D, IN_DIM), D ** -0.5),
        "conv_ssd_w": nrm((DEPTH, CONV_K, SSD_CONV_DIM), CONV_K ** -0.5),
        "conv_ssd_b": nrm((DEPTH, SSD_CONV_DIM), 0.02),
        "ssd_dt_bias": dt_bias((DEPTH, 2, SSD_H)),
        "ssd_a_log": a_log((DEPTH, 2, SSD_H)),
        "ssd_d": 1.0 + nrm((DEPTH, SSD_H), 0.1),
        "ssd_norm": gain((DEPTH, SSD_W)),
        "conv_dn_w": nrm((DEPTH, CONV_K, DN_CONV_DIM), CONV_K ** -0.5),
        "dn_dt_bias": dt_bias((DEPTH, 2, DN_H)),
        "dn_a_log": a_log((DEPTH, 2, DN_H)),
        "dn_norm": gain((DEPTH, DN_HD)),
        "q_norm": gain((DEPTH, ATT_HD)),
        "k_norm": gain((DEPTH, ATT_HD)),
        "w_out": nrm((DEPTH, MIX_W, D), MIX_W ** -0.5),
        "norm_ffn_pre": gain((DEPTH, D)),
        "norm_ffn_post": gain((DEPTH, D)),
        "w_router": nrm((DEPTH, D, N_EXPERTS), D ** -0.5),
        "w_gate": nrm((DEPTH, N_EXPERTS, D, EXPERT_FF), D ** -0.5),
        "w_up": nrm((DEPTH, N_EXPERTS, D, EXPERT_FF), D ** -0.5),
        "w_down": nrm((DEPTH, N_EXPERTS, EXPERT_FF, D), EXPERT_FF ** -0.5),
    }


def reference(x_prompt, x_sample, c_prompt, c_sample, w_ada, b_ada, norm_mix_pre, norm_mix_post, w_in,
              conv_ssd_w, conv_ssd_b, ssd_dt_bias, ssd_a_log, ssd_d, ssd_norm,
              conv_dn_w, dn_dt_bias, dn_a_log, dn_norm, q_norm, k_norm, w_out,
              norm_ffn_pre, norm_ffn_post, w_router, w_gate, w_up, w_down):
    params = dict(w_ada=w_ada, b_ada=b_ada, norm_mix_pre=norm_mix_pre, norm_mix_post=norm_mix_post,
                  w_in=w_in, conv_ssd_w=conv_ssd_w, conv_ssd_b=conv_ssd_b, ssd_dt_bias=ssd_dt_bias,
                  ssd_a_log=ssd_a_log, ssd_d=ssd_d, ssd_norm=ssd_norm, conv_dn_w=conv_dn_w,
                  dn_dt_bias=dn_dt_bias, dn_a_log=dn_a_log, dn_norm=dn_norm, q_norm=q_norm, k_norm=k_norm,
                  w_out=w_out, norm_ffn_pre=norm_ffn_pre, norm_ffn_post=norm_ffn_post,
                  w_router=w_router, w_gate=w_gate, w_up=w_up, w_down=w_down)
    y_prompt = encoder_trunk(x_prompt, c_prompt, params)
    y_sample = encoder_trunk(x_sample, c_sample, params)
    return (y_prompt, y_sample)
```

```python
import functools

import jax
import jax.numpy as jnp
import numpy as np
from jax import lax
from jax.experimental import pallas as pl
from jax.experimental.pallas import tpu as pltpu

F32 = jnp.float32
BF16 = jnp.bfloat16
I32 = jnp.int32

D_MODEL = 1024
SSD_W = 1024
SSD_P = 64
SSD_H = 16
SSD_G = 2
SSD_N = 128
SSD_CONV_DIM = SSD_W + 2 * SSD_G * SSD_N
DN_W = 512
DN_HD = 128
DN_H = 4
DN_CONV_DIM = 3 * DN_W
ATT_W = 512
ATT_HD = 128
ATT_H = 4
ATT_KV = 2
ATT_KV_W = ATT_KV * ATT_HD
GRID_W = 64
ROPE_THETA = 10000.0
N_EXPERTS = 16
EC_CAPACITY = 2
EXPERT_FF = 2048
EPS = 1e-6
CONV_K = 5

LANES = 128
SUBLANES = 8
VMEM_LIMIT = 56 * 1024 * 1024

_IN_SEGS = (
    ("z_ssd", SSD_W, SSD_W, BF16),
    ("xbc", SSD_CONV_DIM, SSD_CONV_DIM, BF16),
    ("dt", SSD_H, LANES, F32),
    ("qkv_dn", DN_CONV_DIM, DN_CONV_DIM, BF16),
    ("z_dn", DN_W, DN_W, BF16),
    ("ba_dn", 2 * DN_H, LANES, F32),
    ("q", ATT_W, ATT_W, BF16),
    ("k", ATT_KV_W, ATT_KV_W, BF16),
    ("v", ATT_KV_W, ATT_KV_W, BF16),
)


def _params(sem):
    return pltpu.CompilerParams(dimension_semantics=sem, vmem_limit_bytes=VMEM_LIMIT)


def _sigmoid(x):
    return 1.0 / (1.0 + jnp.exp(-x))


def _silu(x):
    return x * _sigmoid(x)


def _softplus(x):
    return jnp.maximum(x, 0.0) + jnp.log(1.0 + jnp.exp(-jnp.abs(x)))


def _rms(x):
    return x * lax.rsqrt(jnp.mean(x * x, axis=-1, keepdims=True) + EPS)


def _dot(a, b):
    return jnp.dot(a, b, preferred_element_type=F32)


def _dot_nt(a, b):
    return lax.dot_general(a, b, (((1,), (1,)), ((), ())), preferred_element_type=F32)


def _dot_tn(a, b):
    return lax.dot_general(a, b, (((0,), (0,)), ((), ())), preferred_element_type=F32)


def _cumsum_mm(tri, x):
    hi = x.astype(BF16)
    r1 = x - hi.astype(F32)
    mid = r1.astype(BF16)
    lo = (r1 - mid.astype(F32)).astype(BF16)
    return _dot(tri, hi) + _dot(tri, mid) + _dot(tri, lo)


def _ada_kernel(c_ref, w_ref, b_ref, o_ref):
    o_ref[...] = _dot(_silu(c_ref[...]).astype(BF16), w_ref[...]) + b_ref[...]


def _ada(c, w_bf, b):
    nb = c.shape[0]
    rows = -(-nb // SUBLANES) * SUBLANES
    cp = jnp.zeros((rows, D_MODEL), F32).at[:nb].set(c)
    n = w_bf.shape[1]
    tn = 1024
    out = pl.pallas_call(
        _ada_kernel,
        grid=(n // tn,),
        in_specs=[pl.BlockSpec((rows, D_MODEL), lambda j: (0, 0)),
                  pl.BlockSpec((D_MODEL, tn), lambda j: (0, j)),
                  pl.BlockSpec((1, tn), lambda j: (0, j))],
        out_specs=pl.BlockSpec((rows, tn), lambda j: (0, j)),
        out_shape=jax.ShapeDtypeStruct((rows, n), F32),
        compiler_params=_params(("arbitrary",)),
    )(cp, w_bf, b.reshape(1, n))
    return out[:nb].reshape(nb, 6, D_MODEL)


def _inproj_kernel(x_ref, mod_ref, nw_ref, w_ref, *out_refs):
    mod = mod_ref[0]
    h = _rms(x_ref[...]) * nw_ref[...]
    h = h * (1.0 + mod[1:2, :]) + mod[0:1, :]
    hb = h.astype(BF16)
    off = 0
    for o_ref in out_refs:
        width = o_ref.shape[1]
        o_ref[...] = _dot(hb, w_ref[:, off:off + width]).astype(o_ref.dtype)
        off += width


def _inproj(x, mod, nw, w_p, seq_len):
    t = x.shape[0]
    tm = min(512, seq_len)
    per_seq = seq_len // tm
    ntot = w_p.shape[1]
    out_shapes = [jax.ShapeDtypeStruct((t, sw), dt) for (_, _, sw, dt) in _IN_SEGS]
    out_specs = [pl.BlockSpec((tm, sw), lambda i: (i, 0)) for (_, _, sw, _) in _IN_SEGS]
    return pl.pallas_call(
        _inproj_kernel,
        grid=(t // tm,),
        in_specs=[pl.BlockSpec((tm, D_MODEL), lambda i: (i, 0)),
                  pl.BlockSpec((1, 6, D_MODEL), lambda i: (i // per_seq, 0, 0)),
                  pl.BlockSpec((1, D_MODEL), lambda i: (0, 0)),
                  pl.BlockSpec((D_MODEL, ntot), lambda i: (0, 0))],
        out_specs=out_specs,
        out_shape=out_shapes,
        compiler_params=_params(("arbitrary",)),
    )(x, mod, nw.reshape(1, D_MODEL), w_p)


_HALO = 16


def _conv_kernel(prev_ref, x_ref, next_ref, w_ref, b_ref, o_ref, buf_ref, *, per_seq):
    i = pl.program_id(0)
    tm = x_ref.shape[0]
    first = (i % per_seq) == 0
    last = (i % per_seq) == per_seq - 1
    prev = prev_ref[...].astype(F32)
    nxt = next_ref[...].astype(F32)
    buf_ref[0:_HALO, :] = jnp.where(first, jnp.zeros_like(prev), prev)
    buf_ref[_HALO:_HALO + tm, :] = x_ref[...].astype(F32)
    buf_ref[_HALO + tm:2 * _HALO + tm, :] = jnp.where(last, jnp.zeros_like(nxt), nxt)
    w = w_ref[...]
    acc = jnp.zeros((tm, x_ref.shape[1]), F32) + b_ref[...]
    for k in range(CONV_K):
        acc = acc + buf_ref[_HALO - CONV_K // 2 + k:_HALO - CONV_K // 2 + k + tm, :] * w[k:k + 1, :]
    o_ref[...] = _silu(acc).astype(o_ref.dtype)


def _conv(x, w, b, seq_len):
    t, c = x.shape
    tm = min(512, seq_len)
    per_seq = seq_len // tm
    hb = tm // _HALO
    nh = t // _HALO
    wp = jnp.zeros((SUBLANES, c), F32).at[:CONV_K].set(w)
    return pl.pallas_call(
        functools.partial(_conv_kernel, per_seq=per_seq),
        grid=(t // tm,),
        in_specs=[pl.BlockSpec((_HALO, c), lambda i: (jnp.maximum(i * hb - 1, 0), 0)),
                  pl.BlockSpec((tm, c), lambda i: (i, 0)),
                  pl.BlockSpec((_HALO, c), lambda i: (jnp.minimum((i + 1) * hb, nh - 1), 0)),
                  pl.BlockSpec((SUBLANES, c), lambda i: (0, 0)),
                  pl.BlockSpec((1, c), lambda i: (0, 0))],
        out_specs=pl.BlockSpec((tm, c), lambda i: (i, 0)),
        out_shape=jax.ShapeDtypeStruct((t, c), BF16),
        scratch_shapes=[pltpu.VMEM((tm + 2 * _HALO, c), F32)],
        compiler_params=_params(("arbitrary",)),
    )(x, x, x, wp, b.reshape(1, c))


SSD_Q = 128


def _ssd_kernel(*refs, reverse, finalize):
    if finalize:
        (xact_ref, dtp_ref, bias_ref, alog_ref, z_ref, yb_ref, dsk_ref, nw_ref, o_ref, h_ref) = refs
    else:
        (xact_ref, dtp_ref, bias_ref, alog_ref, o_ref, h_ref) = refs
    q = SSD_Q

    @pl.when(pl.program_id(1) == 0)
    def _():
        h_ref[...] = jnp.zeros_like(h_ref)

    r = lax.broadcasted_iota(I32, (q, q), 0)
    s = lax.broadcasted_iota(I32, (q, q), 1)
    mask = (s >= r) if reverse else (r >= s)
    tri = jnp.where(mask, 1.0, 0.0).astype(BF16)
    lo = s < SSD_P

    dt = _softplus(dtp_ref[...] + bias_ref[...])
    dta = dt * (-jnp.exp(alog_ref[...]))
    acs = _cumsum_mm(tri, dta)
    acs_t = acs.T
    dt_t = dt.T
    eacs = jnp.exp(acs)
    tot = acs[0:1, :] if reverse else acs[q - 1:q, :]
    wcols = jnp.exp(tot - acs) * dt

    xact = xact_ref[...]
    ys = []
    for g in range(SSD_G):
        bm = xact[:, SSD_W + g * SSD_N:SSD_W + (g + 1) * SSD_N]
        cm = xact[:, SSD_W + (SSD_G + g) * SSD_N:SSD_W + (SSD_G + g + 1) * SSD_N]
        cb = _dot_nt(cm, bm)
        hg = h_ref[g]
        cmh = _dot(cm, hg.astype(BF16))
        xw_parts, e_parts = [], []
        for pp in range(4):
            j0 = g * 8 + pp * 2
            xpair = xact[:, (g * 4 + pp) * LANES:(g * 4 + pp + 1) * LANES].astype(F32)
            ms = []
            for j in (j0, j0 + 1):
                seg = acs[:, j:j + 1] - acs_t[j:j + 1, :]
                lm = jnp.exp(jnp.where(mask, seg, -jnp.inf))
                ms.append((cb * lm * dt_t[j:j + 1, :]).astype(BF16))
            mcat = jnp.concatenate(ms, axis=1)
            x2 = jnp.concatenate([jnp.where(lo, xpair, 0.0), jnp.where(lo, 0.0, xpair)], axis=0).astype(BF16)
            yd = _dot(mcat, x2)
            esc = jnp.where(lo, jnp.broadcast_to(eacs[:, j0:j0 + 1], (q, LANES)),
                            jnp.broadcast_to(eacs[:, j0 + 1:j0 + 2], (q, LANES)))
            ys.append(yd + cmh[:, pp * LANES:(pp + 1) * LANES] * esc)
            wsc = jnp.where(lo, jnp.broadcast_to(wcols[:, j0:j0 + 1], (q, LANES)),
                            jnp.broadcast_to(wcols[:, j0 + 1:j0 + 2], (q, LANES)))
            xw_parts.append((xpair * wsc).astype(BF16))
            e_parts.append(esc)
        xw = jnp.concatenate(xw_parts, axis=1)
        e_all = jnp.concatenate(e_parts, axis=1)
        dec = e_all[0:1, :] if reverse else e_all[q - 1:q, :]
        h_ref[g] = hg * dec + _dot_tn(bm, xw)
    y = jnp.concatenate(ys, axis=1)

    if not finalize:
        o_ref[...] = y
        return
    xs = xact[:, :SSD_W].astype(F32)
    y = y + yb_ref[...] + dsk_ref[...] * xs
    y = y * _silu(z_ref[...].astype(F32))
    half = SSD_W // SSD_G
    nw = nw_ref[...]
    outs = [_rms(y[:, g * half:(g + 1) * half]) * nw[:, g * half:(g + 1) * half] for g in range(SSD_G)]
    o_ref[...] = jnp.concatenate(outs, axis=1).astype(o_ref.dtype)


def _ssd(xact, dtp, bias, alog, batch, seq_len, reverse, z=None, yb=None, dskip=None, nw=None):
    t = xact.shape[0]
    nc = seq_len // SSD_Q
    finalize = z is not None

    def row(b, c):
        cc = (nc - 1 - c) if reverse else c
        return (b * nc + cc, 0)

    def const(b, c):
        return (0, 0)

    in_specs = [pl.BlockSpec((SSD_Q, SSD_CONV_DIM), row),
                pl.BlockSpec((SSD_Q, LANES), row),
                pl.BlockSpec((1, LANES), const),
                pl.BlockSpec((1, LANES), const)]
    args = [xact, dtp, bias, alog]
    if finalize:
        in_specs += [pl.BlockSpec((SSD_Q, SSD_W), row), pl.BlockSpec((SSD_Q, SSD_W), row),
                     pl.BlockSpec((1, SSD_W), const), pl.BlockSpec((1, SSD_W), const)]
        args += [z, yb, dskip, nw]
    return pl.pallas_call(
        functools.partial(_ssd_kernel, reverse=reverse, finalize=finalize),
        grid=(batch, nc),
        in_specs=in_specs,
        out_specs=pl.BlockSpec((SSD_Q, SSD_W), row),
        out_shape=jax.ShapeDtypeStruct((t, SSD_W), BF16 if finalize else F32),
        scratch_shapes=[pltpu.VMEM((SSD_G, SSD_N, SSD_W // SSD_G), F32)],
        compiler_params=_params(("arbitrary", "arbitrary")),
    )(*args)


DN_BLK = 128
DN_C = 64


def _dn_kernel(*refs, reverse, finalize):
    if finalize:
        (qkv_ref, ba_ref, dtb_ref, alog_ref, z_ref, ob_ref, nw_ref, o_ref, s_ref) = refs
    else:
        (qkv_ref, ba_ref, dtb_ref, alog_ref, o_ref, s_ref) = refs
    n = DN_BLK

    @pl.when(pl.program_id(1) == 0)
    def _():
        s_ref[...] = jnp.zeros_like(s_ref)

    r = lax.broadcasted_iota(I32, (n, n), 0)
    s = lax.broadcasted_iota(I32, (n, n), 1)
    same = (r >= DN_C) == (s >= DN_C)
    incl = same & ((s >= r) if reverse else (r >= s))
    strict = same & ((s > r) if reverse else (r > s))
    tri = jnp.where(incl, 1.0, 0.0).astype(BF16)
    eye = jnp.where(r == s, 1.0, 0.0)

    ba = ba_ref[...]
    beta = _sigmoid(ba)
    gl = -jnp.exp(alog_ref[...]) * _softplus(ba + dtb_ref[...])
    gcs = _cumsum_mm(tri, gl)
    gcs_t = gcs.T
    if reverse:
        t0, t1 = gcs[0:1, :], gcs[DN_C:DN_C + 1, :]
    else:
        t0, t1 = gcs[DN_C - 1:DN_C, :], gcs[n - 1:n, :]
    tot = jnp.where(r < DN_C, t0, t1)
    eg = jnp.exp(gcs)
    ekd = jnp.exp(tot - gcs)
    dec0 = jnp.exp(t0)
    dec1 = jnp.exp(t1)

    qkv = qkv_ref[...]
    outs = []
    zeros_c = jnp.zeros((DN_C, DN_HD), F32)
    for h in range(DN_H):
        la = DN_H + h
        qh = qkv[:, h * DN_HD:(h + 1) * DN_HD].astype(F32)
        kh = qkv[:, DN_W + h * DN_HD:DN_W + (h + 1) * DN_HD].astype(F32)
        vh = qkv[:, 2 * DN_W + h * DN_HD:2 * DN_W + (h + 1) * DN_HD].astype(F32)
        qh = qh * lax.rsqrt(jnp.sum(qh * qh, axis=-1, keepdims=True) + EPS) * (DN_HD ** -0.5)
        kh = kh * lax.rsqrt(jnp.sum(kh * kh, axis=-1, keepdims=True) + EPS)
        bcol = beta[:, h:h + 1]
        gcol = gcs[:, la:la + 1]
        decay = jnp.exp(jnp.where(incl, gcol - gcs_t[la:la + 1, :], -jnp.inf))
        kb = kh * bcol
        khb = kh.astype(BF16)
        nm = jnp.where(strict, _dot_nt(kb.astype(BF16), khb) * decay, 0.0)
        p = -nm
        inv = eye + p
        for _ in range(5):
            pb = p.astype(BF16)
            p = _dot(pb, pb)
            inv = inv + _dot(inv.astype(BF16), p.astype(BF16))
        invb = inv.astype(BF16)
        u = _dot(invb, (vh * bcol).astype(BF16))
        w = _dot(invb, (kb * eg[:, la:la + 1]).astype(BF16)).astype(BF16)
        qk = (_dot_nt(qh.astype(BF16), khb) * decay).astype(BF16)
        qg = (qh * eg[:, la:la + 1]).astype(BF16)
        kd = (kh * ekd[:, la:la + 1]).astype(BF16)
        st = s_ref[h]
        o_parts = [None, None]
        for ci in ((1, 0) if reverse else (0, 1)):
            rows = slice(ci * DN_C, (ci + 1) * DN_C)
            sb = st.astype(BF16)
            vnew = u[rows] - _dot(w[rows], sb)
            vpad = jnp.concatenate([vnew, zeros_c] if ci == 0 else [zeros_c, vnew], axis=0).astype(BF16)
            o_parts[ci] = _dot(qg[rows], sb) + _dot(qk[rows], vpad)
            dec = (dec0 if ci == 0 else dec1)[:, la:la + 1]
            st = st * dec + _dot_tn(kd[rows], vnew.astype(BF16))
        s_ref[h] = st
        o = jnp.concatenate(o_parts, axis=0)
        if finalize:
            o = o + ob_ref[:, h * DN_HD:(h + 1) * DN_HD]
            o = _rms(o) * nw_ref[...] * _silu(z_ref[:, h * DN_HD:(h + 1) * DN_HD].astype(F32))
        outs.append(o)
    o_ref[...] = jnp.concatenate(outs, axis=1).astype(o_ref.dtype)


def _dn(qkv, ba, dtb, alog, batch, seq_len, reverse, z=None, ob=None, nw=None):
    t = qkv.shape[0]
    nb = seq_len // DN_BLK
    finalize = z is not None

    def row(b, c):
        cc = (nb - 1 - c) if reverse else c
        return (b * nb + cc, 0)

    def const(b, c):
        return (0, 0)

    in_specs = [pl.BlockSpec((DN_BLK, DN_CONV_DIM), row),
                pl.BlockSpec((DN_BLK, LANES), row),
                pl.BlockSpec((1, LANES), const),
                pl.BlockSpec((1, LANES), const)]
    args = [qkv, ba, dtb, alog]
    if finalize:
        in_specs += [pl.BlockSpec((DN_BLK, DN_W), row), pl.BlockSpec((DN_BLK, DN_W), row),
                     pl.BlockSpec((1, DN_HD), const)]
        args += [z, ob, nw]
    return pl.pallas_call(
        functools.partial(_dn_kernel, reverse=reverse, finalize=finalize),
        grid=(batch, nb),
        in_specs=in_specs,
        out_specs=pl.BlockSpec((DN_BLK, DN_W), row),
        out_shape=jax.ShapeDtypeStruct((t, DN_W), BF16 if finalize else F32),
        scratch_shapes=[pltpu.VMEM((DN_H, DN_HD, DN_HD), F32)],
        compiler_params=_params(("arbitrary", "arbitrary")),
    )(*args)


def _rope_tables(seq_len):
    rows = seq_len // GRID_W
    row_idx = jnp.repeat(jnp.arange(rows, dtype=F32), GRID_W)
    col_idx = jnp.tile(jnp.arange(GRID_W, dtype=F32), rows)
    axis_dim = ATT_HD // 2
    inv_freq = jnp.power(ROPE_THETA, -jnp.arange(0, axis_dim, 2, dtype=F32) / axis_dim)
    ra = row_idx[:, None] * inv_freq
    ca = col_idx[:, None] * inv_freq
    cos_t = jnp.concatenate([jnp.cos(ra), jnp.cos(ra), jnp.cos(ca), jnp.cos(ca)], axis=1)
    sin_t = jnp.concatenate([-jnp.sin(ra), jnp.sin(ra), -jnp.sin(ca), jnp.sin(ca)], axis=1)
    return cos_t, sin_t


def _qkprep_kernel(q_ref, k_ref, cos_ref, sin_ref, qn_ref, kn_ref, qo_ref, ko_ref):
    cos_t = cos_ref[...]
    sin_t = sin_ref[...]
    lane = lax.broadcasted_iota(I32, cos_t.shape, 1)
    low = (lane & (ATT_HD // 4)) == 0

    def prep(x, nw, scale):
        x = _rms(x.astype(F32)) * nw
        partner = jnp.where(low, pltpu.roll(x, ATT_HD - ATT_HD // 4, 1), pltpu.roll(x, ATT_HD // 4, 1))
        return (x * cos_t + partner * sin_t) * scale

    qs = [prep(q_ref[:, h * ATT_HD:(h + 1) * ATT_HD], qn_ref[...], ATT_HD ** -0.5) for h in range(ATT_H)]
    ks = [prep(k_ref[:, h * ATT_HD:(h + 1) * ATT_HD], kn_ref[...], 1.0) for h in range(ATT_KV)]
    qo_ref[...] = jnp.concatenate(qs, axis=1).astype(qo_ref.dtype)
    ko_ref[...] = jnp.concatenate(ks, axis=1).astype(ko_ref.dtype)


def _qkprep(q, k, cos_t, sin_t, qn, kn, seq_len):
    t = q.shape[0]
    tm = min(512, seq_len)
    per_seq = seq_len // tm
    return pl.pallas_call(
        _qkprep_kernel,
        grid=(t // tm,),
        in_specs=[pl.BlockSpec((tm, ATT_W), lambda i: (i, 0)),
                  pl.BlockSpec((tm, ATT_KV_W), lambda i: (i, 0)),
                  pl.BlockSpec((tm, ATT_HD), lambda i: (i % per_seq, 0)),
                  pl.BlockSpec((tm, ATT_HD), lambda i: (i % per_seq, 0)),
                  pl.BlockSpec((1, ATT_HD), lambda i: (0, 0)),
                  pl.BlockSpec((1, ATT_HD), lambda i: (0, 0))],
        out_specs=[pl.BlockSpec((tm, ATT_W), lambda i: (i, 0)),
                   pl.BlockSpec((tm, ATT_KV_W), lambda i: (i, 0))],
        out_shape=[jax.ShapeDtypeStruct((t, ATT_W), BF16), jax.ShapeDtypeStruct((t, ATT_KV_W), BF16)],
        compiler_params=_params(("arbitrary",)),
    )(q, k, cos_t, sin_t, qn.reshape(1, ATT_HD), kn.reshape(1, ATT_HD))


def _flash_kernel(q_ref, k_ref, v_ref, o_ref, m_ref, l_ref, acc_ref):
    kv = pl.program_id(3)
    tq = q_ref.shape[0]

    @pl.when(kv == 0)
    def _():
        m_ref[...] = jnp.full_like(m_ref, -jnp.inf)
        l_ref[...] = jnp.zeros_like(l_ref)
        acc_ref[...] = jnp.zeros_like(acc_ref)

    q2 = jnp.concatenate([q_ref[:, :ATT_HD], q_ref[:, ATT_HD:]], axis=0)
    sc = _dot_nt(q2, k_ref[...])
    m_prev = m_ref[...]
    m_new = jnp.maximum(m_prev, jnp.max(sc, axis=-1, keepdims=True))
    p = jnp.exp(sc - m_new)
    alpha = jnp.exp(m_prev - m_new)
    l_ref[...] = alpha * l_ref[...] + jnp.sum(p, axis=-1, keepdims=True)
    acc_ref[...] = alpha * acc_ref[...] + _dot(p.astype(BF16), v_ref[...])
    m_ref[...] = m_new

    @pl.when(kv == pl.num_programs(3) - 1)
    def _():
        out = acc_ref[...] / l_ref[...]
        o_ref[...] = jnp.concatenate([out[:tq], out[tq:]], axis=1).astype(o_ref.dtype)


def _flash(qh, kh, v, batch, seq_len):
    t = qh.shape[0]
    tq = min(256, seq_len)
    tk = min(1024, seq_len)
    nq = seq_len // tq
    nk = seq_len // tk
    rep_w = (ATT_H // ATT_KV) * ATT_HD
    return pl.pallas_call(
        _flash_kernel,
        grid=(batch, ATT_KV, nq, nk),
        in_specs=[pl.BlockSpec((tq, rep_w), lambda b, g, i, j: (b * nq + i, g)),
                  pl.BlockSpec((tk, ATT_HD), lambda b, g, i, j: (b * nk + j, g)),
                  pl.BlockSpec((tk, ATT_HD), lambda b, g, i, j: (b * nk + j, g))],
        out_specs=pl.BlockSpec((tq, rep_w), lambda b, g, i, j: (b * nq + i, g)),
        out_shape=jax.ShapeDtypeStruct((t, ATT_W), BF16),
        scratch_shapes=[pltpu.VMEM((2 * tq, 1), F32), pltpu.VMEM((2 * tq, 1), F32),
                        pltpu.VMEM((2 * tq, ATT_HD), F32)],
        compiler_params=_params(("arbitrary",) * 4),
    )(qh, kh, v)


def _outproj_kernel(ys_ref, yd_ref, ya_ref, x_ref, mod_ref, nwp_ref, nwf_ref, wo_ref, wr_ref,
                    x1_ref, h2_ref, aff_ref):
    m = (_dot(ys_ref[...], wo_ref[0:SSD_W, :]) + _dot(yd_ref[...], wo_ref[SSD_W:SSD_W + DN_W, :])
         + _dot(ya_ref[...], wo_ref[SSD_W + DN_W:, :]))
    mod = mod_ref[0]
    x1 = x_ref[...] + mod[2:3, :] * (_rms(m) * nwp_ref[...])
    x1_ref[...] = x1
    h2 = _rms(x1) * nwf_ref[...] * (1.0 + mod[4:5, :]) + mod[3:4, :]
    h2_ref[...] = h2.astype(h2_ref.dtype)
    logits = jnp.dot(h2, wr_ref[...], preferred_element_type=F32, precision=lax.Precision.HIGHEST)
    lane = lax.broadcasted_iota(I32, logits.shape, 1)
    logits = jnp.where(lane < N_EXPERTS, logits, -jnp.inf)
    ex = jnp.exp(logits - jnp.max(logits, axis=-1, keepdims=True))
    aff = ex / jnp.sum(ex, axis=-1, keepdims=True)
    aff_ref[...] = aff.T[0:N_EXPERTS, :]


def _outproj(ys, yd, ya, x, mod, nwp, nwf, wo_bf, wr_p, seq_len):
    t = x.shape[0]
    tm = min(512, seq_len)
    per_seq = seq_len // tm
    row = lambda i: (i, 0)
    const = lambda i: (0, 0)
    return pl.pallas_call(
        _outproj_kernel,
        grid=(t // tm,),
        in_specs=[pl.BlockSpec((tm, SSD_W), row), pl.BlockSpec((tm, DN_W), row), pl.BlockSpec((tm, ATT_W), row),
                  pl.BlockSpec((tm, D_MODEL), row),
                  pl.BlockSpec((1, 6, D_MODEL), lambda i: (i // per_seq, 0, 0)),
                  pl.BlockSpec((1, D_MODEL), const), pl.BlockSpec((1, D_MODEL), const),
                  pl.BlockSpec((2 * D_MODEL, D_MODEL), const),
                  pl.BlockSpec((D_MODEL, LANES), const)],
        out_specs=[pl.BlockSpec((tm, D_MODEL), row), pl.BlockSpec((tm, D_MODEL), row),
                   pl.BlockSpec((N_EXPERTS, tm), lambda i: (0, i))],
        out_shape=[jax.ShapeDtypeStruct((t, D_MODEL), F32), jax.ShapeDtypeStruct((t, D_MODEL), BF16),
                   jax.ShapeDtypeStruct((N_EXPERTS, t), F32)],
        compiler_params=_params(("arbitrary",)),
    )(ys, yd, ya, x, mod, nwp.reshape(1, D_MODEL), nwf.reshape(1, D_MODEL), wo_bf, wr_p)


def _route_kernel(aff_ref, pos_ref, offs_ref, *, cap):
    ne, t = aff_ref.shape
    nb = t // LANES
    capf = float(cap)

    def bits_of(x):
        return pltpu.bitcast(x, I32)

    def bis(i, thr):
        cand = thr | jnp.left_shift(jnp.int32(1), 30 - i)
        cnt = jnp.sum(jnp.where(bits_of(aff_ref[...]) >= cand, 1.0, 0.0), axis=1, keepdims=True)
        return jnp.where(cnt >= capf, cand, thr)

    thr = lax.fori_loop(0, 31, bis, jnp.zeros((ne, 1), I32))
    n_gt = jnp.sum(jnp.where(bits_of(aff_ref[...]) > thr, 1.0, 0.0), axis=1, keepdims=True)
    need_eq = capf - n_gt

    r = lax.broadcasted_iota(I32, (LANES, LANES), 0)
    s = lax.broadcasted_iota(I32, (LANES, LANES), 1)
    triu = jnp.where(r <= s, 1.0, 0.0).astype(BF16)
    lane_nb = lax.broadcasted_iota(I32, (ne, nb), 1)

    offs_ref[...] = jnp.zeros_like(offs_ref)

    def tile(i, carry):
        run_sel, run_eq = carry
        start = pl.multiple_of(i * LANES, LANES)
        b = bits_of(aff_ref[:, pl.ds(start, LANES)])
        gt = b > thr
        eq = jnp.where(b == thr, 1.0, 0.0)
        eq_rank = _dot(eq.astype(BF16), triu) - eq + run_eq
        sel = jnp.where(gt | ((eq > 0.0) & (eq_rank < need_eq)), 1.0, 0.0)
        pos = _dot(sel.astype(BF16), triu) - sel + run_sel
        pos_ref[:, pl.ds(start, LANES)] = jnp.where(sel > 0.0, pos, -1.0).astype(I32)
        offs_ref[...] = jnp.where(lane_nb == i, run_sel.astype(I32), offs_ref[...])
        return (run_sel + jnp.sum(sel, axis=1, keepdims=True),
                run_eq + jnp.sum(eq, axis=1, keepdims=True))

    zero = jnp.zeros((ne, 1), F32)
    lax.fori_loop(0, nb, tile, (zero, zero))


def _route(aff_t, cap):
    ne, t = aff_t.shape
    nb = t // LANES
    return pl.pallas_call(
        functools.partial(_route_kernel, cap=cap),
        out_shape=[jax.ShapeDtypeStruct((ne, t), I32), jax.ShapeDtypeStruct((ne, nb), I32)],
        compiler_params=pltpu.CompilerParams(vmem_limit_bytes=VMEM_LIMIT),
    )(aff_t)


MOE_SUB = 128
MOE_WIN = MOE_SUB + SUBLANES
MOE_TILE = 256


def _moe_kernel(offs_ref, h_ref, pos_ref, gate_ref, wg_ref, wu_ref, wd_ref, o_ref,
                xe_ref, ye_ref, gb_ref, *, cap, nsub):
    sb = pl.program_id(0)
    e = pl.program_id(1)
    f = pl.program_id(2)
    nsb = pl.num_programs(0)
    nf = pl.num_programs(2)
    nblk = nsb * nsub
    start = offs_ref[e, sb * nsub]
    end = jnp.where(sb == nsb - 1, cap, offs_ref[e, jnp.minimum((sb + 1) * nsub, nblk - 1)])
    ntiles = (end - start + MOE_TILE - 1) // MOE_TILE

    @pl.when((sb == 0) & (e == 0) & (f == 0))
    def _():
        xe_ref[...] = jnp.zeros_like(xe_ref)
        ye_ref[...] = jnp.zeros_like(ye_ref)
        gb_ref[...] = jnp.zeros_like(gb_ref)

    @pl.when((e == 0) & (f == 0))
    def _():
        o_ref[...] = jnp.zeros_like(o_ref)

    rid = lax.broadcasted_iota(I32, (MOE_WIN, MOE_SUB), 0)

    def window(j):
        base = offs_ref[e, sb * nsub + j] - start
        a8 = pl.multiple_of((base // SUBLANES) * SUBLANES, SUBLANES)
        posrow = pos_ref[0, :, j * MOE_SUB:(j + 1) * MOE_SUB]
        onehot = (posrow - (start + a8) == rid) & (posrow >= 0)
        return base, a8, onehot

    @pl.when(f == 0)
    def _():
        for j in range(nsub):
            base, a8, onehot = window(j)
            comp = _dot(jnp.where(onehot, 1.0, 0.0).astype(BF16), h_ref[j * MOE_SUB:(j + 1) * MOE_SUB, :])
            gate = jnp.sum(jnp.where(onehot, gate_ref[0, :, j * MOE_SUB:(j + 1) * MOE_SUB], 0.0),
                           axis=1, keepdims=True)
            keep = rid[:, 0:1] < (base - a8)
            xe_ref[pl.ds(a8, MOE_WIN), :] = jnp.where(keep, xe_ref[pl.ds(a8, MOE_WIN), :], comp)
            gb_ref[pl.ds(a8, MOE_WIN), :] = jnp.where(keep, gb_ref[pl.ds(a8, MOE_WIN), :],
                                                      jnp.broadcast_to(gate, (MOE_WIN, LANES)))

    def ffn_tile(i, carry):
        r0 = pl.multiple_of(i * MOE_TILE, MOE_TILE)
        x = xe_ref[pl.ds(r0, MOE_TILE), :].astype(BF16)
        a = _dot(x, wg_ref[0])
        hid = (_silu(a) * _dot(x, wu_ref[0])).astype(BF16)
        y = _dot(hid, wd_ref[0])
        ye_ref[pl.ds(r0, MOE_TILE), :] = jnp.where(f == 0, y, ye_ref[pl.ds(r0, MOE_TILE), :] + y)
        return carry

    lax.fori_loop(0, ntiles, ffn_tile, 0)

    @pl.when(f == nf - 1)
    def _():
        for j in range(nsub):
            _, a8, onehot = window(j)
            yw = (ye_ref[pl.ds(a8, MOE_WIN), :] * gb_ref[pl.ds(a8, MOE_WIN), 0:1]).astype(BF16)
            o_ref[j * MOE_SUB:(j + 1) * MOE_SUB, :] += _dot_tn(jnp.where(onehot, 1.0, 0.0).astype(BF16), yw)


def _moe(offs, h2, pos, aff_t, wg, wu, wd, cap, sblk, fchunk):
    t = h2.shape[0]
    nsb = t // sblk
    nsub = sblk // MOE_SUB
    nf = EXPERT_FF // fchunk
    rows = sblk + MOE_TILE + MOE_SUB
    grid_spec = pltpu.PrefetchScalarGridSpec(
        num_scalar_prefetch=1,
        grid=(nsb, N_EXPERTS, nf),
        in_specs=[pl.BlockSpec((sblk, D_MODEL), lambda s, e, f, o: (s, 0)),
                  pl.BlockSpec((1, 1, sblk), lambda s, e, f, o: (e, 0, s)),
                  pl.BlockSpec((1, 1, sblk), lambda s, e, f, o: (e, 0, s)),
                  pl.BlockSpec((1, D_MODEL, fchunk), lambda s, e, f, o: (e, 0, f)),
                  pl.BlockSpec((1, D_MODEL, fchunk), lambda s, e, f, o: (e, 0, f)),
                  pl.BlockSpec((1, fchunk, D_MODEL), lambda s, e, f, o: (e, f, 0))],
        out_specs=pl.BlockSpec((sblk, D_MODEL), lambda s, e, f, o: (s, 0)),
        scratch_shapes=[pltpu.VMEM((rows, D_MODEL), F32), pltpu.VMEM((rows, D_MODEL), F32),
                        pltpu.VMEM((rows, LANES), F32)],
    )
    return pl.pallas_call(
        functools.partial(_moe_kernel, cap=cap, nsub=nsub),
        grid_spec=grid_spec,
        out_shape=jax.ShapeDtypeStruct((t, D_MODEL), F32),
        compiler_params=_params(("arbitrary",) * 3),
    )(offs, h2, pos.reshape(N_EXPERTS, 1, t), aff_t.reshape(N_EXPERTS, 1, t), wg, wu, wd)


def _post_kernel(x_ref, f_ref, mod_ref, nw_ref, o_ref):
    mod = mod_ref[0]
    o_ref[...] = x_ref[...] + mod[5:6, :] * (_rms(f_ref[...]) * nw_ref[...])


def _post(x1, f, mod, nw, seq_len):
    t = x1.shape[0]
    tm = min(512, seq_len)
    per_seq = seq_len // tm
    row = lambda i: (i, 0)
    return pl.pallas_call(
        _post_kernel,
        grid=(t // tm,),
        in_specs=[pl.BlockSpec((tm, D_MODEL), row), pl.BlockSpec((tm, D_MODEL), row),
                  pl.BlockSpec((1, 6, D_MODEL), lambda i: (i // per_seq, 0, 0)),
                  pl.BlockSpec((1, D_MODEL), lambda i: (0, 0))],
        out_specs=pl.BlockSpec((tm, D_MODEL), row),
        out_shape=jax.ShapeDtypeStruct((t, D_MODEL), F32),
        compiler_params=_params(("arbitrary",)),
    )(x1, f, mod, nw.reshape(1, D_MODEL))


def _pad_lanes(v, offset=0):
    out = jnp.zeros((1, LANES), F32)
    return out.at[0, offset:offset + v.shape[0]].set(v.astype(F32))


def _prep_layer(p, l):
    w_in = p["w_in"][l]
    cols, off = [], 0
    for (_, width, stored, _) in _IN_SEGS:
        seg = w_in[:, off:off + width]
        if stored != width:
            seg = jnp.pad(seg, ((0, 0), (0, stored - width)))
        cols.append(seg)
        off += width
    q = {
        "w_ada": p["w_ada"][l].astype(BF16), "b_ada": p["b_ada"][l],
        "w_in": jnp.concatenate(cols, axis=1).astype(BF16),
        "w_out": p["w_out"][l].astype(BF16),
        "w_router": jnp.pad(p["w_router"][l], ((0, 0), (0, LANES - N_EXPERTS))),
        "w_gate": p["w_gate"][l].astype(BF16), "w_up": p["w_up"][l].astype(BF16),
        "w_down": p["w_down"][l].astype(BF16),
        "ssd_bias": [_pad_lanes(p["ssd_dt_bias"][l, d]) for d in range(2)],
        "ssd_alog": [_pad_lanes(p["ssd_a_log"][l, d]) for d in range(2)],
        "ssd_d": jnp.repeat(p["ssd_d"][l], SSD_P).reshape(1, SSD_W),
        "ssd_norm": p["ssd_norm"][l].reshape(1, SSD_W),
        "dn_bias": [_pad_lanes(p["dn_dt_bias"][l, d], DN_H) for d in range(2)],
        "dn_alog": [_pad_lanes(p["dn_a_log"][l, d], DN_H) for d in range(2)],
        "dn_norm": p["dn_norm"][l].reshape(1, DN_HD),
    }
    for name in ("norm_mix_pre", "norm_mix_post", "norm_ffn_pre", "norm_ffn_post", "conv_ssd_w", "conv_ssd_b",
                 "conv_dn_w", "q_norm", "k_norm"):
        q[name] = p[name][l]
    return q


def _trunk(x, c, layers, sblk, fchunk):
    batch, seq_len, _ = x.shape
    t = batch * seq_len
    cap = EC_CAPACITY * t // N_EXPERTS
    cos_t, sin_t = _rope_tables(seq_len)
    x = x.reshape(t, D_MODEL)
    for q in layers:
        mod = _ada(c, q["w_ada"], q["b_ada"])
        z_ssd, xbc, dtp, qkv_dn, z_dn, ba_dn, qa, ka, va = _inproj(x, mod, q["norm_mix_pre"], q["w_in"], seq_len)
        xact = _conv(xbc, q["conv_ssd_w"], q["conv_ssd_b"], seq_len)
        qkvact = _conv(qkv_dn, q["conv_dn_w"], jnp.zeros((DN_CONV_DIM,), F32), seq_len)
        yb = _ssd(xact, dtp, q["ssd_bias"][1], q["ssd_alog"][1], batch, seq_len, True)
        y_ssd = _ssd(xact, dtp, q["ssd_bias"][0], q["ssd_alog"][0], batch, seq_len, False,
                     z=z_ssd, yb=yb, dskip=q["ssd_d"], nw=q["ssd_norm"])
        ob = _dn(qkvact, ba_dn, q["dn_bias"][1], q["dn_alog"][1], batch, seq_len, True)
        y_dn = _dn(qkvact, ba_dn, q["dn_bias"][0], q["dn_alog"][0], batch, seq_len, False,
                   z=z_dn, ob=ob, nw=q["dn_norm"])
        qh, kh = _qkprep(qa, ka, cos_t, sin_t, q["q_norm"], q["k_norm"], seq_len)
        y_att = _flash(qh, kh, va, batch, seq_len)
        x1, h2, aff_t = _outproj(y_ssd, y_dn, y_att, x, mod, q["norm_mix_post"], q["norm_ffn_pre"],
                                 q["w_out"], q["w_router"], seq_len)
        pos, offs = _route(aff_t, cap)
        f = _moe(offs, h2, pos, aff_t, q["w_gate"], q["w_up"], q["w_down"], cap, sblk, fchunk)
        x = _post(x1, f, mod, q["norm_ffn_post"], seq_len)
    return x.reshape(batch, seq_len, D_MODEL)


def kernel(x_prompt, x_sample, c_prompt, c_sample, w_ada, b_ada, norm_mix_pre, norm_mix_post, w_in, conv_ssd_w, conv_ssd_b, ssd_dt_bias, ssd_a_log, ssd_d, ssd_norm, conv_dn_w, dn_dt_bias, dn_a_log, dn_norm, q_norm, k_norm, w_out, norm_ffn_pre, norm_ffn_post, w_router, w_gate, w_up, w_down):
    p = dict(w_ada=w_ada, b_ada=b_ada, norm_mix_pre=norm_mix_pre, norm_mix_post=norm_mix_post,
             w_in=w_in, conv_ssd_w=conv_ssd_w, conv_ssd_b=conv_ssd_b, ssd_dt_bias=ssd_dt_bias,
             ssd_a_log=ssd_a_log, ssd_d=ssd_d, ssd_norm=ssd_norm, conv_dn_w=conv_dn_w,
             dn_dt_bias=dn_dt_bias, dn_a_log=dn_a_log, dn_norm=dn_norm, q_norm=q_norm, k_norm=k_norm,
             w_out=w_out, norm_ffn_pre=norm_ffn_pre, norm_ffn_post=norm_ffn_post,
             w_router=w_router, w_gate=w_gate, w_up=w_up, w_down=w_down)
    layers = [_prep_layer(p, l) for l in range(w_in.shape[0])]
    sblk = min(2048, x_prompt.shape[0] * x_prompt.shape[1])
    y_prompt = _trunk(x_prompt, c_prompt, layers, sblk, 512)
    sblk = min(2048, x_sample.shape[0] * x_sample.shape[1])
    y_sample = _trunk(x_sample, c_sample, layers, sblk, 512)
    return (y_prompt, y_sample)
```

```python
import functools

import jax
import jax.numpy as jnp
import numpy as np
from jax import lax
from jax.experimental import pallas as pl
from jax.experimental.pallas import tpu as pltpu

F32 = jnp.float32
BF16 = jnp.bfloat16
I32 = jnp.int32

D_MODEL = 1024
SSD_W = 1024
SSD_P = 64
SSD_H = 16
SSD_G = 2
SSD_N = 128
SSD_CONV_DIM = SSD_W + 2 * SSD_G * SSD_N
DN_W = 512
DN_HD = 128
DN_H = 4
DN_CONV_DIM = 3 * DN_W
ATT_W = 512
ATT_HD = 128
ATT_H = 4
ATT_KV = 2
ATT_KV_W = ATT_KV * ATT_HD
GRID_W = 64
ROPE_THETA = 10000.0
N_EXPERTS = 16
EC_CAPACITY = 2
EXPERT_FF = 2048
EPS = 1e-6
CONV_K = 5

LANES = 128
SUBLANES = 8
VMEM_LIMIT = 56 * 1024 * 1024

_IN_SEGS = (
    ("z_ssd", SSD_W, SSD_W, BF16),
    ("xbc", SSD_CONV_DIM, SSD_CONV_DIM, BF16),
    ("dt", SSD_H, LANES, F32),
    ("qkv_dn", DN_CONV_DIM, DN_CONV_DIM, BF16),
    ("z_dn", DN_W, DN_W, BF16),
    ("ba_dn", 2 * DN_H, LANES, F32),
    ("q", ATT_W, ATT_W, BF16),
    ("k", ATT_KV_W, ATT_KV_W, BF16),
    ("v", ATT_KV_W, ATT_KV_W, BF16),
)


def _params(sem):
    return pltpu.CompilerParams(dimension_semantics=sem, vmem_limit_bytes=VMEM_LIMIT)


def _sigmoid(x):
    return 1.0 / (1.0 + jnp.exp(-x))


def _silu(x):
    return x * _sigmoid(x)


def _softplus(x):
    return jnp.maximum(x, 0.0) + jnp.log(1.0 + jnp.exp(-jnp.abs(x)))


def _rms(x):
    return x * lax.rsqrt(jnp.mean(x * x, axis=-1, keepdims=True) + EPS)


def _dot(a, b):
    return jnp.dot(a, b, preferred_element_type=F32)


def _dot_nt(a, b):
    return lax.dot_general(a, b, (((1,), (1,)), ((), ())), preferred_element_type=F32)


def _dot_tn(a, b):
    return lax.dot_general(a, b, (((0,), (0,)), ((), ())), preferred_element_type=F32)


def _cumsum_mm(tri, x):
    hi = x.astype(BF16)
    r1 = x - hi.astype(F32)
    mid = r1.astype(BF16)
    lo = (r1 - mid.astype(F32)).astype(BF16)
    return _dot(tri, hi) + _dot(tri, mid) + _dot(tri, lo)


def _ada_kernel(c_ref, w_ref, b_ref, o_ref):
    o_ref[...] = _dot(_silu(c_ref[...]).astype(BF16), w_ref[...]) + b_ref[...]


def _ada(c, w_bf, b):
    nb = c.shape[0]
    rows = -(-nb // SUBLANES) * SUBLANES
    cp = jnp.zeros((rows, D_MODEL), F32).at[:nb].set(c)
    n = w_bf.shape[1]
    tn = 1024
    out = pl.pallas_call(
        _ada_kernel,
        name="ada",
        grid=(n // tn,),
        in_specs=[pl.BlockSpec((rows, D_MODEL), lambda j: (0, 0)),
                  pl.BlockSpec((D_MODEL, tn), lambda j: (0, j)),
                  pl.BlockSpec((1, tn), lambda j: (0, j))],
        out_specs=pl.BlockSpec((rows, tn), lambda j: (0, j)),
        out_shape=jax.ShapeDtypeStruct((rows, n), F32),
        compiler_params=_params(("arbitrary",)),
    )(cp, w_bf, b.reshape(1, n))
    return out[:nb].reshape(nb, 6, D_MODEL)


def _inproj_kernel(x_ref, mod_ref, nw_ref, w_ref, *out_refs):
    mod = mod_ref[0]
    h = _rms(x_ref[...]) * nw_ref[...]
    h = h * (1.0 + mod[1:2, :]) + mod[0:1, :]
    hb = h.astype(BF16)
    off = 0
    for o_ref in out_refs:
        width = o_ref.shape[1]
        o_ref[...] = _dot(hb, w_ref[:, off:off + width]).astype(o_ref.dtype)
        off += width


def _inproj(x, mod, nw, w_p, seq_len):
    t = x.shape[0]
    tm = min(512, seq_len)
    per_seq = seq_len // tm
    ntot = w_p.shape[1]
    out_shapes = [jax.ShapeDtypeStruct((t, sw), dt) for (_, _, sw, dt) in _IN_SEGS]
    out_specs = [pl.BlockSpec((tm, sw), lambda i: (i, 0)) for (_, _, sw, _) in _IN_SEGS]
    return pl.pallas_call(
        _inproj_kernel,
        name="inproj",
        grid=(t // tm,),
        in_specs=[pl.BlockSpec((tm, D_MODEL), lambda i: (i, 0)),
                  pl.BlockSpec((1, 6, D_MODEL), lambda i: (i // per_seq, 0, 0)),
                  pl.BlockSpec((1, D_MODEL), lambda i: (0, 0)),
                  pl.BlockSpec((D_MODEL, ntot), lambda i: (0, 0))],
        out_specs=out_specs,
        out_shape=out_shapes,
        compiler_params=_params(("arbitrary",)),
    )(x, mod, nw.reshape(1, D_MODEL), w_p)


_HALO = 16


def _conv_kernel(prev_ref, x_ref, next_ref, w_ref, b_ref, o_ref, buf_ref, *, per_seq):
    i = pl.program_id(0)
    tm = x_ref.shape[0]
    first = (i % per_seq) == 0
    last = (i % per_seq) == per_seq - 1
    prev = prev_ref[...].astype(F32)
    nxt = next_ref[...].astype(F32)
    buf_ref[0:_HALO, :] = jnp.where(first, jnp.zeros_like(prev), prev)
    buf_ref[_HALO:_HALO + tm, :] = x_ref[...].astype(F32)
    buf_ref[_HALO + tm:2 * _HALO + tm, :] = jnp.where(last, jnp.zeros_like(nxt), nxt)
    w = w_ref[...]
    acc = jnp.zeros((tm, x_ref.shape[1]), F32) + b_ref[...]
    for k in range(CONV_K):
        acc = acc + buf_ref[_HALO - CONV_K // 2 + k:_HALO - CONV_K // 2 + k + tm, :] * w[k:k + 1, :]
    o_ref[...] = _silu(acc).astype(o_ref.dtype)


def _conv(x, w, b, seq_len):
    t, c = x.shape
    tm = min(512, seq_len)
    per_seq = seq_len // tm
    hb = tm // _HALO
    nh = t // _HALO
    wp = jnp.zeros((SUBLANES, c), F32).at[:CONV_K].set(w)
    return pl.pallas_call(
        functools.partial(_conv_kernel, per_seq=per_seq),
        name="conv",
        grid=(t // tm,),
        in_specs=[pl.BlockSpec((_HALO, c), lambda i: (jnp.maximum(i * hb - 1, 0), 0)),
                  pl.BlockSpec((tm, c), lambda i: (i, 0)),
                  pl.BlockSpec((_HALO, c), lambda i: (jnp.minimum((i + 1) * hb, nh - 1), 0)),
                  pl.BlockSpec((SUBLANES, c), lambda i: (0, 0)),
                  pl.BlockSpec((1, c), lambda i: (0, 0))],
        out_specs=pl.BlockSpec((tm, c), lambda i: (i, 0)),
        out_shape=jax.ShapeDtypeStruct((t, c), BF16),
        scratch_shapes=[pltpu.VMEM((tm + 2 * _HALO, c), F32)],
        compiler_params=_params(("arbitrary",)),
    )(x, x, x, wp, b.reshape(1, c))


SSD_Q = 128


def _ssd_kernel(*refs, reverse, finalize):
    if finalize:
        (xact_ref, dtp_ref, bias_ref, alog_ref, z_ref, yb_ref, dsk_ref, nw_ref, o_ref, h_ref) = refs
    else:
        (xact_ref, dtp_ref, bias_ref, alog_ref, o_ref, h_ref) = refs
    q = SSD_Q

    @pl.when(pl.program_id(1) == 0)
    def _():
        h_ref[...] = jnp.zeros_like(h_ref)

    r = lax.broadcasted_iota(I32, (q, q), 0)
    s = lax.broadcasted_iota(I32, (q, q), 1)
    mask = (s >= r) if reverse else (r >= s)
    tri = jnp.where(mask, 1.0, 0.0).astype(BF16)
    lo = s < SSD_P

    dt = _softplus(dtp_ref[...] + bias_ref[...])
    dta = dt * (-jnp.exp(alog_ref[...]))
    acs = _cumsum_mm(tri, dta)
    acs_t = acs.T
    dt_t = dt.T
    eacs = jnp.exp(acs)
    tot = acs[0:1, :] if reverse else acs[q - 1:q, :]
    wcols = jnp.exp(tot - acs) * dt

    xact = xact_ref[...]
    ys = []
    for g in range(SSD_G):
        bm = xact[:, SSD_W + g * SSD_N:SSD_W + (g + 1) * SSD_N]
        cm = xact[:, SSD_W + (SSD_G + g) * SSD_N:SSD_W + (SSD_G + g + 1) * SSD_N]
        cb = _dot_nt(cm, bm)
        hg = h_ref[g]
        cmh = _dot(cm, hg.astype(BF16))
        xw_parts, e_parts = [], []
        for pp in range(4):
            j0 = g * 8 + pp * 2
            xpair = xact[:, (g * 4 + pp) * LANES:(g * 4 + pp + 1) * LANES].astype(F32)
            ms = []
            for j in (j0, j0 + 1):
                seg = acs[:, j:j + 1] - acs_t[j:j + 1, :]
                lm = jnp.exp(jnp.where(mask, seg, -jnp.inf))
                ms.append((cb * lm * dt_t[j:j + 1, :]).astype(BF16))
            mcat = jnp.concatenate(ms, axis=1)
            x2 = jnp.concatenate([jnp.where(lo, xpair, 0.0), jnp.where(lo, 0.0, xpair)], axis=0).astype(BF16)
            yd = _dot(mcat, x2)
            esc = jnp.where(lo, jnp.broadcast_to(eacs[:, j0:j0 + 1], (q, LANES)),
                            jnp.broadcast_to(eacs[:, j0 + 1:j0 + 2], (q, LANES)))
            ys.append(yd + cmh[:, pp * LANES:(pp + 1) * LANES] * esc)
            wsc = jnp.where(lo, jnp.broadcast_to(wcols[:, j0:j0 + 1], (q, LANES)),
                            jnp.broadcast_to(wcols[:, j0 + 1:j0 + 2], (q, LANES)))
            xw_parts.append((xpair * wsc).astype(BF16))
            e_parts.append(esc)
        xw = jnp.concatenate(xw_parts, axis=1)
        e_all = jnp.concatenate(e_parts, axis=1)
        dec = e_all[0:1, :] if reverse else e_all[q - 1:q, :]
        h_ref[g] = hg * dec + _dot_tn(bm, xw)
    y = jnp.concatenate(ys, axis=1)

    if not finalize:
        o_ref[...] = y
        return
    xs = xact[:, :SSD_W].astype(F32)
    y = y + yb_ref[...] + dsk_ref[...] * xs
    y = y * _silu(z_ref[...].astype(F32))
    half = SSD_W // SSD_G
    nw = nw_ref[...]
    outs = [_rms(y[:, g * half:(g + 1) * half]) * nw[:, g * half:(g + 1) * half] for g in range(SSD_G)]
    o_ref[...] = jnp.concatenate(outs, axis=1).astype(o_ref.dtype)


def _ssd(xact, dtp, bias, alog, batch, seq_len, reverse, z=None, yb=None, dskip=None, nw=None):
    t = xact.shape[0]
    nc = seq_len // SSD_Q
    finalize = z is not None

    def row(b, c):
        cc = (nc - 1 - c) if reverse else c
        return (b * nc + cc, 0)

    def const(b, c):
        return (0, 0)

    in_specs = [pl.BlockSpec((SSD_Q, SSD_CONV_DIM), row),
                pl.BlockSpec((SSD_Q, LANES), row),
                pl.BlockSpec((1, LANES), const),
                pl.BlockSpec((1, LANES), const)]
    args = [xact, dtp, bias, alog]
    if finalize:
        in_specs += [pl.BlockSpec((SSD_Q, SSD_W), row), pl.BlockSpec((SSD_Q, SSD_W), row),
                     pl.BlockSpec((1, SSD_W), const), pl.BlockSpec((1, SSD_W), const)]
        args += [z, yb, dskip, nw]
    return pl.pallas_call(
        functools.partial(_ssd_kernel, reverse=reverse, finalize=finalize),
        name="ssd_fwd" if finalize else "ssd_bwd",
        grid=(batch, nc),
        in_specs=in_specs,
        out_specs=pl.BlockSpec((SSD_Q, SSD_W), row),
        out_shape=jax.ShapeDtypeStruct((t, SSD_W), BF16 if finalize else F32),
        scratch_shapes=[pltpu.VMEM((SSD_G, SSD_N, SSD_W // SSD_G), F32)],
        compiler_params=_params(("arbitrary", "arbitrary")),
    )(*args)


DN_BLK = 128
DN_C = 64


def _dn_masks(reverse):
    n = DN_BLK
    r = lax.broadcasted_iota(I32, (n, n), 0)
    s = lax.broadcasted_iota(I32, (n, n), 1)
    same = (r >= DN_C) == (s >= DN_C)
    incl = same & ((s >= r) if reverse else (r >= s))
    strict = same & ((s > r) if reverse else (r > s))
    return r, incl, strict, jnp.where(incl, 1.0, 0.0).astype(BF16), jnp.where(r == s, 1.0, 0.0)


def _bdot(a, b):
    return lax.dot_general(a, b, (((2,), (1,)), ((0,), (0,))), preferred_element_type=F32)


def _bdot_nt(a, b):
    return lax.dot_general(a, b, (((2,), (2,)), ((0,), (0,))), preferred_element_type=F32)


def _bdot_tn(a, b):
    return lax.dot_general(a, b, (((1,), (1,)), ((0,), (0,))), preferred_element_type=F32)


def _dn_group(qkvs, bas, dtb, alog, states, reverse):
    n = DN_BLK
    nseq = len(qkvs)
    r, incl, strict, tri, eye = _dn_masks(reverse)
    ba = jnp.concatenate(bas, axis=1)
    beta = _sigmoid(ba)
    gl = -jnp.exp(jnp.concatenate([alog] * nseq, axis=1)) * _softplus(ba + jnp.concatenate([dtb] * nseq, axis=1))
    gcs = _cumsum_mm(tri, gl)
    if reverse:
        t0, t1 = gcs[0:1, :], gcs[DN_C:DN_C + 1, :]
    else:
        t0, t1 = gcs[DN_C - 1:DN_C, :], gcs[n - 1:n, :]
    eg = jnp.exp(gcs)
    ekd = jnp.exp(jnp.where(r[:, 0:1] < DN_C, t0, t1) - gcs)
    dec0 = jnp.exp(t0)
    dec1 = jnp.exp(t1)

    qhb, qg, khb, kd, kbs, kes, vbs, decays, d0s, d1s = [], [], [], [], [], [], [], [], [], []
    for b in range(nseq):
        gcs_t = gcs[:, b * LANES:(b + 1) * LANES].T
        for h in range(DN_H):
            lb = b * LANES + h
            la = lb + DN_H
            qh = qkvs[b][:, h * DN_HD:(h + 1) * DN_HD].astype(F32)
            kh = qkvs[b][:, DN_W + h * DN_HD:DN_W + (h + 1) * DN_HD].astype(F32)
            vh = qkvs[b][:, 2 * DN_W + h * DN_HD:2 * DN_W + (h + 1) * DN_HD].astype(F32)
            qh = qh * lax.rsqrt(jnp.sum(qh * qh, axis=-1, keepdims=True) + EPS) * (DN_HD ** -0.5)
            kh = kh * lax.rsqrt(jnp.sum(kh * kh, axis=-1, keepdims=True) + EPS)
            bcol = beta[:, lb:lb + 1]
            decays.append(jnp.exp(jnp.where(incl, gcs[:, la:la + 1] - gcs_t[DN_H + h:DN_H + h + 1, :], -jnp.inf)))
            kb = kh * bcol
            qhb.append(qh.astype(BF16))
            qg.append((qh * eg[:, la:la + 1]).astype(BF16))
            khb.append(kh.astype(BF16))
            kd.append((kh * ekd[:, la:la + 1]).astype(BF16))
            kbs.append(kb.astype(BF16))
            kes.append((kb * eg[:, la:la + 1]).astype(BF16))
            vbs.append((vh * bcol).astype(BF16))
            d0s.append(dec0[:, la:la + 1])
            d1s.append(dec1[:, la:la + 1])
    decay = jnp.stack(decays)
    khb = jnp.stack(khb)
    kd = jnp.stack(kd)
    qg = jnp.stack(qg)
    nm = jnp.where(strict, _bdot_nt(jnp.stack(kbs), khb) * decay, 0.0)
    p = -nm
    inv = eye + p
    for _ in range(5):
        pb = p.astype(BF16)
        p = _bdot(pb, pb)
        inv = inv + _bdot(inv.astype(BF16), p.astype(BF16))
    invb = inv.astype(BF16)
    u = _bdot(invb, jnp.stack(vbs))
    w = _bdot(invb, jnp.stack(kes)).astype(BF16)
    qk = (_bdot_nt(jnp.stack(qhb), khb) * decay).astype(BF16)
    dec = (jnp.stack(d0s), jnp.stack(d1s))
    st = states
    zeros_c = jnp.zeros((nseq * DN_H, DN_C, DN_HD), F32)
    o_parts = [None, None]
    for ci in ((1, 0) if reverse else (0, 1)):
        rows = slice(ci * DN_C, (ci + 1) * DN_C)
        sb = st.astype(BF16)
        vnew = u[:, rows] - _bdot(w[:, rows], sb)
        vpad = jnp.concatenate([vnew, zeros_c] if ci == 0 else [zeros_c, vnew], axis=1).astype(BF16)
        o_parts[ci] = _bdot(qg[:, rows], sb) + _bdot(qk[:, rows], vpad)
        st = st * dec[ci] + _bdot_tn(kd[:, rows], vnew.astype(BF16))
    o = jnp.concatenate(o_parts, axis=1)
    outs = [jnp.concatenate([o[b * DN_H + h] for h in range(DN_H)], axis=1) for b in range(nseq)]
    return outs, st


def _dn_kernel(qf_ref, baf_ref, qb_ref, bab_ref, dtb_ref, alog_ref, of_ref, ob_ref, s_ref):
    @pl.when(pl.program_id(0) == 0)
    def _():
        s_ref[...] = jnp.zeros_like(s_ref)

    nseq = qf_ref.shape[0]
    for d, (q_ref, ba_ref, o_ref) in enumerate(((qf_ref, baf_ref, of_ref), (qb_ref, bab_ref, ob_ref))):
        outs, st = _dn_group([q_ref[b] for b in range(nseq)], [ba_ref[b] for b in range(nseq)],
                             dtb_ref[d:d + 1, :], alog_ref[d:d + 1, :], s_ref[d], bool(d))
        for b in range(nseq):
            o_ref[b] = outs[b]
        s_ref[d] = st


def _dn(qkv, ba, dtb, alog, batch, seq_len):
    nb = seq_len // DN_BLK
    qkv3 = qkv.reshape(batch, seq_len, DN_CONV_DIM)
    ba3 = ba.reshape(batch, seq_len, LANES)
    fwd = lambda c: (0, c, 0)
    bwd = lambda c: (0, nb - 1 - c, 0)
    const = lambda c: (0, 0)
    of, ob = pl.pallas_call(
        _dn_kernel,
        name="dn",
        grid=(nb,),
        in_specs=[pl.BlockSpec((batch, DN_BLK, DN_CONV_DIM), fwd), pl.BlockSpec((batch, DN_BLK, LANES), fwd),
                  pl.BlockSpec((batch, DN_BLK, DN_CONV_DIM), bwd), pl.BlockSpec((batch, DN_BLK, LANES), bwd),
                  pl.BlockSpec((2, LANES), const), pl.BlockSpec((2, LANES), const)],
        out_specs=[pl.BlockSpec((batch, DN_BLK, DN_W), fwd), pl.BlockSpec((batch, DN_BLK, DN_W), bwd)],
        out_shape=[jax.ShapeDtypeStruct((batch, seq_len, DN_W), F32)] * 2,
        scratch_shapes=[pltpu.VMEM((2, batch * DN_H, DN_HD, DN_HD), F32)],
        compiler_params=_params(("arbitrary",)),
    )(qkv3, ba3, qkv3, ba3, dtb, alog)
    t = batch * seq_len
    return of.reshape(t, DN_W), ob.reshape(t, DN_W)


def _rope_tables(seq_len):
    rows = seq_len // GRID_W
    row_idx = jnp.repeat(jnp.arange(rows, dtype=F32), GRID_W)
    col_idx = jnp.tile(jnp.arange(GRID_W, dtype=F32), rows)
    axis_dim = ATT_HD // 2
    inv_freq = jnp.power(ROPE_THETA, -jnp.arange(0, axis_dim, 2, dtype=F32) / axis_dim)
    ra = row_idx[:, None] * inv_freq
    ca = col_idx[:, None] * inv_freq
    cos_t = jnp.concatenate([jnp.cos(ra), jnp.cos(ra), jnp.cos(ca), jnp.cos(ca)], axis=1)
    sin_t = jnp.concatenate([-jnp.sin(ra), jnp.sin(ra), -jnp.sin(ca), jnp.sin(ca)], axis=1)
    return cos_t, sin_t


def _qkprep_kernel(q_ref, k_ref, cos_ref, sin_ref, qn_ref, kn_ref, qo_ref, ko_ref):
    cos_t = cos_ref[...]
    sin_t = sin_ref[...]
    lane = lax.broadcasted_iota(I32, cos_t.shape, 1)
    low = (lane & (ATT_HD // 4)) == 0

    def prep(x, nw, scale):
        x = _rms(x.astype(F32)) * nw
        partner = jnp.where(low, pltpu.roll(x, ATT_HD - ATT_HD // 4, 1), pltpu.roll(x, ATT_HD // 4, 1))
        return (x * cos_t + partner * sin_t) * scale

    qscale = ATT_HD ** -0.5 * float(np.log2(np.e))
    qs = [prep(q_ref[:, h * ATT_HD:(h + 1) * ATT_HD], qn_ref[...], qscale) for h in range(ATT_H)]
    ks = [prep(k_ref[:, h * ATT_HD:(h + 1) * ATT_HD], kn_ref[...], 1.0) for h in range(ATT_KV)]
    qo_ref[...] = jnp.concatenate(qs, axis=1).astype(qo_ref.dtype)
    ko_ref[...] = jnp.concatenate(ks, axis=1).astype(ko_ref.dtype)


def _qkprep(q, k, cos_t, sin_t, qn, kn, seq_len):
    t = q.shape[0]
    tm = min(512, seq_len)
    per_seq = seq_len // tm
    return pl.pallas_call(
        _qkprep_kernel,
        name="qkprep",
        grid=(t // tm,),
        in_specs=[pl.BlockSpec((tm, ATT_W), lambda i: (i, 0)),
                  pl.BlockSpec((tm, ATT_KV_W), lambda i: (i, 0)),
                  pl.BlockSpec((tm, ATT_HD), lambda i: (i % per_seq, 0)),
                  pl.BlockSpec((tm, ATT_HD), lambda i: (i % per_seq, 0)),
                  pl.BlockSpec((1, ATT_HD), lambda i: (0, 0)),
                  pl.BlockSpec((1, ATT_HD), lambda i: (0, 0))],
        out_specs=[pl.BlockSpec((tm, ATT_W), lambda i: (i, 0)),
                   pl.BlockSpec((tm, ATT_KV_W), lambda i: (i, 0))],
        out_shape=[jax.ShapeDtypeStruct((t, ATT_W), BF16), jax.ShapeDtypeStruct((t, ATT_KV_W), BF16)],
        compiler_params=_params(("arbitrary",)),
    )(q, k, cos_t, sin_t, qn.reshape(1, ATT_HD), kn.reshape(1, ATT_HD))


def _flash_kernel(q_ref, k_ref, v_ref, o_ref, *, tk):
    tq = q_ref.shape[0]
    nk = k_ref.shape[0] // tk
    q2 = jnp.concatenate([q_ref[:, :ATT_HD], q_ref[:, ATT_HD:]], axis=0)
    m = l = acc = None
    for j in range(nk):
        sc = _dot_nt(q2, k_ref[j * tk:(j + 1) * tk, :])
        mx = jnp.max(sc, axis=-1, keepdims=True)
        if j == 0:
            m = mx
            p = jnp.exp2(sc - m)
            l = jnp.sum(p, axis=-1, keepdims=True)
            acc = _dot(p.astype(BF16), v_ref[j * tk:(j + 1) * tk, :])
        else:
            m_new = jnp.maximum(m, mx)
            p = jnp.exp2(sc - m_new)
            alpha = jnp.exp2(m - m_new)
            l = alpha * l + jnp.sum(p, axis=-1, keepdims=True)
            acc = alpha * acc + _dot(p.astype(BF16), v_ref[j * tk:(j + 1) * tk, :])
            m = m_new
    out = acc / l
    o_ref[...] = jnp.concatenate([out[:tq], out[tq:]], axis=1).astype(o_ref.dtype)


def _flash(qh, kh, v, batch, seq_len):
    t = qh.shape[0]
    tq = min(256, seq_len)
    tk = min(512, seq_len)
    nq = seq_len // tq
    rep_w = (ATT_H // ATT_KV) * ATT_HD
    return pl.pallas_call(
        functools.partial(_flash_kernel, tk=tk),
        name="flash",
        grid=(batch, ATT_KV, nq),
        in_specs=[pl.BlockSpec((tq, rep_w), lambda b, g, i: (b * nq + i, g)),
                  pl.BlockSpec((seq_len, ATT_HD), lambda b, g, i: (b, g)),
                  pl.BlockSpec((seq_len, ATT_HD), lambda b, g, i: (b, g))],
        out_specs=pl.BlockSpec((tq, rep_w), lambda b, g, i: (b * nq + i, g)),
        out_shape=jax.ShapeDtypeStruct((t, ATT_W), BF16),
        compiler_params=_params(("arbitrary",) * 3),
    )(qh, kh, v)


def _outproj_kernel(ys_ref, of_ref, ob_ref, zd_ref, ya_ref, x_ref, mod_ref, nwd_ref, nwp_ref, nwf_ref, wo_ref,
                    wr_ref, x1_ref, h2_ref, aff_ref):
    yd = []
    for h in range(DN_H):
        cols = slice(h * DN_HD, (h + 1) * DN_HD)
        o = _rms(of_ref[:, cols] + ob_ref[:, cols]) * nwd_ref[...]
        yd.append((o * _silu(zd_ref[:, cols].astype(F32))).astype(BF16))
    yd = jnp.concatenate(yd, axis=1)
    m = (_dot(ys_ref[...], wo_ref[0:SSD_W, :]) + _dot(yd, wo_ref[SSD_W:SSD_W + DN_W, :])
         + _dot(ya_ref[...], wo_ref[SSD_W + DN_W:, :]))
    mod = mod_ref[0]
    x1 = x_ref[...] + mod[2:3, :] * (_rms(m) * nwp_ref[...])
    x1_ref[...] = x1
    h2 = _rms(x1) * nwf_ref[...] * (1.0 + mod[4:5, :]) + mod[3:4, :]
    h2_ref[...] = h2.astype(h2_ref.dtype)
    logits = jnp.dot(h2, wr_ref[...], preferred_element_type=F32, precision=lax.Precision.HIGHEST)
    lane = lax.broadcasted_iota(I32, logits.shape, 1)
    logits = jnp.where(lane < N_EXPERTS, logits, -jnp.inf)
    ex = jnp.exp(logits - jnp.max(logits, axis=-1, keepdims=True))
    aff = ex / jnp.sum(ex, axis=-1, keepdims=True)
    aff_ref[...] = aff.T[0:N_EXPERTS, :]


def _outproj(ys, of, ob, zd, ya, x, mod, nwd, nwp, nwf, wo_bf, wr_p, seq_len):
    t = x.shape[0]
    tm = min(512, seq_len)
    per_seq = seq_len // tm
    row = lambda i: (i, 0)
    const = lambda i: (0, 0)
    return pl.pallas_call(
        _outproj_kernel,
        name="outproj",
        grid=(t // tm,),
        in_specs=[pl.BlockSpec((tm, SSD_W), row), pl.BlockSpec((tm, DN_W), row), pl.BlockSpec((tm, DN_W), row),
                  pl.BlockSpec((tm, DN_W), row), pl.BlockSpec((tm, ATT_W), row),
                  pl.BlockSpec((tm, D_MODEL), row),
                  pl.BlockSpec((1, 6, D_MODEL), lambda i: (i // per_seq, 0, 0)),
                  pl.BlockSpec((1, DN_HD), const),
                  pl.BlockSpec((1, D_MODEL), const), pl.BlockSpec((1, D_MODEL), const),
                  pl.BlockSpec((2 * D_MODEL, D_MODEL), const),
                  pl.BlockSpec((D_MODEL, LANES), const)],
        out_specs=[pl.BlockSpec((tm, D_MODEL), row), pl.BlockSpec((tm, D_MODEL), row),
                   pl.BlockSpec((N_EXPERTS, tm), lambda i: (0, i))],
        out_shape=[jax.ShapeDtypeStruct((t, D_MODEL), F32), jax.ShapeDtypeStruct((t, D_MODEL), BF16),
                   jax.ShapeDtypeStruct((N_EXPERTS, t), F32)],
        compiler_params=_params(("arbitrary",)),
    )(ys, of, ob, zd, ya, x, mod, nwd, nwp.reshape(1, D_MODEL), nwf.reshape(1, D_MODEL), wo_bf, wr_p)


def _route_kernel(aff_ref, pos_ref, offs_ref, *, cap):
    ne, t = aff_ref.shape
    nb = t // LANES
    capf = float(cap)

    def bits_of(x):
        return pltpu.bitcast(x, I32)

    def bis(i, thr):
        cand = thr | jnp.left_shift(jnp.int32(1), 30 - i)
        cnt = jnp.sum(jnp.where(bits_of(aff_ref[...]) >= cand, 1.0, 0.0), axis=1, keepdims=True)
        return jnp.where(cnt >= capf, cand, thr)

    thr = lax.fori_loop(0, 31, bis, jnp.zeros((ne, 1), I32))
    n_gt = jnp.sum(jnp.where(bits_of(aff_ref[...]) > thr, 1.0, 0.0), axis=1, keepdims=True)
    need_eq = capf - n_gt

    r = lax.broadcasted_iota(I32, (LANES, LANES), 0)
    s = lax.broadcasted_iota(I32, (LANES, LANES), 1)
    triu = jnp.where(r <= s, 1.0, 0.0).astype(BF16)
    lane_nb = lax.broadcasted_iota(I32, (ne, nb), 1)

    offs_ref[...] = jnp.zeros_like(offs_ref)

    def tile(i, carry):
        run_sel, run_eq = carry
        start = pl.multiple_of(i * LANES, LANES)
        b = bits_of(aff_ref[:, pl.ds(start, LANES)])
        gt = b > thr
        eq = jnp.where(b == thr, 1.0, 0.0)
        eq_rank = _dot(eq.astype(BF16), triu) - eq + run_eq
        sel = jnp.where(gt | ((eq > 0.0) & (eq_rank < need_eq)), 1.0, 0.0)
        pos = _dot(sel.astype(BF16), triu) - sel + run_sel
        pos_ref[:, pl.ds(start, LANES)] = jnp.where(sel > 0.0, pos, -1.0).astype(I32)
        offs_ref[...] = jnp.where(lane_nb == i, run_sel.astype(I32), offs_ref[...])
        return (run_sel + jnp.sum(sel, axis=1, keepdims=True),
                run_eq + jnp.sum(eq, axis=1, keepdims=True))

    zero = jnp.zeros((ne, 1), F32)
    lax.fori_loop(0, nb, tile, (zero, zero))


def _route(aff_t, cap):
    ne, t = aff_t.shape
    nb = t // LANES
    return pl.pallas_call(
        functools.partial(_route_kernel, cap=cap),
        name="route",
        out_shape=[jax.ShapeDtypeStruct((ne, t), I32), jax.ShapeDtypeStruct((ne, nb), I32)],
        compiler_params=pltpu.CompilerParams(vmem_limit_bytes=VMEM_LIMIT),
    )(aff_t)


MOE_SUB = 128
MOE_WIN = MOE_SUB + SUBLANES
MOE_TILE = 256


def _moe_kernel(offs_ref, h_ref, pos_ref, gate_ref, wg_ref, wu_ref, wd_ref, o_ref,
                xe_ref, ye_ref, gb_ref, *, cap, nsub):
    sb = pl.program_id(0)
    e = pl.program_id(1)
    f = pl.program_id(2)
    nsb = pl.num_programs(0)
    nf = pl.num_programs(2)
    nblk = nsb * nsub
    start = offs_ref[e, sb * nsub]
    end = jnp.where(sb == nsb - 1, cap, offs_ref[e, jnp.minimum((sb + 1) * nsub, nblk - 1)])
    count = end - start
    nfull = count // MOE_TILE

    @pl.when((sb == 0) & (e == 0) & (f == 0))
    def _():
        xe_ref[...] = jnp.zeros_like(xe_ref)
        ye_ref[...] = jnp.zeros_like(ye_ref)
        gb_ref[...] = jnp.zeros_like(gb_ref)

    @pl.when((e == 0) & (f == 0))
    def _():
        o_ref[...] = jnp.zeros_like(o_ref)

    rid = lax.broadcasted_iota(I32, (MOE_WIN, MOE_SUB), 0)

    def window(j):
        base = offs_ref[e, sb * nsub + j] - start
        a8 = pl.multiple_of((base // SUBLANES) * SUBLANES, SUBLANES)
        posrow = pos_ref[0, :, j * MOE_SUB:(j + 1) * MOE_SUB]
        onehot = (posrow - (start + a8) == rid) & (posrow >= 0)
        return base, a8, onehot

    @pl.when(f == 0)
    def _():
        for j in range(nsub):
            base, a8, onehot = window(j)
            comp = _dot(jnp.where(onehot, 1.0, 0.0).astype(BF16), h_ref[j * MOE_SUB:(j + 1) * MOE_SUB, :])
            gate = jnp.sum(jnp.where(onehot, gate_ref[0, :, j * MOE_SUB:(j + 1) * MOE_SUB], 0.0),
                           axis=1, keepdims=True)
            keep = rid[:, 0:1] < (base - a8)
            xe_ref[pl.ds(a8, MOE_WIN), :] = jnp.where(keep, xe_ref[pl.ds(a8, MOE_WIN), :], comp)
            gb_ref[pl.ds(a8, MOE_WIN), :] = jnp.where(keep, gb_ref[pl.ds(a8, MOE_WIN), :],
                                                      jnp.broadcast_to(gate, (MOE_WIN, LANES)))

    def ffn_rows(r0, rows):
        x = xe_ref[pl.ds(r0, rows), :].astype(BF16)
        a = _dot(x, wg_ref[0])
        hid = (_silu(a) * _dot(x, wu_ref[0])).astype(BF16)
        y = _dot(hid, wd_ref[0])
        ye_ref[pl.ds(r0, rows), :] = jnp.where(f == 0, y, ye_ref[pl.ds(r0, rows), :] + y)

    def ffn_tile(i, carry):
        ffn_rows(pl.multiple_of(i * MOE_TILE, MOE_TILE), MOE_TILE)
        return carry

    lax.fori_loop(0, nfull, ffn_tile, 0)
    rem0 = pl.multiple_of(nfull * MOE_TILE, MOE_TILE)
    for k in range(MOE_TILE // MOE_SUB):
        @pl.when(count - nfull * MOE_TILE > k * MOE_SUB)
        def _():
            ffn_rows(rem0 + k * MOE_SUB, MOE_SUB)

    @pl.when(f == nf - 1)
    def _():
        for j in range(nsub):
            _, a8, onehot = window(j)
            yw = (ye_ref[pl.ds(a8, MOE_WIN), :] * gb_ref[pl.ds(a8, MOE_WIN), 0:1]).astype(BF16)
            o_ref[j * MOE_SUB:(j + 1) * MOE_SUB, :] += _dot_tn(jnp.where(onehot, 1.0, 0.0).astype(BF16), yw)


def _moe(offs, h2, pos, aff_t, wg, wu, wd, cap, sblk, fchunk):
    t = h2.shape[0]
    nsb = t // sblk
    nsub = sblk // MOE_SUB
    nf = EXPERT_FF // fchunk
    rows = sblk + MOE_TILE + MOE_SUB
    grid_spec = pltpu.PrefetchScalarGridSpec(
        num_scalar_prefetch=1,
        grid=(nsb, N_EXPERTS, nf),
        in_specs=[pl.BlockSpec((sblk, D_MODEL), lambda s, e, f, o: (s, 0)),
                  pl.BlockSpec((1, 1, sblk), lambda s, e, f, o: (e, 0, s)),
                  pl.BlockSpec((1, 1, sblk), lambda s, e, f, o: (e, 0, s)),
                  pl.BlockSpec((1, D_MODEL, fchunk), lambda s, e, f, o: (e, 0, f)),
                  pl.BlockSpec((1, D_MODEL, fchunk), lambda s, e, f, o: (e, 0, f)),
                  pl.BlockSpec((1, fchunk, D_MODEL), lambda s, e, f, o: (e, f, 0))],
        out_specs=pl.BlockSpec((sblk, D_MODEL), lambda s, e, f, o: (s, 0)),
        scratch_shapes=[pltpu.VMEM((rows, D_MODEL), F32), pltpu.VMEM((rows, D_MODEL), F32),
                        pltpu.VMEM((rows, LANES), F32)],
    )
    return pl.pallas_call(
        functools.partial(_moe_kernel, cap=cap, nsub=nsub),
        name="moe",
        grid_spec=grid_spec,
        out_shape=jax.ShapeDtypeStruct((t, D_MODEL), F32),
        compiler_params=_params(("arbitrary",) * 3),
    )(offs, h2, pos.reshape(N_EXPERTS, 1, t), aff_t.reshape(N_EXPERTS, 1, t), wg, wu, wd)


def _post_kernel(x_ref, f_ref, mod_ref, nw_ref, o_ref):
    mod = mod_ref[0]
    o_ref[...] = x_ref[...] + mod[5:6, :] * (_rms(f_ref[...]) * nw_ref[...])


def _post(x1, f, mod, nw, seq_len):
    t = x1.shape[0]
    tm = min(512, seq_len)
    per_seq = seq_len // tm
    row = lambda i: (i, 0)
    return pl.pallas_call(
        _post_kernel,
        name="post",
        grid=(t // tm,),
        in_specs=[pl.BlockSpec((tm, D_MODEL), row), pl.BlockSpec((tm, D_MODEL), row),
                  pl.BlockSpec((1, 6, D_MODEL), lambda i: (i // per_seq, 0, 0)),
                  pl.BlockSpec((1, D_MODEL), lambda i: (0, 0))],
        out_specs=pl.BlockSpec((tm, D_MODEL), row),
        out_shape=jax.ShapeDtypeStruct((t, D_MODEL), F32),
        compiler_params=_params(("arbitrary",)),
    )(x1, f, mod, nw.reshape(1, D_MODEL))


def _pad_lanes(v, offset=0):
    out = jnp.zeros((1, LANES), F32)
    return out.at[0, offset:offset + v.shape[0]].set(v.astype(F32))


def _prep_layer(p, l):
    w_in = p["w_in"][l]
    cols, off = [], 0
    for (_, width, stored, _) in _IN_SEGS:
        seg = w_in[:, off:off + width]
        if stored != width:
            seg = jnp.pad(seg, ((0, 0), (0, stored - width)))
        cols.append(seg)
        off += width
    q = {
        "w_ada": p["w_ada"][l].astype(BF16), "b_ada": p["b_ada"][l],
        "w_in": jnp.concatenate(cols, axis=1).astype(BF16),
        "w_out": p["w_out"][l].astype(BF16),
        "w_router": jnp.pad(p["w_router"][l], ((0, 0), (0, LANES - N_EXPERTS))),
        "w_gate": p["w_gate"][l].astype(BF16), "w_up": p["w_up"][l].astype(BF16),
        "w_down": p["w_down"][l].astype(BF16),
        "ssd_bias": [_pad_lanes(p["ssd_dt_bias"][l, d]) for d in range(2)],
        "ssd_alog": [_pad_lanes(p["ssd_a_log"][l, d]) for d in range(2)],
        "ssd_d": jnp.repeat(p["ssd_d"][l], SSD_P).reshape(1, SSD_W),
        "ssd_norm": p["ssd_norm"][l].reshape(1, SSD_W),
        "dn_bias": jnp.concatenate([_pad_lanes(p["dn_dt_bias"][l, d], DN_H) for d in range(2)], axis=0),
        "dn_alog": jnp.concatenate([_pad_lanes(p["dn_a_log"][l, d], DN_H) for d in range(2)], axis=0),
        "dn_norm": p["dn_norm"][l].reshape(1, DN_HD),
    }
    for name in ("norm_mix_pre", "norm_mix_post", "norm_ffn_pre", "norm_ffn_post", "conv_ssd_w", "conv_ssd_b",
                 "conv_dn_w", "q_norm", "k_norm"):
        q[name] = p[name][l]
    return q


def _trunk(x, c, layers, sblk, fchunk):
    batch, seq_len, _ = x.shape
    t = batch * seq_len
    cap = EC_CAPACITY * t // N_EXPERTS
    cos_t, sin_t = _rope_tables(seq_len)
    x = x.reshape(t, D_MODEL)
    for q in layers:
        mod = _ada(c, q["w_ada"], q["b_ada"])
        z_ssd, xbc, dtp, qkv_dn, z_dn, ba_dn, qa, ka, va = _inproj(x, mod, q["norm_mix_pre"], q["w_in"], seq_len)
        xact = _conv(xbc, q["conv_ssd_w"], q["conv_ssd_b"], seq_len)
        qkvact = _conv(qkv_dn, q["conv_dn_w"], jnp.zeros((DN_CONV_DIM,), F32), seq_len)
        yb = _ssd(xact, dtp, q["ssd_bias"][1], q["ssd_alog"][1], batch, seq_len, True)
        y_ssd = _ssd(xact, dtp, q["ssd_bias"][0], q["ssd_alog"][0], batch, seq_len, False,
                     z=z_ssd, yb=yb, dskip=q["ssd_d"], nw=q["ssd_norm"])
        o_f, o_b = _dn(qkvact, ba_dn, q["dn_bias"], q["dn_alog"], batch, seq_len)
        qh, kh = _qkprep(qa, ka, cos_t, sin_t, q["q_norm"], q["k_norm"], seq_len)
        y_att = _flash(qh, kh, va, batch, seq_len)
        x1, h2, aff_t = _outproj(y_ssd, o_f, o_b, z_dn, y_att, x, mod, q["dn_norm"], q["norm_mix_post"],
                                 q["norm_ffn_pre"], q["w_out"], q["w_router"], seq_len)
        pos, offs = _route(aff_t, cap)
        f = _moe(offs, h2, pos, aff_t, q["w_gate"], q["w_up"], q["w_down"], cap, sblk, fchunk)
        x = _post(x1, f, mod, q["norm_ffn_post"], seq_len)
    return x.reshape(batch, seq_len, D_MODEL)


def kernel(x_prompt, x_sample, c_prompt, c_sample, w_ada, b_ada, norm_mix_pre, norm_mix_post, w_in, conv_ssd_w, conv_ssd_b, ssd_dt_bias, ssd_a_log, ssd_d, ssd_norm, conv_dn_w, dn_dt_bias, dn_a_log, dn_norm, q_norm, k_norm, w_out, norm_ffn_pre, norm_ffn_post, w_router, w_gate, w_up, w_down):
    p = dict(w_ada=w_ada, b_ada=b_ada, norm_mix_pre=norm_mix_pre, norm_mix_post=norm_mix_post,
             w_in=w_in, conv_ssd_w=conv_ssd_w, conv_ssd_b=conv_ssd_b, ssd_dt_bias=ssd_dt_bias,
             ssd_a_log=ssd_a_log, ssd_d=ssd_d, ssd_norm=ssd_norm, conv_dn_w=conv_dn_w,
             dn_dt_bias=dn_dt_bias, dn_a_log=dn_a_log, dn_norm=dn_norm, q_norm=q_norm, k_norm=k_norm,
             w_out=w_out, norm_ffn_pre=norm_ffn_pre, norm_ffn_post=norm_ffn_post,
             w_router=w_router, w_gate=w_gate, w_up=w_up, w_down=w_down)
    layers = [_prep_layer(p, l) for l in range(w_in.shape[0])]
    sblk = min(2048, x_prompt.shape[0] * x_prompt.shape[1])
    y_prompt = _trunk(x_prompt, c_prompt, layers, sblk, 512)
    sblk = min(2048, x_sample.shape[0] * x_sample.shape[1])
    y_sample = _trunk(x_sample, c_sample, layers, sblk, 512)
    return (y_prompt, y_sample)
```

```python
import functools

import jax
import jax.numpy as jnp
import numpy as np
from jax import lax
from jax.experimental import pallas as pl
from jax.experimental.pallas import tpu as pltpu

F32 = jnp.float32
BF16 = jnp.bfloat16
I32 = jnp.int32

D_MODEL = 1024
SSD_W = 1024
SSD_P = 64
SSD_H = 16
SSD_G = 2
SSD_N = 128
SSD_CONV_DIM = SSD_W + 2 * SSD_G * SSD_N
DN_W = 512
DN_HD = 128
DN_H = 4
DN_CONV_DIM = 3 * DN_W
ATT_W = 512
ATT_HD = 128
ATT_H = 4
ATT_KV = 2
ATT_KV_W = ATT_KV * ATT_HD
GRID_W = 64
ROPE_THETA = 10000.0
N_EXPERTS = 16
EC_CAPACITY = 2
EXPERT_FF = 2048
EPS = 1e-6
CONV_K = 5

LANES = 128
SUBLANES = 8
VMEM_LIMIT = 56 * 1024 * 1024
MOE_VMEM_LIMIT = 60 * 1024 * 1024

_IN_SEGS = (
    ("z_ssd", SSD_W, SSD_W, BF16),
    ("xbc", SSD_CONV_DIM, SSD_CONV_DIM, BF16),
    ("dt", SSD_H, LANES, F32),
    ("qkv_dn", DN_CONV_DIM, DN_CONV_DIM, BF16),
    ("z_dn", DN_W, DN_W, BF16),
    ("ba_dn", 2 * DN_H, LANES, F32),
    ("q", ATT_W, ATT_W, BF16),
    ("k", ATT_KV_W, ATT_KV_W, BF16),
    ("v", ATT_KV_W, ATT_KV_W, BF16),
)


def _params(sem):
    return pltpu.CompilerParams(dimension_semantics=sem, vmem_limit_bytes=VMEM_LIMIT)


def _sigmoid(x):
    return 1.0 / (1.0 + jnp.exp(-x))


def _silu(x):
    return x * _sigmoid(x)


def _softplus(x):
    return jnp.maximum(x, 0.0) + jnp.log(1.0 + jnp.exp(-jnp.abs(x)))


def _rms(x):
    return x * lax.rsqrt(jnp.mean(x * x, axis=-1, keepdims=True) + EPS)


def _dot(a, b):
    return jnp.dot(a, b, preferred_element_type=F32)


def _dot_nt(a, b):
    return lax.dot_general(a, b, (((1,), (1,)), ((), ())), preferred_element_type=F32)


def _dot_tn(a, b):
    return lax.dot_general(a, b, (((0,), (0,)), ((), ())), preferred_element_type=F32)


def _cumsum_mm(tri, x):
    hi = x.astype(BF16)
    r1 = x - hi.astype(F32)
    mid = r1.astype(BF16)
    lo = (r1 - mid.astype(F32)).astype(BF16)
    return _dot(tri, hi) + _dot(tri, mid) + _dot(tri, lo)


def _ada_kernel(c_ref, w_ref, b_ref, o_ref):
    o_ref[...] = _dot(_silu(c_ref[...]).astype(BF16), w_ref[...]) + b_ref[...]


def _ada(c, w_bf, b):
    nb = c.shape[0]
    rows = -(-nb // SUBLANES) * SUBLANES
    cp = jnp.zeros((rows, D_MODEL), F32).at[:nb].set(c)
    n = w_bf.shape[1]
    tn = 1024
    out = pl.pallas_call(
        _ada_kernel,
        name="ada",
        grid=(n // tn,),
        in_specs=[pl.BlockSpec((rows, D_MODEL), lambda j: (0, 0)),
                  pl.BlockSpec((D_MODEL, tn), lambda j: (0, j)),
                  pl.BlockSpec((1, tn), lambda j: (0, j))],
        out_specs=pl.BlockSpec((rows, tn), lambda j: (0, j)),
        out_shape=jax.ShapeDtypeStruct((rows, n), F32),
        compiler_params=_params(("arbitrary",)),
    )(cp, w_bf, b.reshape(1, n))
    return out[:nb].reshape(nb, 6, D_MODEL)


def _inproj_kernel(x_ref, mod_ref, nw_ref, w_ref, *out_refs):
    mod = mod_ref[0]
    h = _rms(x_ref[...]) * nw_ref[...]
    h = h * (1.0 + mod[1:2, :]) + mod[0:1, :]
    hb = h.astype(BF16)
    off = 0
    for o_ref in out_refs:
        width = o_ref.shape[1]
        o_ref[...] = _dot(hb, w_ref[:, off:off + width]).astype(o_ref.dtype)
        off += width


def _inproj(x, mod, nw, w_p, seq_len):
    t = x.shape[0]
    tm = min(512, seq_len)
    per_seq = seq_len // tm
    ntot = w_p.shape[1]
    out_shapes = [jax.ShapeDtypeStruct((t, sw), dt) for (_, _, sw, dt) in _IN_SEGS]
    out_specs = [pl.BlockSpec((tm, sw), lambda i: (i, 0)) for (_, _, sw, _) in _IN_SEGS]
    return pl.pallas_call(
        _inproj_kernel,
        name="inproj",
        grid=(t // tm,),
        in_specs=[pl.BlockSpec((tm, D_MODEL), lambda i: (i, 0)),
                  pl.BlockSpec((1, 6, D_MODEL), lambda i: (i // per_seq, 0, 0)),
                  pl.BlockSpec((1, D_MODEL), lambda i: (0, 0)),
                  pl.BlockSpec((D_MODEL, ntot), lambda i: (0, 0))],
        out_specs=out_specs,
        out_shape=out_shapes,
        compiler_params=_params(("arbitrary",)),
    )(x, mod, nw.reshape(1, D_MODEL), w_p)


_HALO = 16


def _conv_kernel(prev_ref, x_ref, next_ref, w_ref, b_ref, o_ref, buf_ref, *, per_seq):
    i = pl.program_id(0)
    tm = x_ref.shape[0]
    first = (i % per_seq) == 0
    last = (i % per_seq) == per_seq - 1
    prev = prev_ref[...].astype(F32)
    nxt = next_ref[...].astype(F32)
    buf_ref[0:_HALO, :] = jnp.where(first, jnp.zeros_like(prev), prev)
    buf_ref[_HALO:_HALO + tm, :] = x_ref[...].astype(F32)
    buf_ref[_HALO + tm:2 * _HALO + tm, :] = jnp.where(last, jnp.zeros_like(nxt), nxt)
    w = w_ref[...]
    acc = jnp.zeros((tm, x_ref.shape[1]), F32) + b_ref[...]
    for k in range(CONV_K):
        acc = acc + buf_ref[_HALO - CONV_K // 2 + k:_HALO - CONV_K // 2 + k + tm, :] * w[k:k + 1, :]
    o_ref[...] = _silu(acc).astype(o_ref.dtype)


def _conv(x, w, b, seq_len):
    t, c = x.shape
    tm = min(512, seq_len)
    per_seq = seq_len // tm
    hb = tm // _HALO
    nh = t // _HALO
    wp = jnp.zeros((SUBLANES, c), F32).at[:CONV_K].set(w)
    return pl.pallas_call(
        functools.partial(_conv_kernel, per_seq=per_seq),
        name="conv",
        grid=(t // tm,),
        in_specs=[pl.BlockSpec((_HALO, c), lambda i: (jnp.maximum(i * hb - 1, 0), 0)),
                  pl.BlockSpec((tm, c), lambda i: (i, 0)),
                  pl.BlockSpec((_HALO, c), lambda i: (jnp.minimum((i + 1) * hb, nh - 1), 0)),
                  pl.BlockSpec((SUBLANES, c), lambda i: (0, 0)),
                  pl.BlockSpec((1, c), lambda i: (0, 0))],
        out_specs=pl.BlockSpec((tm, c), lambda i: (i, 0)),
        out_shape=jax.ShapeDtypeStruct((t, c), BF16),
        scratch_shapes=[pltpu.VMEM((tm + 2 * _HALO, c), F32)],
        compiler_params=_params(("arbitrary",)),
    )(x, x, x, wp, b.reshape(1, c))


SSD_Q = 128


def _ssd_kernel(*refs, reverse, finalize):
    if finalize:
        (xact_ref, dtp_ref, bias_ref, alog_ref, z_ref, yb_ref, dsk_ref, nw_ref, o_ref, h_ref) = refs
    else:
        (xact_ref, dtp_ref, bias_ref, alog_ref, o_ref, h_ref) = refs
    q = SSD_Q

    @pl.when(pl.program_id(1) == 0)
    def _():
        h_ref[...] = jnp.zeros_like(h_ref)

    r = lax.broadcasted_iota(I32, (q, q), 0)
    s = lax.broadcasted_iota(I32, (q, q), 1)
    mask = (s >= r) if reverse else (r >= s)
    tri = jnp.where(mask, 1.0, 0.0).astype(BF16)
    lo = s < SSD_P

    dt = _softplus(dtp_ref[...] + bias_ref[...])
    dta = dt * (-jnp.exp(alog_ref[...]))
    acs = _cumsum_mm(tri, dta)
    acs_t = acs.T
    dt_t = dt.T
    eacs = jnp.exp(acs)
    tot = acs[0:1, :] if reverse else acs[q - 1:q, :]
    wcols = jnp.exp(tot - acs) * dt

    xact = xact_ref[...]
    ys = []
    for g in range(SSD_G):
        bm = xact[:, SSD_W + g * SSD_N:SSD_W + (g + 1) * SSD_N]
        cm = xact[:, SSD_W + (SSD_G + g) * SSD_N:SSD_W + (SSD_G + g + 1) * SSD_N]
        cb = _dot_nt(cm, bm)
        hg = h_ref[g]
        cmh = _dot(cm, hg.astype(BF16))
        xw_parts, e_parts = [], []
        for pp in range(4):
            j0 = g * 8 + pp * 2
            xpair = xact[:, (g * 4 + pp) * LANES:(g * 4 + pp + 1) * LANES].astype(F32)
            ms = []
            for j in (j0, j0 + 1):
                seg = acs[:, j:j + 1] - acs_t[j:j + 1, :]
                lm = jnp.exp(jnp.where(mask, seg, -jnp.inf))
                ms.append((cb * lm * dt_t[j:j + 1, :]).astype(BF16))
            mcat = jnp.concatenate(ms, axis=1)
            x2 = jnp.concatenate([jnp.where(lo, xpair, 0.0), jnp.where(lo, 0.0, xpair)], axis=0).astype(BF16)
            yd = _dot(mcat, x2)
            esc = jnp.where(lo, jnp.broadcast_to(eacs[:, j0:j0 + 1], (q, LANES)),
                            jnp.broadcast_to(eacs[:, j0 + 1:j0 + 2], (q, LANES)))
            ys.append(yd + cmh[:, pp * LANES:(pp + 1) * LANES] * esc)
            wsc = jnp.where(lo, jnp.broadcast_to(wcols[:, j0:j0 + 1], (q, LANES)),
                            jnp.broadcast_to(wcols[:, j0 + 1:j0 + 2], (q, LANES)))
            xw_parts.append((xpair * wsc).astype(BF16))
            e_parts.append(esc)
        xw = jnp.concatenate(xw_parts, axis=1)
        e_all = jnp.concatenate(e_parts, axis=1)
        dec = e_all[0:1, :] if reverse else e_all[q - 1:q, :]
        h_ref[g] = hg * dec + _dot_tn(bm, xw)
    y = jnp.concatenate(ys, axis=1)

    if not finalize:
        o_ref[...] = y
        return
    xs = xact[:, :SSD_W].astype(F32)
    y = y + yb_ref[...] + dsk_ref[...] * xs
    y = y * _silu(z_ref[...].astype(F32))
    half = SSD_W // SSD_G
    nw = nw_ref[...]
    outs = [_rms(y[:, g * half:(g + 1) * half]) * nw[:, g * half:(g + 1) * half] for g in range(SSD_G)]
    o_ref[...] = jnp.concatenate(outs, axis=1).astype(o_ref.dtype)


def _ssd(xact, dtp, bias, alog, batch, seq_len, reverse, z=None, yb=None, dskip=None, nw=None):
    t = xact.shape[0]
    nc = seq_len // SSD_Q
    finalize = z is not None

    def row(b, c):
        cc = (nc - 1 - c) if reverse else c
        return (b * nc + cc, 0)

    def const(b, c):
        return (0, 0)

    in_specs = [pl.BlockSpec((SSD_Q, SSD_CONV_DIM), row),
                pl.BlockSpec((SSD_Q, LANES), row),
                pl.BlockSpec((1, LANES), const),
                pl.BlockSpec((1, LANES), const)]
    args = [xact, dtp, bias, alog]
    if finalize:
        in_specs += [pl.BlockSpec((SSD_Q, SSD_W), row), pl.BlockSpec((SSD_Q, SSD_W), row),
                     pl.BlockSpec((1, SSD_W), const), pl.BlockSpec((1, SSD_W), const)]
        args += [z, yb, dskip, nw]
    return pl.pallas_call(
        functools.partial(_ssd_kernel, reverse=reverse, finalize=finalize),
        name="ssd_fwd" if finalize else "ssd_bwd",
        grid=(batch, nc),
        in_specs=in_specs,
        out_specs=pl.BlockSpec((SSD_Q, SSD_W), row),
        out_shape=jax.ShapeDtypeStruct((t, SSD_W), BF16 if finalize else F32),
        scratch_shapes=[pltpu.VMEM((SSD_G, SSD_N, SSD_W // SSD_G), F32)],
        compiler_params=_params(("arbitrary", "arbitrary")),
    )(*args)


DN_BLK = 128
DN_C = 64


def _dn_masks(reverse):
    n = DN_BLK
    r = lax.broadcasted_iota(I32, (n, n), 0)
    s = lax.broadcasted_iota(I32, (n, n), 1)
    same = (r >= DN_C) == (s >= DN_C)
    incl = same & ((s >= r) if reverse else (r >= s))
    strict = same & ((s > r) if reverse else (r > s))
    return r, incl, strict, jnp.where(incl, 1.0, 0.0).astype(BF16), jnp.where(r == s, 1.0, 0.0)


def _bdot(a, b):
    return lax.dot_general(a, b, (((2,), (1,)), ((0,), (0,))), preferred_element_type=F32)


def _bdot_nt(a, b):
    return lax.dot_general(a, b, (((2,), (2,)), ((0,), (0,))), preferred_element_type=F32)


def _bdot_tn(a, b):
    return lax.dot_general(a, b, (((1,), (1,)), ((0,), (0,))), preferred_element_type=F32)


def _dn_group(qkvs, bas, dtb, alog, states, reverse):
    n = DN_BLK
    nseq = len(qkvs)
    r, incl, strict, tri, eye = _dn_masks(reverse)
    ba = jnp.concatenate(bas, axis=1)
    beta = _sigmoid(ba)
    gl = -jnp.exp(jnp.concatenate([alog] * nseq, axis=1)) * _softplus(ba + jnp.concatenate([dtb] * nseq, axis=1))
    gcs = _cumsum_mm(tri, gl)
    if reverse:
        t0, t1 = gcs[0:1, :], gcs[DN_C:DN_C + 1, :]
    else:
        t0, t1 = gcs[DN_C - 1:DN_C, :], gcs[n - 1:n, :]
    eg = jnp.exp(gcs)
    ekd = jnp.exp(jnp.where(r[:, 0:1] < DN_C, t0, t1) - gcs)
    dec0 = jnp.exp(t0)
    dec1 = jnp.exp(t1)

    qhb, qg, khb, kd, kbs, kes, vbs, decays, d0s, d1s = [], [], [], [], [], [], [], [], [], []
    for b in range(nseq):
        gcs_t = gcs[:, b * LANES:(b + 1) * LANES].T
        for h in range(DN_H):
            lb = b * LANES + h
            la = lb + DN_H
            qh = qkvs[b][:, h * DN_HD:(h + 1) * DN_HD].astype(F32)
            kh = qkvs[b][:, DN_W + h * DN_HD:DN_W + (h + 1) * DN_HD].astype(F32)
            vh = qkvs[b][:, 2 * DN_W + h * DN_HD:2 * DN_W + (h + 1) * DN_HD].astype(F32)
            qh = qh * lax.rsqrt(jnp.sum(qh * qh, axis=-1, keepdims=True) + EPS) * (DN_HD ** -0.5)
            kh = kh * lax.rsqrt(jnp.sum(kh * kh, axis=-1, keepdims=True) + EPS)
            bcol = beta[:, lb:lb + 1]
            decays.append(jnp.exp(jnp.where(incl, gcs[:, la:la + 1] - gcs_t[DN_H + h:DN_H + h + 1, :], -jnp.inf)))
            kb = kh * bcol
            qhb.append(qh.astype(BF16))
            qg.append((qh * eg[:, la:la + 1]).astype(BF16))
            khb.append(kh.astype(BF16))
            kd.append((kh * ekd[:, la:la + 1]).astype(BF16))
            kbs.append(kb.astype(BF16))
            kes.append((kb * eg[:, la:la + 1]).astype(BF16))
            vbs.append((vh * bcol).astype(BF16))
            d0s.append(dec0[:, la:la + 1])
            d1s.append(dec1[:, la:la + 1])
    decay = jnp.stack(decays)
    khb = jnp.stack(khb)
    kd = jnp.stack(kd)
    qg = jnp.stack(qg)
    nm = jnp.where(strict, _bdot_nt(jnp.stack(kbs), khb) * decay, 0.0)
    p = -nm
    inv = eye + p
    for _ in range(5):
        pb = p.astype(BF16)
        p = _bdot(pb, pb)
        inv = inv + _bdot(inv.astype(BF16), p.astype(BF16))
    invb = inv.astype(BF16)
    u = _bdot(invb, jnp.stack(vbs))
    w = _bdot(invb, jnp.stack(kes)).astype(BF16)
    qk = (_bdot_nt(jnp.stack(qhb), khb) * decay).astype(BF16)
    dec = (jnp.stack(d0s), jnp.stack(d1s))
    st = states
    zeros_c = jnp.zeros((nseq * DN_H, DN_C, DN_HD), F32)
    o_parts = [None, None]
    for ci in ((1, 0) if reverse else (0, 1)):
        rows = slice(ci * DN_C, (ci + 1) * DN_C)
        sb = st.astype(BF16)
        vnew = u[:, rows] - _bdot(w[:, rows], sb)
        vpad = jnp.concatenate([vnew, zeros_c] if ci == 0 else [zeros_c, vnew], axis=1).astype(BF16)
        o_parts[ci] = _bdot(qg[:, rows], sb) + _bdot(qk[:, rows], vpad)
        st = st * dec[ci] + _bdot_tn(kd[:, rows], vnew.astype(BF16))
    o = jnp.concatenate(o_parts, axis=1)
    outs = [jnp.concatenate([o[b * DN_H + h] for h in range(DN_H)], axis=1) for b in range(nseq)]
    return outs, st


def _dn_kernel(qf_ref, baf_ref, qb_ref, bab_ref, dtb_ref, alog_ref, of_ref, ob_ref, s_ref):
    @pl.when(pl.program_id(0) == 0)
    def _():
        s_ref[...] = jnp.zeros_like(s_ref)

    nseq = qf_ref.shape[0]
    for d, (q_ref, ba_ref, o_ref) in enumerate(((qf_ref, baf_ref, of_ref), (qb_ref, bab_ref, ob_ref))):
        outs, st = _dn_group([q_ref[b] for b in range(nseq)], [ba_ref[b] for b in range(nseq)],
                             dtb_ref[d:d + 1, :], alog_ref[d:d + 1, :], s_ref[d], bool(d))
        for b in range(nseq):
            o_ref[b] = outs[b]
        s_ref[d] = st


def _dn(qkv, ba, dtb, alog, batch, seq_len):
    nb = seq_len // DN_BLK
    qkv3 = qkv.reshape(batch, seq_len, DN_CONV_DIM)
    ba3 = ba.reshape(batch, seq_len, LANES)
    fwd = lambda c: (0, c, 0)
    bwd = lambda c: (0, nb - 1 - c, 0)
    const = lambda c: (0, 0)
    of, ob = pl.pallas_call(
        _dn_kernel,
        name="dn",
        grid=(nb,),
        in_specs=[pl.BlockSpec((batch, DN_BLK, DN_CONV_DIM), fwd), pl.BlockSpec((batch, DN_BLK, LANES), fwd),
                  pl.BlockSpec((batch, DN_BLK, DN_CONV_DIM), bwd), pl.BlockSpec((batch, DN_BLK, LANES), bwd),
                  pl.BlockSpec((2, LANES), const), pl.BlockSpec((2, LANES), const)],
        out_specs=[pl.BlockSpec((batch, DN_BLK, DN_W), fwd), pl.BlockSpec((batch, DN_BLK, DN_W), bwd)],
        out_shape=[jax.ShapeDtypeStruct((batch, seq_len, DN_W), F32)] * 2,
        scratch_shapes=[pltpu.VMEM((2, batch * DN_H, DN_HD, DN_HD), F32)],
        compiler_params=_params(("arbitrary",)),
    )(qkv3, ba3, qkv3, ba3, dtb, alog)
    t = batch * seq_len
    return of.reshape(t, DN_W), ob.reshape(t, DN_W)


def _rope_tables(seq_len):
    rows = seq_len // GRID_W
    row_idx = jnp.repeat(jnp.arange(rows, dtype=F32), GRID_W)
    col_idx = jnp.tile(jnp.arange(GRID_W, dtype=F32), rows)
    axis_dim = ATT_HD // 2
    inv_freq = jnp.power(ROPE_THETA, -jnp.arange(0, axis_dim, 2, dtype=F32) / axis_dim)
    ra = row_idx[:, None] * inv_freq
    ca = col_idx[:, None] * inv_freq
    cos_t = jnp.concatenate([jnp.cos(ra), jnp.cos(ra), jnp.cos(ca), jnp.cos(ca)], axis=1)
    sin_t = jnp.concatenate([-jnp.sin(ra), jnp.sin(ra), -jnp.sin(ca), jnp.sin(ca)], axis=1)
    return cos_t, sin_t


def _qkprep_kernel(q_ref, k_ref, cos_ref, sin_ref, qn_ref, kn_ref, qo_ref, ko_ref):
    cos_t = cos_ref[...]
    sin_t = sin_ref[...]
    lane = lax.broadcasted_iota(I32, cos_t.shape, 1)
    low = (lane & (ATT_HD // 4)) == 0

    def prep(x, nw, scale):
        x = _rms(x.astype(F32)) * nw
        partner = jnp.where(low, pltpu.roll(x, ATT_HD - ATT_HD // 4, 1), pltpu.roll(x, ATT_HD // 4, 1))
        return (x * cos_t + partner * sin_t) * scale

    qscale = ATT_HD ** -0.5 * float(np.log2(np.e))
    qs = [prep(q_ref[:, h * ATT_HD:(h + 1) * ATT_HD], qn_ref[...], qscale) for h in range(ATT_H)]
    ks = [prep(k_ref[:, h * ATT_HD:(h + 1) * ATT_HD], kn_ref[...], 1.0) for h in range(ATT_KV)]
    qo_ref[...] = jnp.concatenate(qs, axis=1).astype(qo_ref.dtype)
    ko_ref[...] = jnp.concatenate(ks, axis=1).astype(ko_ref.dtype)


def _qkprep(q, k, cos_t, sin_t, qn, kn, seq_len):
    t = q.shape[0]
    tm = min(512, seq_len)
    per_seq = seq_len // tm
    return pl.pallas_call(
        _qkprep_kernel,
        name="qkprep",
        grid=(t // tm,),
        in_specs=[pl.BlockSpec((tm, ATT_W), lambda i: (i, 0)),
                  pl.BlockSpec((tm, ATT_KV_W), lambda i: (i, 0)),
                  pl.BlockSpec((tm, ATT_HD), lambda i: (i % per_seq, 0)),
                  pl.BlockSpec((tm, ATT_HD), lambda i: (i % per_seq, 0)),
                  pl.BlockSpec((1, ATT_HD), lambda i: (0, 0)),
                  pl.BlockSpec((1, ATT_HD), lambda i: (0, 0))],
        out_specs=[pl.BlockSpec((tm, ATT_W), lambda i: (i, 0)),
                   pl.BlockSpec((tm, ATT_KV_W), lambda i: (i, 0))],
        out_shape=[jax.ShapeDtypeStruct((t, ATT_W), BF16), jax.ShapeDtypeStruct((t, ATT_KV_W), BF16)],
        compiler_params=_params(("arbitrary",)),
    )(q, k, cos_t, sin_t, qn.reshape(1, ATT_HD), kn.reshape(1, ATT_HD))


def _flash_kernel(q_ref, k_ref, v_ref, o_ref, *, tk):
    tq = q_ref.shape[0]
    nk = k_ref.shape[0] // tk
    q2 = jnp.concatenate([q_ref[:, :ATT_HD], q_ref[:, ATT_HD:]], axis=0)
    m = l = acc = None
    for j in range(nk):
        sc = _dot_nt(q2, k_ref[j * tk:(j + 1) * tk, :])
        mx = jnp.max(sc, axis=-1, keepdims=True)
        if j == 0:
            m = mx
            p = jnp.exp2(sc - m)
            l = jnp.sum(p, axis=-1, keepdims=True)
            acc = _dot(p.astype(BF16), v_ref[j * tk:(j + 1) * tk, :])
        else:
            m_new = jnp.maximum(m, mx)
            p = jnp.exp2(sc - m_new)
            alpha = jnp.exp2(m - m_new)
            l = alpha * l + jnp.sum(p, axis=-1, keepdims=True)
            acc = alpha * acc + _dot(p.astype(BF16), v_ref[j * tk:(j + 1) * tk, :])
            m = m_new
    out = acc / l
    o_ref[...] = jnp.concatenate([out[:tq], out[tq:]], axis=1).astype(o_ref.dtype)


def _flash(qh, kh, v, batch, seq_len):
    t = qh.shape[0]
    tq = min(256, seq_len)
    tk = min(256, seq_len)
    nq = seq_len // tq
    rep_w = (ATT_H // ATT_KV) * ATT_HD
    return pl.pallas_call(
        functools.partial(_flash_kernel, tk=tk),
        name="flash",
        grid=(batch, ATT_KV, nq),
        in_specs=[pl.BlockSpec((tq, rep_w), lambda b, g, i: (b * nq + i, g)),
                  pl.BlockSpec((seq_len, ATT_HD), lambda b, g, i: (b, g)),
                  pl.BlockSpec((seq_len, ATT_HD), lambda b, g, i: (b, g))],
        out_specs=pl.BlockSpec((tq, rep_w), lambda b, g, i: (b * nq + i, g)),
        out_shape=jax.ShapeDtypeStruct((t, ATT_W), BF16),
        compiler_params=_params(("arbitrary",) * 3),
    )(qh, kh, v)


def _outproj_kernel(ys_ref, of_ref, ob_ref, zd_ref, ya_ref, x_ref, mod_ref, nwd_ref, nwp_ref, nwf_ref, wo_ref,
                    wr_ref, x1_ref, h2_ref, aff_ref):
    yd = []
    for h in range(DN_H):
        cols = slice(h * DN_HD, (h + 1) * DN_HD)
        o = _rms(of_ref[:, cols] + ob_ref[:, cols]) * nwd_ref[...]
        yd.append((o * _silu(zd_ref[:, cols].astype(F32))).astype(BF16))
    yd = jnp.concatenate(yd, axis=1)
    m = (_dot(ys_ref[...], wo_ref[0:SSD_W, :]) + _dot(yd, wo_ref[SSD_W:SSD_W + DN_W, :])
         + _dot(ya_ref[...], wo_ref[SSD_W + DN_W:, :]))
    mod = mod_ref[0]
    x1 = x_ref[...] + mod[2:3, :] * (_rms(m) * nwp_ref[...])
    x1_ref[...] = x1
    h2 = _rms(x1) * nwf_ref[...] * (1.0 + mod[4:5, :]) + mod[3:4, :]
    h2_ref[...] = h2.astype(h2_ref.dtype)
    logits = jnp.dot(h2, wr_ref[...], preferred_element_type=F32, precision=lax.Precision.HIGHEST)
    lane = lax.broadcasted_iota(I32, logits.shape, 1)
    logits = jnp.where(lane < N_EXPERTS, logits, -jnp.inf)
    ex = jnp.exp(logits - jnp.max(logits, axis=-1, keepdims=True))
    aff = ex / jnp.sum(ex, axis=-1, keepdims=True)
    aff_ref[...] = aff.T[0:N_EXPERTS, :]


def _outproj(ys, of, ob, zd, ya, x, mod, nwd, nwp, nwf, wo_bf, wr_p, seq_len):
    t = x.shape[0]
    tm = min(512, seq_len)
    per_seq = seq_len // tm
    row = lambda i: (i, 0)
    const = lambda i: (0, 0)
    return pl.pallas_call(
        _outproj_kernel,
        name="outproj",
        grid=(t // tm,),
        in_specs=[pl.BlockSpec((tm, SSD_W), row), pl.BlockSpec((tm, DN_W), row), pl.BlockSpec((tm, DN_W), row),
                  pl.BlockSpec((tm, DN_W), row), pl.BlockSpec((tm, ATT_W), row),
                  pl.BlockSpec((tm, D_MODEL), row),
                  pl.BlockSpec((1, 6, D_MODEL), lambda i: (i // per_seq, 0, 0)),
                  pl.BlockSpec((1, DN_HD), const),
                  pl.BlockSpec((1, D_MODEL), const), pl.BlockSpec((1, D_MODEL), const),
                  pl.BlockSpec((2 * D_MODEL, D_MODEL), const),
                  pl.BlockSpec((D_MODEL, LANES), const)],
        out_specs=[pl.BlockSpec((tm, D_MODEL), row), pl.BlockSpec((tm, D_MODEL), row),
                   pl.BlockSpec((N_EXPERTS, tm), lambda i: (0, i))],
        out_shape=[jax.ShapeDtypeStruct((t, D_MODEL), F32), jax.ShapeDtypeStruct((t, D_MODEL), BF16),
                   jax.ShapeDtypeStruct((N_EXPERTS, t), F32)],
        compiler_params=_params(("arbitrary",)),
    )(ys, of, ob, zd, ya, x, mod, nwd, nwp.reshape(1, D_MODEL), nwf.reshape(1, D_MODEL), wo_bf, wr_p)


def _route_kernel(aff_ref, pos_ref, offs_ref, *, cap):
    ne, t = aff_ref.shape
    nb = t // LANES
    capf = float(cap)

    def bits_of(x):
        return pltpu.bitcast(x, I32)

    def bis(i, thr):
        cand = thr | jnp.left_shift(jnp.int32(1), 30 - i)
        cnt = jnp.sum(jnp.where(bits_of(aff_ref[...]) >= cand, 1.0, 0.0), axis=1, keepdims=True)
        return jnp.where(cnt >= capf, cand, thr)

    thr = lax.fori_loop(0, 31, bis, jnp.zeros((ne, 1), I32))
    n_gt = jnp.sum(jnp.where(bits_of(aff_ref[...]) > thr, 1.0, 0.0), axis=1, keepdims=True)
    need_eq = capf - n_gt

    r = lax.broadcasted_iota(I32, (LANES, LANES), 0)
    s = lax.broadcasted_iota(I32, (LANES, LANES), 1)
    triu = jnp.where(r <= s, 1.0, 0.0).astype(BF16)
    lane_nb = lax.broadcasted_iota(I32, (ne, nb), 1)

    offs_ref[...] = jnp.zeros_like(offs_ref)

    def tile(i, carry):
        run_sel, run_eq = carry
        start = pl.multiple_of(i * LANES, LANES)
        b = bits_of(aff_ref[:, pl.ds(start, LANES)])
        gt = b > thr
        eq = jnp.where(b == thr, 1.0, 0.0)
        eq_rank = _dot(eq.astype(BF16), triu) - eq + run_eq
        sel = jnp.where(gt | ((eq > 0.0) & (eq_rank < need_eq)), 1.0, 0.0)
        pos = _dot(sel.astype(BF16), triu) - sel + run_sel
        pos_ref[:, pl.ds(start, LANES)] = jnp.where(sel > 0.0, pos, -1.0).astype(I32)
        offs_ref[...] = jnp.where(lane_nb == i, run_sel.astype(I32), offs_ref[...])
        return (run_sel + jnp.sum(sel, axis=1, keepdims=True),
                run_eq + jnp.sum(eq, axis=1, keepdims=True))

    zero = jnp.zeros((ne, 1), F32)
    lax.fori_loop(0, nb, tile, (zero, zero))


def _route(aff_t, cap):
    ne, t = aff_t.shape
    nb = t // LANES
    return pl.pallas_call(
        functools.partial(_route_kernel, cap=cap),
        name="route",
        out_shape=[jax.ShapeDtypeStruct((ne, t), I32), jax.ShapeDtypeStruct((ne, nb), I32)],
        compiler_params=pltpu.CompilerParams(vmem_limit_bytes=VMEM_LIMIT),
    )(aff_t)


MOE_SUB = 128
MOE_ALIGN = 16
MOE_WIN_SMALL = 32 + MOE_ALIGN
MOE_WIN_FULL = MOE_SUB + MOE_ALIGN
MOE_TILE = 288
MOE_HALF = MOE_TILE // 2
MOE_FC = 512
MOE_BLOCK = 2048


def _moe_kernel(offs_ref, h_ref, pos_ref, gate_ref, wg_ref, wu_ref, wd_ref, o_ref,
                xe_ref, ye_ref, gb_ref, *, cap, nsub):
    sb = pl.program_id(0)
    e = pl.program_id(1)
    nsb = pl.num_programs(0)
    nblk = nsb * nsub
    start = offs_ref[e, sb * nsub]
    end = jnp.where(sb == nsb - 1, cap, offs_ref[e, jnp.minimum((sb + 1) * nsub, nblk - 1)])
    count = end - start
    nfull = count // MOE_TILE

    @pl.when((sb == 0) & (e == 0))
    def _():
        xe_ref[...] = jnp.zeros_like(xe_ref)
        ye_ref[...] = jnp.zeros_like(ye_ref)
        gb_ref[...] = jnp.zeros_like(gb_ref)

    @pl.when(e == 0)
    def _():
        o_ref[...] = jnp.zeros_like(o_ref)

    def window(j):
        lo = offs_ref[e, sb * nsub + j] - start
        hi = (offs_ref[e, sb * nsub + j + 1] - start) if j + 1 < nsub else count
        a = pl.multiple_of((lo // MOE_ALIGN) * MOE_ALIGN, MOE_ALIGN)
        return lo, hi, a

    wins = [window(j) for j in range(nsub)]
    all_small = functools.reduce(jnp.logical_and, [hi - a <= MOE_WIN_SMALL for (_, hi, a) in wins])

    def onehots(win):
        rid = lax.broadcasted_iota(I32, (win, MOE_SUB), 0)
        out = []
        for j, (_, _, a) in enumerate(wins):
            posrow = pos_ref[0, :, j * MOE_SUB:(j + 1) * MOE_SUB]
            out.append((posrow - (start + a) == rid) & (posrow >= 0))
        return out, jnp.stack([jnp.where(oh, 1.0, 0.0).astype(BF16) for oh in out])

    def for_window_size(body):
        @pl.when(all_small)
        def _():
            body(MOE_WIN_SMALL)

        @pl.when(jnp.logical_not(all_small))
        def _():
            body(MOE_WIN_FULL)

    def gather(win):
        ohs, oh_b = onehots(win)
        comp = _bdot(oh_b, h_ref[...].reshape(nsub, MOE_SUB, D_MODEL))
        rid = lax.broadcasted_iota(I32, (MOE_ALIGN, 1), 0)
        for j, (lo, _, a) in enumerate(wins):
            gate = jnp.sum(jnp.where(ohs[j], gate_ref[0, :, j * MOE_SUB:(j + 1) * MOE_SUB], 0.0),
                           axis=1, keepdims=True)
            gate = jnp.broadcast_to(gate, (win, LANES))
            keep = rid < (lo - a)
            a1 = pl.multiple_of(a + MOE_ALIGN, MOE_ALIGN)
            xe_ref[pl.ds(a, MOE_ALIGN), :] = jnp.where(keep, xe_ref[pl.ds(a, MOE_ALIGN), :].astype(F32),
                                                       comp[j, :MOE_ALIGN]).astype(BF16)
            xe_ref[pl.ds(a1, win - MOE_ALIGN), :] = comp[j, MOE_ALIGN:].astype(BF16)
            gb_ref[pl.ds(a, MOE_ALIGN), :] = jnp.where(keep, gb_ref[pl.ds(a, MOE_ALIGN), :], gate[:MOE_ALIGN])
            gb_ref[pl.ds(a1, win - MOE_ALIGN), :] = gate[MOE_ALIGN:]

    def scatter(win):
        _, oh_b = onehots(win)
        yw = jnp.stack([(ye_ref[pl.ds(a, win), :] * gb_ref[pl.ds(a, win), 0:1]).astype(BF16) for (_, _, a) in wins])
        o_ref[...] += _bdot_tn(oh_b, yw).reshape(nsub * MOE_SUB, D_MODEL)

    for_window_size(gather)

    nfc = wg_ref.shape[2] // MOE_FC

    def ffn_rows(r0, rows):
        x = xe_ref[pl.ds(r0, rows), :]
        y = None
        for fc in range(nfc):
            cols = slice(fc * MOE_FC, (fc + 1) * MOE_FC)
            hid = (_silu(_dot(x, wg_ref[0, :, cols])) * _dot(x, wu_ref[0, :, cols])).astype(BF16)
            part = _dot(hid, wd_ref[0, cols, :])
            y = part if y is None else y + part
        ye_ref[pl.ds(r0, rows), :] = y

    def ffn_tile(i, carry):
        ffn_rows(pl.multiple_of(i * MOE_TILE, MOE_ALIGN), MOE_TILE)
        return carry

    lax.fori_loop(0, nfull, ffn_tile, 0)
    rem = count - nfull * MOE_TILE
    rem0 = pl.multiple_of(nfull * MOE_TILE, MOE_ALIGN)

    @pl.when(rem > MOE_HALF)
    def _():
        ffn_rows(rem0, MOE_TILE)

    @pl.when((rem > 0) & (rem <= MOE_HALF))
    def _():
        ffn_rows(rem0, MOE_HALF)

    for_window_size(scatter)


def _moe(offs, h2, pos, aff_t, wg, wu, wd, cap, sblk):
    t = h2.shape[0]
    nsb = t // sblk
    nsub = sblk // MOE_SUB
    rows = sblk + MOE_TILE + MOE_WIN_FULL
    once = pl.Buffered(1)
    grid_spec = pltpu.PrefetchScalarGridSpec(
        num_scalar_prefetch=1,
        grid=(nsb, N_EXPERTS),
        in_specs=[pl.BlockSpec((sblk, D_MODEL), lambda s, e, o: (s, 0), pipeline_mode=once),
                  pl.BlockSpec((1, 1, sblk), lambda s, e, o: (e, 0, s)),
                  pl.BlockSpec((1, 1, sblk), lambda s, e, o: (e, 0, s)),
                  pl.BlockSpec((1, D_MODEL, EXPERT_FF), lambda s, e, o: (e, 0, 0)),
                  pl.BlockSpec((1, D_MODEL, EXPERT_FF), lambda s, e, o: (e, 0, 0)),
                  pl.BlockSpec((1, EXPERT_FF, D_MODEL), lambda s, e, o: (e, 0, 0))],
        out_specs=pl.BlockSpec((sblk, D_MODEL), lambda s, e, o: (s, 0), pipeline_mode=once),
        scratch_shapes=[pltpu.VMEM((rows, D_MODEL), BF16), pltpu.VMEM((rows, D_MODEL), F32),
                        pltpu.VMEM((rows, LANES), F32)],
    )
    return pl.pallas_call(
        functools.partial(_moe_kernel, cap=cap, nsub=nsub),
        name="moe",
        grid_spec=grid_spec,
        out_shape=jax.ShapeDtypeStruct((t, D_MODEL), F32),
        compiler_params=pltpu.CompilerParams(dimension_semantics=("arbitrary",) * 2,
                                             vmem_limit_bytes=MOE_VMEM_LIMIT),
    )(offs, h2, pos.reshape(N_EXPERTS, 1, t), aff_t.reshape(N_EXPERTS, 1, t), wg, wu, wd)


def _post_kernel(x_ref, f_ref, mod_ref, nw_ref, o_ref):
    mod = mod_ref[0]
    o_ref[...] = x_ref[...] + mod[5:6, :] * (_rms(f_ref[...]) * nw_ref[...])


def _post(x1, f, mod, nw, seq_len):
    t = x1.shape[0]
    tm = min(512, seq_len)
    per_seq = seq_len // tm
    row = lambda i: (i, 0)
    return pl.pallas_call(
        _post_kernel,
        name="post",
        grid=(t // tm,),
        in_specs=[pl.BlockSpec((tm, D_MODEL), row), pl.BlockSpec((tm, D_MODEL), row),
                  pl.BlockSpec((1, 6, D_MODEL), lambda i: (i // per_seq, 0, 0)),
                  pl.BlockSpec((1, D_MODEL), lambda i: (0, 0))],
        out_specs=pl.BlockSpec((tm, D_MODEL), row),
        out_shape=jax.ShapeDtypeStruct((t, D_MODEL), F32),
        compiler_params=_params(("arbitrary",)),
    )(x1, f, mod, nw.reshape(1, D_MODEL))


def _pad_lanes(v, offset=0):
    out = jnp.zeros((1, LANES), F32)
    return out.at[0, offset:offset + v.shape[0]].set(v.astype(F32))


def _prep_layer(p, l):
    w_in = p["w_in"][l]
    cols, off = [], 0
    for (_, width, stored, _) in _IN_SEGS:
        seg = w_in[:, off:off + width]
        if stored != width:
            seg = jnp.pad(seg, ((0, 0), (0, stored - width)))
        cols.append(seg)
        off += width
    q = {
        "w_ada": p["w_ada"][l].astype(BF16), "b_ada": p["b_ada"][l],
        "w_in": jnp.concatenate(cols, axis=1).astype(BF16),
        "w_out": p["w_out"][l].astype(BF16),
        "w_router": jnp.pad(p["w_router"][l], ((0, 0), (0, LANES - N_EXPERTS))),
        "w_gate": p["w_gate"][l].astype(BF16), "w_up": p["w_up"][l].astype(BF16),
        "w_down": p["w_down"][l].astype(BF16),
        "ssd_bias": [_pad_lanes(p["ssd_dt_bias"][l, d]) for d in range(2)],
        "ssd_alog": [_pad_lanes(p["ssd_a_log"][l, d]) for d in range(2)],
        "ssd_d": jnp.repeat(p["ssd_d"][l], SSD_P).reshape(1, SSD_W),
        "ssd_norm": p["ssd_norm"][l].reshape(1, SSD_W),
        "dn_bias": jnp.concatenate([_pad_lanes(p["dn_dt_bias"][l, d], DN_H) for d in range(2)], axis=0),
        "dn_alog": jnp.concatenate([_pad_lanes(p["dn_a_log"][l, d], DN_H) for d in range(2)], axis=0),
        "dn_norm": p["dn_norm"][l].reshape(1, DN_HD),
    }
    for name in ("norm_mix_pre", "norm_mix_post", "norm_ffn_pre", "norm_ffn_post", "conv_ssd_w", "conv_ssd_b",
                 "conv_dn_w", "q_norm", "k_norm"):
        q[name] = p[name][l]
    return q


def _trunk(x, c, layers, sblk):
    batch, seq_len, _ = x.shape
    t = batch * seq_len
    cap = EC_CAPACITY * t // N_EXPERTS
    cos_t, sin_t = _rope_tables(seq_len)
    x = x.reshape(t, D_MODEL)
    for q in layers:
        mod = _ada(c, q["w_ada"], q["b_ada"])
        z_ssd, xbc, dtp, qkv_dn, z_dn, ba_dn, qa, ka, va = _inproj(x, mod, q["norm_mix_pre"], q["w_in"], seq_len)
        xact = _conv(xbc, q["conv_ssd_w"], q["conv_ssd_b"], seq_len)
        qkvact = _conv(qkv_dn, q["conv_dn_w"], jnp.zeros((DN_CONV_DIM,), F32), seq_len)
        yb = _ssd(xact, dtp, q["ssd_bias"][1], q["ssd_alog"][1], batch, seq_len, True)
        y_ssd = _ssd(xact, dtp, q["ssd_bias"][0], q["ssd_alog"][0], batch, seq_len, False,
                     z=z_ssd, yb=yb, dskip=q["ssd_d"], nw=q["ssd_norm"])
        o_f, o_b = _dn(qkvact, ba_dn, q["dn_bias"], q["dn_alog"], batch, seq_len)
        qh, kh = _qkprep(qa, ka, cos_t, sin_t, q["q_norm"], q["k_norm"], seq_len)
        y_att = _flash(qh, kh, va, batch, seq_len)
        x1, h2, aff_t = _outproj(y_ssd, o_f, o_b, z_dn, y_att, x, mod, q["dn_norm"], q["norm_mix_post"],
                                 q["norm_ffn_pre"], q["w_out"], q["w_router"], seq_len)
        pos, offs = _route(aff_t, cap)
        f = _moe(offs, h2, pos, aff_t, q["w_gate"], q["w_up"], q["w_down"], cap, sblk)
        x = _post(x1, f, mod, q["norm_ffn_post"], seq_len)
    return x.reshape(batch, seq_len, D_MODEL)


def kernel(x_prompt, x_sample, c_prompt, c_sample, w_ada, b_ada, norm_mix_pre, norm_mix_post, w_in, conv_ssd_w, conv_ssd_b, ssd_dt_bias, ssd_a_log, ssd_d, ssd_norm, conv_dn_w, dn_dt_bias, dn_a_log, dn_norm, q_norm, k_norm, w_out, norm_ffn_pre, norm_ffn_post, w_router, w_gate, w_up, w_down):
    p = dict(w_ada=w_ada, b_ada=b_ada, norm_mix_pre=norm_mix_pre, norm_mix_post=norm_mix_post,
             w_in=w_in, conv_ssd_w=conv_ssd_w, conv_ssd_b=conv_ssd_b, ssd_dt_bias=ssd_dt_bias,
             ssd_a_log=ssd_a_log, ssd_d=ssd_d, ssd_norm=ssd_norm, conv_dn_w=conv_dn_w,
             dn_dt_bias=dn_dt_bias, dn_a_log=dn_a_log, dn_norm=dn_norm, q_norm=q_norm, k_norm=k_norm,
             w_out=w_out, norm_ffn_pre=norm_ffn_pre, norm_ffn_post=norm_ffn_post,
             w_router=w_router, w_gate=w_gate, w_up=w_up, w_down=w_down)
    layers = [_prep_layer(p, l) for l in range(w_in.shape[0])]
    y_prompt = _trunk(x_prompt, c_prompt, layers, MOE_BLOCK)
    y_sample = _trunk(x_sample, c_sample, layers, MOE_BLOCK)
    return (y_prompt, y_sample)
```

```python
import functools

import jax
import jax.numpy as jnp
import numpy as np
from jax import lax
from jax.experimental import pallas as pl
from jax.experimental.pallas import tpu as pltpu

F32 = jnp.float32
BF16 = jnp.bfloat16
I32 = jnp.int32

D_MODEL = 1024
SSD_W = 1024
SSD_P = 64
SSD_H = 16
SSD_G = 2
SSD_N = 128
SSD_CONV_DIM = SSD_W + 2 * SSD_G * SSD_N
DN_W = 512
DN_HD = 128
DN_H = 4
DN_CONV_DIM = 3 * DN_W
ATT_W = 512
ATT_HD = 128
ATT_H = 4
ATT_KV = 2
ATT_KV_W = ATT_KV * ATT_HD
GRID_W = 64
ROPE_THETA = 10000.0
N_EXPERTS = 16
EC_CAPACITY = 2
EXPERT_FF = 2048
EPS = 1e-6
CONV_K = 5

LANES = 128
SUBLANES = 8
VMEM_LIMIT = 56 * 1024 * 1024
MOE_VMEM_LIMIT = 60 * 1024 * 1024

_IN_SEGS = (
    ("z_ssd", SSD_W, SSD_W, BF16),
    ("xbc", SSD_CONV_DIM, SSD_CONV_DIM, BF16),
    ("dt", SSD_H, LANES, F32),
    ("qkv_dn", DN_CONV_DIM, DN_CONV_DIM, BF16),
    ("z_dn", DN_W, DN_W, BF16),
    ("ba_dn", 2 * DN_H, LANES, F32),
    ("q", ATT_W, ATT_W, BF16),
    ("k", ATT_KV_W, ATT_KV_W, BF16),
    ("v", ATT_KV_W, ATT_KV_W, BF16),
)


def _params(sem):
    return pltpu.CompilerParams(dimension_semantics=sem, vmem_limit_bytes=VMEM_LIMIT)


def _sigmoid(x):
    return 1.0 / (1.0 + jnp.exp(-x))


def _silu(x):
    return x * _sigmoid(x)


def _softplus(x):
    return jnp.maximum(x, 0.0) + jnp.log(1.0 + jnp.exp(-jnp.abs(x)))


def _rms(x):
    return x * lax.rsqrt(jnp.mean(x * x, axis=-1, keepdims=True) + EPS)


def _dot(a, b):
    return jnp.dot(a, b, preferred_element_type=F32)


def _dot_nt(a, b):
    return lax.dot_general(a, b, (((1,), (1,)), ((), ())), preferred_element_type=F32)


def _dot_tn(a, b):
    return lax.dot_general(a, b, (((0,), (0,)), ((), ())), preferred_element_type=F32)


def _cumsum_mm(tri, x):
    hi = x.astype(BF16)
    r1 = x - hi.astype(F32)
    mid = r1.astype(BF16)
    lo = (r1 - mid.astype(F32)).astype(BF16)
    return _dot(tri, hi) + _dot(tri, mid) + _dot(tri, lo)


def _ada_kernel(c_ref, w_ref, b_ref, o_ref):
    o_ref[...] = _dot(_silu(c_ref[...]).astype(BF16), w_ref[...]) + b_ref[...]


def _ada(c, w_bf, b):
    nb = c.shape[0]
    rows = -(-nb // SUBLANES) * SUBLANES
    cp = jnp.zeros((rows, D_MODEL), F32).at[:nb].set(c)
    n = w_bf.shape[1]
    tn = 1024
    out = pl.pallas_call(
        _ada_kernel,
        name="ada",
        grid=(n // tn,),
        in_specs=[pl.BlockSpec((rows, D_MODEL), lambda j: (0, 0)),
                  pl.BlockSpec((D_MODEL, tn), lambda j: (0, j)),
                  pl.BlockSpec((1, tn), lambda j: (0, j))],
        out_specs=pl.BlockSpec((rows, tn), lambda j: (0, j)),
        out_shape=jax.ShapeDtypeStruct((rows, n), F32),
        compiler_params=_params(("arbitrary",)),
    )(cp, w_bf, b.reshape(1, n))
    return out[:nb].reshape(nb, 6, D_MODEL)


def _inproj_kernel(x_ref, mod_ref, nw_ref, w_ref, *out_refs):
    mod = mod_ref[0]
    h = _rms(x_ref[...]) * nw_ref[...]
    h = h * (1.0 + mod[1:2, :]) + mod[0:1, :]
    hb = h.astype(BF16)
    off = 0
    for o_ref in out_refs:
        width = o_ref.shape[1]
        o_ref[...] = _dot(hb, w_ref[:, off:off + width]).astype(o_ref.dtype)
        off += width


def _inproj(x, mod, nw, w_p, seq_len):
    t = x.shape[0]
    tm = min(512, seq_len)
    per_seq = seq_len // tm
    ntot = w_p.shape[1]
    out_shapes = [jax.ShapeDtypeStruct((t, sw), dt) for (_, _, sw, dt) in _IN_SEGS]
    out_specs = [pl.BlockSpec((tm, sw), lambda i: (i, 0)) for (_, _, sw, _) in _IN_SEGS]
    return pl.pallas_call(
        _inproj_kernel,
        name="inproj",
        grid=(t // tm,),
        in_specs=[pl.BlockSpec((tm, D_MODEL), lambda i: (i, 0)),
                  pl.BlockSpec((1, 6, D_MODEL), lambda i: (i // per_seq, 0, 0)),
                  pl.BlockSpec((1, D_MODEL), lambda i: (0, 0)),
                  pl.BlockSpec((D_MODEL, ntot), lambda i: (0, 0))],
        out_specs=out_specs,
        out_shape=out_shapes,
        compiler_params=_params(("arbitrary",)),
    )(x, mod, nw.reshape(1, D_MODEL), w_p)


_HALO = 16


def _conv_kernel(prev_ref, x_ref, next_ref, w_ref, b_ref, o_ref, buf_ref, *, per_seq):
    i = pl.program_id(0)
    tm = x_ref.shape[0]
    first = (i % per_seq) == 0
    last = (i % per_seq) == per_seq - 1
    prev = prev_ref[...].astype(F32)
    nxt = next_ref[...].astype(F32)
    buf_ref[0:_HALO, :] = jnp.where(first, jnp.zeros_like(prev), prev)
    buf_ref[_HALO:_HALO + tm, :] = x_ref[...].astype(F32)
    buf_ref[_HALO + tm:2 * _HALO + tm, :] = jnp.where(last, jnp.zeros_like(nxt), nxt)
    w = w_ref[...]
    acc = jnp.zeros((tm, x_ref.shape[1]), F32) + b_ref[...]
    for k in range(CONV_K):
        acc = acc + buf_ref[_HALO - CONV_K // 2 + k:_HALO - CONV_K // 2 + k + tm, :] * w[k:k + 1, :]
    o_ref[...] = _silu(acc).astype(o_ref.dtype)


def _conv(x, w, b, seq_len):
    t, c = x.shape
    tm = min(512, seq_len)
    per_seq = seq_len // tm
    hb = tm // _HALO
    nh = t // _HALO
    wp = jnp.zeros((SUBLANES, c), F32).at[:CONV_K].set(w)
    return pl.pallas_call(
        functools.partial(_conv_kernel, per_seq=per_seq),
        name="conv",
        grid=(t // tm,),
        in_specs=[pl.BlockSpec((_HALO, c), lambda i: (jnp.maximum(i * hb - 1, 0), 0)),
                  pl.BlockSpec((tm, c), lambda i: (i, 0)),
                  pl.BlockSpec((_HALO, c), lambda i: (jnp.minimum((i + 1) * hb, nh - 1), 0)),
                  pl.BlockSpec((SUBLANES, c), lambda i: (0, 0)),
                  pl.BlockSpec((1, c), lambda i: (0, 0))],
        out_specs=pl.BlockSpec((tm, c), lambda i: (i, 0)),
        out_shape=jax.ShapeDtypeStruct((t, c), BF16),
        scratch_shapes=[pltpu.VMEM((tm + 2 * _HALO, c), F32)],
        compiler_params=_params(("arbitrary",)),
    )(x, x, x, wp, b.reshape(1, c))


SSD_Q = 128


def _ssd_kernel(*refs, reverse, finalize):
    if finalize:
        (xact_ref, dtp_ref, bias_ref, alog_ref, sel_ref, z_ref, yb_ref, dsk_ref, nw_ref, o_ref, h_ref) = refs
    else:
        (xact_ref, dtp_ref, bias_ref, alog_ref, sel_ref, o_ref, h_ref) = refs
    q = SSD_Q

    @pl.when(pl.program_id(1) == 0)
    def _():
        h_ref[...] = jnp.zeros_like(h_ref)

    r = lax.broadcasted_iota(I32, (q, q), 0)
    s = lax.broadcasted_iota(I32, (q, q), 1)
    mask = (s >= r) if reverse else (r >= s)
    tri = jnp.where(mask, 1.0, 0.0).astype(BF16)
    lo = s < SSD_P

    dt = _softplus(dtp_ref[...] + bias_ref[...])
    dta = dt * (-jnp.exp(alog_ref[...]))
    acs = _cumsum_mm(tri, dta)
    acs_t = acs.T
    dt_t = dt.T
    eacs = jnp.exp(acs)
    tot = acs[0:1, :] if reverse else acs[q - 1:q, :]
    wcols = jnp.exp(tot - acs) * dt
    cols = jnp.concatenate([eacs, wcols], axis=0)
    cols_hi = cols.astype(BF16)
    cols_lo = (cols - cols_hi.astype(F32)).astype(BF16)
    spread = _dot(jnp.concatenate([cols_hi, cols_lo], axis=0), sel_ref[...])
    eacs_w = spread[0:q] + spread[2 * q:3 * q]
    wcols_w = spread[q:2 * q] + spread[3 * q:4 * q]

    xact = xact_ref[...]
    ys = []
    for g in range(SSD_G):
        bm = xact[:, SSD_W + g * SSD_N:SSD_W + (g + 1) * SSD_N]
        cm = xact[:, SSD_W + (SSD_G + g) * SSD_N:SSD_W + (SSD_G + g + 1) * SSD_N]
        cb = _dot_nt(cm, bm)
        hg = h_ref[g]
        cmh = _dot(cm, hg.astype(BF16))
        xw_parts = []
        for pp in range(4):
            j0 = g * 8 + pp * 2
            lanes = slice((g * 4 + pp) * LANES, (g * 4 + pp + 1) * LANES)
            xpair = xact[:, lanes].astype(F32)
            ms = []
            for j in (j0, j0 + 1):
                seg = acs[:, j:j + 1] - acs_t[j:j + 1, :]
                lm = jnp.exp(jnp.where(mask, seg, -jnp.inf))
                ms.append((cb * lm * dt_t[j:j + 1, :]).astype(BF16))
            mcat = jnp.concatenate(ms, axis=1)
            x2 = jnp.concatenate([jnp.where(lo, xpair, 0.0), jnp.where(lo, 0.0, xpair)], axis=0).astype(BF16)
            yd = _dot(mcat, x2)
            ys.append(yd + cmh[:, pp * LANES:(pp + 1) * LANES] * eacs_w[:, lanes])
            xw_parts.append((xpair * wcols_w[:, lanes]).astype(BF16))
        xw = jnp.concatenate(xw_parts, axis=1)
        half = slice(g * (SSD_W // SSD_G), (g + 1) * (SSD_W // SSD_G))
        dec = eacs_w[0:1, half] if reverse else eacs_w[q - 1:q, half]
        h_ref[g] = hg * dec + _dot_tn(bm, xw)
    y = jnp.concatenate(ys, axis=1)

    if not finalize:
        o_ref[...] = y
        return
    xs = xact[:, :SSD_W].astype(F32)
    y = y + yb_ref[...] + dsk_ref[...] * xs
    y = y * _silu(z_ref[...].astype(F32))
    half = SSD_W // SSD_G
    nw = nw_ref[...]
    outs = [_rms(y[:, g * half:(g + 1) * half]) * nw[:, g * half:(g + 1) * half] for g in range(SSD_G)]
    o_ref[...] = jnp.concatenate(outs, axis=1).astype(o_ref.dtype)


def _ssd(xact, dtp, bias, alog, batch, seq_len, reverse, z=None, yb=None, dskip=None, nw=None):
    t = xact.shape[0]
    nc = seq_len // SSD_Q
    finalize = z is not None

    def row(b, c):
        cc = (nc - 1 - c) if reverse else c
        return (b * nc + cc, 0)

    def const(b, c):
        return (0, 0)

    in_specs = [pl.BlockSpec((SSD_Q, SSD_CONV_DIM), row),
                pl.BlockSpec((SSD_Q, LANES), row),
                pl.BlockSpec((1, LANES), const),
                pl.BlockSpec((1, LANES), const),
                pl.BlockSpec((LANES, SSD_W), const)]
    sel = jnp.repeat(jnp.eye(LANES, SSD_H, dtype=BF16), SSD_P, axis=1)
    args = [xact, dtp, bias, alog, sel]
    if finalize:
        in_specs += [pl.BlockSpec((SSD_Q, SSD_W), row), pl.BlockSpec((SSD_Q, SSD_W), row),
                     pl.BlockSpec((1, SSD_W), const), pl.BlockSpec((1, SSD_W), const)]
        args += [z, yb, dskip, nw]
    return pl.pallas_call(
        functools.partial(_ssd_kernel, reverse=reverse, finalize=finalize),
        name="ssd_fwd" if finalize else "ssd_bwd",
        grid=(batch, nc),
        in_specs=in_specs,
        out_specs=pl.BlockSpec((SSD_Q, SSD_W), row),
        out_shape=jax.ShapeDtypeStruct((t, SSD_W), BF16 if finalize else F32),
        scratch_shapes=[pltpu.VMEM((SSD_G, SSD_N, SSD_W // SSD_G), F32)],
        compiler_params=_params(("arbitrary", "arbitrary")),
    )(*args)


DN_BLK = 128
DN_C = 64


def _dn_masks(reverse):
    n = DN_BLK
    r = lax.broadcasted_iota(I32, (n, n), 0)
    s = lax.broadcasted_iota(I32, (n, n), 1)
    same = (r >= DN_C) == (s >= DN_C)
    incl = same & ((s >= r) if reverse else (r >= s))
    strict = same & ((s > r) if reverse else (r > s))
    return r, incl, strict, jnp.where(incl, 1.0, 0.0).astype(BF16), jnp.where(r == s, 1.0, 0.0)


def _bdot(a, b):
    return lax.dot_general(a, b, (((2,), (1,)), ((0,), (0,))), preferred_element_type=F32)


def _bdot_nt(a, b):
    return lax.dot_general(a, b, (((2,), (2,)), ((0,), (0,))), preferred_element_type=F32)


def _bdot_tn(a, b):
    return lax.dot_general(a, b, (((1,), (1,)), ((0,), (0,))), preferred_element_type=F32)


def _dn_group(qkvs, bas, dtb, alog, states, reverse):
    n = DN_BLK
    nseq = len(qkvs)
    r, incl, strict, tri, eye = _dn_masks(reverse)
    ba = jnp.concatenate(bas, axis=1)
    beta = _sigmoid(ba)
    gl = -jnp.exp(jnp.concatenate([alog] * nseq, axis=1)) * _softplus(ba + jnp.concatenate([dtb] * nseq, axis=1))
    gcs = _cumsum_mm(tri, gl)
    if reverse:
        t0, t1 = gcs[0:1, :], gcs[DN_C:DN_C + 1, :]
    else:
        t0, t1 = gcs[DN_C - 1:DN_C, :], gcs[n - 1:n, :]
    eg = jnp.exp(gcs)
    ekd = jnp.exp(jnp.where(r[:, 0:1] < DN_C, t0, t1) - gcs)
    dec0 = jnp.exp(t0)
    dec1 = jnp.exp(t1)

    qhb, qg, khb, kd, kbs, kes, vbs, decays, d0s, d1s = [], [], [], [], [], [], [], [], [], []
    for b in range(nseq):
        gcs_t = gcs[:, b * LANES:(b + 1) * LANES].T
        for h in range(DN_H):
            lb = b * LANES + h
            la = lb + DN_H
            qh = qkvs[b][:, h * DN_HD:(h + 1) * DN_HD].astype(F32)
            kh = qkvs[b][:, DN_W + h * DN_HD:DN_W + (h + 1) * DN_HD].astype(F32)
            vh = qkvs[b][:, 2 * DN_W + h * DN_HD:2 * DN_W + (h + 1) * DN_HD].astype(F32)
            qh = qh * lax.rsqrt(jnp.sum(qh * qh, axis=-1, keepdims=True) + EPS) * (DN_HD ** -0.5)
            kh = kh * lax.rsqrt(jnp.sum(kh * kh, axis=-1, keepdims=True) + EPS)
            bcol = beta[:, lb:lb + 1]
            decays.append(jnp.exp(jnp.where(incl, gcs[:, la:la + 1] - gcs_t[DN_H + h:DN_H + h + 1, :], -jnp.inf)))
            kb = kh * bcol
            qhb.append(qh.astype(BF16))
            qg.append((qh * eg[:, la:la + 1]).astype(BF16))
            khb.append(kh.astype(BF16))
            kd.append((kh * ekd[:, la:la + 1]).astype(BF16))
            kbs.append(kb.astype(BF16))
            kes.append((kb * eg[:, la:la + 1]).astype(BF16))
            vbs.append((vh * bcol).astype(BF16))
            d0s.append(dec0[:, la:la + 1])
            d1s.append(dec1[:, la:la + 1])
    decay = jnp.stack(decays)
    khb = jnp.stack(khb)
    kd = jnp.stack(kd)
    qg = jnp.stack(qg)
    nm = jnp.where(strict, _bdot_nt(jnp.stack(kbs), khb) * decay, 0.0)
    p = -nm
    inv = eye + p
    pb = p.astype(BF16)
    for _ in range(5):
        p = _bdot(pb, pb)
        pb = p.astype(BF16)
        inv = inv + _bdot(inv.astype(BF16), pb)
    uw = _bdot(inv.astype(BF16), jnp.concatenate([jnp.stack(vbs), jnp.stack(kes)], axis=2))
    u = uw[:, :, :DN_HD]
    w = uw[:, :, DN_HD:].astype(BF16)
    qk = (_bdot_nt(jnp.stack(qhb), khb) * decay).astype(BF16)
    dec = (jnp.stack(d0s), jnp.stack(d1s))
    st = states
    zeros_c = jnp.zeros((nseq * DN_H, DN_C, DN_HD), F32)
    o_parts = [None, None]
    for ci in ((1, 0) if reverse else (0, 1)):
        rows = slice(ci * DN_C, (ci + 1) * DN_C)
        ws = _bdot(jnp.concatenate([w[:, rows], qg[:, rows]], axis=1), st.astype(BF16))
        vnew = u[:, rows] - ws[:, :DN_C]
        vpad = jnp.concatenate([vnew, zeros_c] if ci == 0 else [zeros_c, vnew], axis=1).astype(BF16)
        o_parts[ci] = ws[:, DN_C:] + _bdot(qk[:, rows], vpad)
        st = st * dec[ci] + _bdot_tn(kd[:, rows], vnew.astype(BF16))
    o = jnp.concatenate(o_parts, axis=1)
    outs = [jnp.concatenate([o[b * DN_H + h] for h in range(DN_H)], axis=1) for b in range(nseq)]
    return outs, st


def _dn_kernel(qf_ref, baf_ref, qb_ref, bab_ref, dtb_ref, alog_ref, of_ref, ob_ref, s_ref):
    @pl.when(pl.program_id(0) == 0)
    def _():
        s_ref[...] = jnp.zeros_like(s_ref)

    nseq = qf_ref.shape[0]
    for d, (q_ref, ba_ref, o_ref) in enumerate(((qf_ref, baf_ref, of_ref), (qb_ref, bab_ref, ob_ref))):
        outs, st = _dn_group([q_ref[b] for b in range(nseq)], [ba_ref[b] for b in range(nseq)],
                             dtb_ref[d:d + 1, :], alog_ref[d:d + 1, :], s_ref[d], bool(d))
        for b in range(nseq):
            o_ref[b] = outs[b]
        s_ref[d] = st


def _dn(qkv, ba, dtb, alog, batch, seq_len):
    nb = seq_len // DN_BLK
    qkv3 = qkv.reshape(batch, seq_len, DN_CONV_DIM)
    ba3 = ba.reshape(batch, seq_len, LANES)
    fwd = lambda c: (0, c, 0)
    bwd = lambda c: (0, nb - 1 - c, 0)
    const = lambda c: (0, 0)
    of, ob = pl.pallas_call(
        _dn_kernel,
        name="dn",
        grid=(nb,),
        in_specs=[pl.BlockSpec((batch, DN_BLK, DN_CONV_DIM), fwd), pl.BlockSpec((batch, DN_BLK, LANES), fwd),
                  pl.BlockSpec((batch, DN_BLK, DN_CONV_DIM), bwd), pl.BlockSpec((batch, DN_BLK, LANES), bwd),
                  pl.BlockSpec((2, LANES), const), pl.BlockSpec((2, LANES), const)],
        out_specs=[pl.BlockSpec((batch, DN_BLK, DN_W), fwd), pl.BlockSpec((batch, DN_BLK, DN_W), bwd)],
        out_shape=[jax.ShapeDtypeStruct((batch, seq_len, DN_W), F32)] * 2,
        scratch_shapes=[pltpu.VMEM((2, batch * DN_H, DN_HD, DN_HD), F32)],
        compiler_params=_params(("arbitrary",)),
    )(qkv3, ba3, qkv3, ba3, dtb, alog)
    t = batch * seq_len
    return of.reshape(t, DN_W), ob.reshape(t, DN_W)


def _rope_tables(seq_len):
    rows = seq_len // GRID_W
    row_idx = jnp.repeat(jnp.arange(rows, dtype=F32), GRID_W)
    col_idx = jnp.tile(jnp.arange(GRID_W, dtype=F32), rows)
    axis_dim = ATT_HD // 2
    inv_freq = jnp.power(ROPE_THETA, -jnp.arange(0, axis_dim, 2, dtype=F32) / axis_dim)
    ra = row_idx[:, None] * inv_freq
    ca = col_idx[:, None] * inv_freq
    cos_t = jnp.concatenate([jnp.cos(ra), jnp.cos(ra), jnp.cos(ca), jnp.cos(ca)], axis=1)
    sin_t = jnp.concatenate([-jnp.sin(ra), jnp.sin(ra), -jnp.sin(ca), jnp.sin(ca)], axis=1)
    return cos_t, sin_t


def _qkprep_kernel(q_ref, k_ref, cos_ref, sin_ref, qn_ref, kn_ref, qo_ref, ko_ref):
    cos_t = cos_ref[...]
    sin_t = sin_ref[...]
    lane = lax.broadcasted_iota(I32, cos_t.shape, 1)
    low = (lane & (ATT_HD // 4)) == 0

    def prep(x, nw, scale):
        x = _rms(x.astype(F32)) * nw
        partner = jnp.where(low, pltpu.roll(x, ATT_HD - ATT_HD // 4, 1), pltpu.roll(x, ATT_HD // 4, 1))
        return (x * cos_t + partner * sin_t) * scale

    qscale = ATT_HD ** -0.5 * float(np.log2(np.e))
    qs = [prep(q_ref[:, h * ATT_HD:(h + 1) * ATT_HD], qn_ref[...], qscale) for h in range(ATT_H)]
    ks = [prep(k_ref[:, h * ATT_HD:(h + 1) * ATT_HD], kn_ref[...], 1.0) for h in range(ATT_KV)]
    qo_ref[...] = jnp.concatenate(qs, axis=1).astype(qo_ref.dtype)
    ko_ref[...] = jnp.concatenate(ks, axis=1).astype(ko_ref.dtype)


def _qkprep(q, k, cos_t, sin_t, qn, kn, seq_len):
    t = q.shape[0]
    tm = min(512, seq_len)
    per_seq = seq_len // tm
    return pl.pallas_call(
        _qkprep_kernel,
        name="qkprep",
        grid=(t // tm,),
        in_specs=[pl.BlockSpec((tm, ATT_W), lambda i: (i, 0)),
                  pl.BlockSpec((tm, ATT_KV_W), lambda i: (i, 0)),
                  pl.BlockSpec((tm, ATT_HD), lambda i: (i % per_seq, 0)),
                  pl.BlockSpec((tm, ATT_HD), lambda i: (i % per_seq, 0)),
                  pl.BlockSpec((1, ATT_HD), lambda i: (0, 0)),
                  pl.BlockSpec((1, ATT_HD), lambda i: (0, 0))],
        out_specs=[pl.BlockSpec((tm, ATT_W), lambda i: (i, 0)),
                   pl.BlockSpec((tm, ATT_KV_W), lambda i: (i, 0))],
        out_shape=[jax.ShapeDtypeStruct((t, ATT_W), BF16), jax.ShapeDtypeStruct((t, ATT_KV_W), BF16)],
        compiler_params=_params(("arbitrary",)),
    )(q, k, cos_t, sin_t, qn.reshape(1, ATT_HD), kn.reshape(1, ATT_HD))


def _flash_kernel(q_ref, k_ref, v_ref, o_ref, *, tk):
    tq = q_ref.shape[0]
    nk = k_ref.shape[0] // tk
    q2 = jnp.concatenate([q_ref[:, :ATT_HD], q_ref[:, ATT_HD:]], axis=0)
    m = l = acc = None
    for j in range(nk):
        sc = _dot_nt(q2, k_ref[j * tk:(j + 1) * tk, :])
        mx = jnp.max(sc, axis=-1, keepdims=True)
        if j == 0:
            m = mx
            p = jnp.exp2(sc - m)
            l = jnp.sum(p, axis=-1, keepdims=True)
            acc = _dot(p.astype(BF16), v_ref[j * tk:(j + 1) * tk, :])
        else:
            m_new = jnp.maximum(m, mx)
            p = jnp.exp2(sc - m_new)
            alpha = jnp.exp2(m - m_new)
            l = alpha * l + jnp.sum(p, axis=-1, keepdims=True)
            acc = alpha * acc + _dot(p.astype(BF16), v_ref[j * tk:(j + 1) * tk, :])
            m = m_new
    out = acc / l
    o_ref[...] = jnp.concatenate([out[:tq], out[tq:]], axis=1).astype(o_ref.dtype)


def _flash(qh, kh, v, batch, seq_len):
    t = qh.shape[0]
    tq = min(256, seq_len)
    tk = min(256, seq_len)
    nq = seq_len // tq
    rep_w = (ATT_H // ATT_KV) * ATT_HD
    return pl.pallas_call(
        functools.partial(_flash_kernel, tk=tk),
        name="flash",
        grid=(batch, ATT_KV, nq),
        in_specs=[pl.BlockSpec((tq, rep_w), lambda b, g, i: (b * nq + i, g)),
                  pl.BlockSpec((seq_len, ATT_HD), lambda b, g, i: (b, g)),
                  pl.BlockSpec((seq_len, ATT_HD), lambda b, g, i: (b, g))],
        out_specs=pl.BlockSpec((tq, rep_w), lambda b, g, i: (b * nq + i, g)),
        out_shape=jax.ShapeDtypeStruct((t, ATT_W), BF16),
        compiler_params=_params(("arbitrary",) * 3),
    )(qh, kh, v)


def _outproj_kernel(ys_ref, of_ref, ob_ref, zd_ref, ya_ref, x_ref, mod_ref, nwd_ref, nwp_ref, nwf_ref, wo_ref,
                    wr_ref, x1_ref, h2_ref, aff_ref):
    yd = []
    for h in range(DN_H):
        cols = slice(h * DN_HD, (h + 1) * DN_HD)
        o = _rms(of_ref[:, cols] + ob_ref[:, cols]) * nwd_ref[...]
        yd.append((o * _silu(zd_ref[:, cols].astype(F32))).astype(BF16))
    yd = jnp.concatenate(yd, axis=1)
    m = (_dot(ys_ref[...], wo_ref[0:SSD_W, :]) + _dot(yd, wo_ref[SSD_W:SSD_W + DN_W, :])
         + _dot(ya_ref[...], wo_ref[SSD_W + DN_W:, :]))
    mod = mod_ref[0]
    x1 = x_ref[...] + mod[2:3, :] * (_rms(m) * nwp_ref[...])
    x1_ref[...] = x1
    h2 = _rms(x1) * nwf_ref[...] * (1.0 + mod[4:5, :]) + mod[3:4, :]
    h2_hi = h2.astype(BF16)
    h2_ref[...] = h2_hi
    h2_lo = (h2 - h2_hi.astype(F32)).astype(BF16)
    logits = _dot(h2_hi, wr_ref[0]) + _dot(h2_lo, wr_ref[0]) + _dot(h2_hi, wr_ref[1])
    lane = lax.broadcasted_iota(I32, logits.shape, 1)
    logits = jnp.where(lane < N_EXPERTS, logits, -jnp.inf)
    ex = jnp.exp(logits - jnp.max(logits, axis=-1, keepdims=True))
    aff = ex / jnp.sum(ex, axis=-1, keepdims=True)
    aff_ref[...] = aff.T[0:N_EXPERTS, :]


def _outproj(ys, of, ob, zd, ya, x, mod, nwd, nwp, nwf, wo_bf, wr_p, seq_len):
    t = x.shape[0]
    tm = min(512, seq_len)
    per_seq = seq_len // tm
    row = lambda i: (i, 0)
    const = lambda i: (0, 0)
    return pl.pallas_call(
        _outproj_kernel,
        name="outproj",
        grid=(t // tm,),
        in_specs=[pl.BlockSpec((tm, SSD_W), row), pl.BlockSpec((tm, DN_W), row), pl.BlockSpec((tm, DN_W), row),
                  pl.BlockSpec((tm, DN_W), row), pl.BlockSpec((tm, ATT_W), row),
                  pl.BlockSpec((tm, D_MODEL), row),
                  pl.BlockSpec((1, 6, D_MODEL), lambda i: (i // per_seq, 0, 0)),
                  pl.BlockSpec((1, DN_HD), const),
                  pl.BlockSpec((1, D_MODEL), const), pl.BlockSpec((1, D_MODEL), const),
                  pl.BlockSpec((2 * D_MODEL, D_MODEL), const),
                  pl.BlockSpec((2, D_MODEL, LANES), lambda i: (0, 0, 0))],
        out_specs=[pl.BlockSpec((tm, D_MODEL), row), pl.BlockSpec((tm, D_MODEL), row),
                   pl.BlockSpec((N_EXPERTS, tm), lambda i: (0, i))],
        out_shape=[jax.ShapeDtypeStruct((t, D_MODEL), F32), jax.ShapeDtypeStruct((t, D_MODEL), BF16),
                   jax.ShapeDtypeStruct((N_EXPERTS, t), F32)],
        compiler_params=_params(("arbitrary",)),
    )(ys, of, ob, zd, ya, x, mod, nwd, nwp.reshape(1, D_MODEL), nwf.reshape(1, D_MODEL), wo_bf, wr_p)


def _route_kernel(aff_ref, pos_ref, offs_ref, *, cap):
    ne, t = aff_ref.shape
    nb = t // LANES
    capf = float(cap)

    def bits_of(x):
        return pltpu.bitcast(x, I32)

    def bis(i, thr):
        cand = thr | jnp.left_shift(jnp.int32(1), 30 - i)
        cnt = jnp.sum(jnp.where(bits_of(aff_ref[...]) >= cand, 1.0, 0.0), axis=1, keepdims=True)
        return jnp.where(cnt >= capf, cand, thr)

    thr = lax.fori_loop(0, 31, bis, jnp.zeros((ne, 1), I32))
    n_gt = jnp.sum(jnp.where(bits_of(aff_ref[...]) > thr, 1.0, 0.0), axis=1, keepdims=True)
    need_eq = capf - n_gt

    r = lax.broadcasted_iota(I32, (LANES, LANES), 0)
    s = lax.broadcasted_iota(I32, (LANES, LANES), 1)
    triu = jnp.where(r <= s, 1.0, 0.0).astype(BF16)
    lane_nb = lax.broadcasted_iota(I32, (ne, nb), 1)

    offs_ref[...] = jnp.zeros_like(offs_ref)

    def tile(i, carry):
        run_sel, run_eq = carry
        start = pl.multiple_of(i * LANES, LANES)
        b = bits_of(aff_ref[:, pl.ds(start, LANES)])
        gt = b > thr
        eq = jnp.where(b == thr, 1.0, 0.0)
        eq_rank = _dot(eq.astype(BF16), triu) - eq + run_eq
        sel = jnp.where(gt | ((eq > 0.0) & (eq_rank < need_eq)), 1.0, 0.0)
        pos = _dot(sel.astype(BF16), triu) - sel + run_sel
        pos_ref[:, pl.ds(start, LANES)] = jnp.where(sel > 0.0, pos, -1.0).astype(I32)
        offs_ref[...] = jnp.where(lane_nb == i, run_sel.astype(I32), offs_ref[...])
        return (run_sel + jnp.sum(sel, axis=1, keepdims=True),
                run_eq + jnp.sum(eq, axis=1, keepdims=True))

    zero = jnp.zeros((ne, 1), F32)
    lax.fori_loop(0, nb, tile, (zero, zero))


def _route(aff_t, cap):
    ne, t = aff_t.shape
    nb = t // LANES
    return pl.pallas_call(
        functools.partial(_route_kernel, cap=cap),
        name="route",
        out_shape=[jax.ShapeDtypeStruct((ne, t), I32), jax.ShapeDtypeStruct((ne, nb), I32)],
        compiler_params=pltpu.CompilerParams(vmem_limit_bytes=VMEM_LIMIT),
    )(aff_t)


MOE_SUB = 128
MOE_ALIGN = 16
MOE_WIN_SMALL = 32 + MOE_ALIGN
MOE_WIN_FULL = MOE_SUB + MOE_ALIGN
MOE_TILE = 288
MOE_HALF = MOE_TILE // 2
MOE_FC = 512
MOE_BLOCK = 2048


def _moe_kernel(offs_ref, h_ref, pos_ref, gate_ref, wg_ref, wu_ref, wd_ref, o_ref,
                xe_ref, ye_ref, gb_ref, *, cap, nsub):
    sb = pl.program_id(0)
    e = pl.program_id(1)
    nsb = pl.num_programs(0)
    nblk = nsb * nsub
    start = offs_ref[e, sb * nsub]
    end = jnp.where(sb == nsb - 1, cap, offs_ref[e, jnp.minimum((sb + 1) * nsub, nblk - 1)])
    count = end - start
    nfull = count // MOE_TILE

    @pl.when((sb == 0) & (e == 0))
    def _():
        xe_ref[...] = jnp.zeros_like(xe_ref)
        ye_ref[...] = jnp.zeros_like(ye_ref)
        gb_ref[...] = jnp.zeros_like(gb_ref)

    @pl.when(e == 0)
    def _():
        o_ref[...] = jnp.zeros_like(o_ref)

    def window(j):
        lo = offs_ref[e, sb * nsub + j] - start
        hi = (offs_ref[e, sb * nsub + j + 1] - start) if j + 1 < nsub else count
        a = pl.multiple_of((lo // MOE_ALIGN) * MOE_ALIGN, MOE_ALIGN)
        return lo, hi, a

    wins = [window(j) for j in range(nsub)]
    all_small = functools.reduce(jnp.logical_and, [hi - a <= MOE_WIN_SMALL for (_, hi, a) in wins])

    def onehots(win):
        rid = lax.broadcasted_iota(I32, (win, MOE_SUB), 0)
        out = []
        for j, (_, _, a) in enumerate(wins):
            posrow = pos_ref[0, :, j * MOE_SUB:(j + 1) * MOE_SUB]
            out.append((posrow - (start + a) == rid) & (posrow >= 0))
        return out, jnp.stack([jnp.where(oh, 1.0, 0.0).astype(BF16) for oh in out])

    def for_window_size(body):
        @pl.when(all_small)
        def _():
            body(MOE_WIN_SMALL)

        @pl.when(jnp.logical_not(all_small))
        def _():
            body(MOE_WIN_FULL)

    def gather(win):
        ohs, oh_b = onehots(win)
        comp = _bdot(oh_b, h_ref[...].reshape(nsub, MOE_SUB, D_MODEL))
        rid = lax.broadcasted_iota(I32, (MOE_ALIGN, 1), 0)
        for j, (lo, _, a) in enumerate(wins):
            gate = jnp.sum(jnp.where(ohs[j], gate_ref[0, :, j * MOE_SUB:(j + 1) * MOE_SUB], 0.0),
                           axis=1, keepdims=True)
            gate = jnp.broadcast_to(gate, (win, LANES))
            keep = rid < (lo - a)
            a1 = pl.multiple_of(a + MOE_ALIGN, MOE_ALIGN)
            xe_ref[pl.ds(a, MOE_ALIGN), :] = jnp.where(keep, xe_ref[pl.ds(a, MOE_ALIGN), :].astype(F32),
                                                       comp[j, :MOE_ALIGN]).astype(BF16)
            xe_ref[pl.ds(a1, win - MOE_ALIGN), :] = comp[j, MOE_ALIGN:].astype(BF16)
            gb_ref[pl.ds(a, MOE_ALIGN), :] = jnp.where(keep, gb_ref[pl.ds(a, MOE_ALIGN), :], gate[:MOE_ALIGN])
            gb_ref[pl.ds(a1, win - MOE_ALIGN), :] = gate[MOE_ALIGN:]

    def scatter(win):
        _, oh_b = onehots(win)
        yw = jnp.stack([(ye_ref[pl.ds(a, win), :] * gb_ref[pl.ds(a, win), 0:1]).astype(BF16) for (_, _, a) in wins])
        o_ref[...] += _bdot_tn(oh_b, yw).reshape(nsub * MOE_SUB, D_MODEL)

    for_window_size(gather)

    nfc = wg_ref.shape[3] // MOE_FC

    def ffn_rows(r0, rows):
        x = xe_ref[pl.ds(r0, rows), :]
        y = None
        for fc in range(nfc):
            cols = slice(fc * MOE_FC, (fc + 1) * MOE_FC)
            hid = (_silu(_dot(x, wg_ref[0, 0, :, cols])) * _dot(x, wu_ref[0, 0, :, cols])).astype(BF16)
            part = _dot(hid, wd_ref[0, 0, cols, :])
            y = part if y is None else y + part
        ye_ref[pl.ds(r0, rows), :] = y

    def ffn_tile(i, carry):
        ffn_rows(pl.multiple_of(i * MOE_TILE, MOE_ALIGN), MOE_TILE)
        return carry

    lax.fori_loop(0, nfull, ffn_tile, 0)
    rem = count - nfull * MOE_TILE
    rem0 = pl.multiple_of(nfull * MOE_TILE, MOE_ALIGN)

    @pl.when(rem > MOE_HALF)
    def _():
        ffn_rows(rem0, MOE_TILE)

    @pl.when((rem > 0) & (rem <= MOE_HALF))
    def _():
        ffn_rows(rem0, MOE_HALF)

    for_window_size(scatter)


def _moe(offs, h2, pos, aff_t, wg, wu, wd, layer, cap, sblk):
    t = h2.shape[0]
    nsb = t // sblk
    nsub = sblk // MOE_SUB
    rows = sblk + MOE_TILE + MOE_WIN_FULL
    once = pl.Buffered(1)
    grid_spec = pltpu.PrefetchScalarGridSpec(
        num_scalar_prefetch=1,
        grid=(nsb, N_EXPERTS),
        in_specs=[pl.BlockSpec((sblk, D_MODEL), lambda s, e, o: (s, 0), pipeline_mode=once),
                  pl.BlockSpec((1, 1, sblk), lambda s, e, o: (e, 0, s)),
                  pl.BlockSpec((1, 1, sblk), lambda s, e, o: (e, 0, s)),
                  pl.BlockSpec((1, 1, D_MODEL, EXPERT_FF), lambda s, e, o: (layer, e, 0, 0)),
                  pl.BlockSpec((1, 1, D_MODEL, EXPERT_FF), lambda s, e, o: (layer, e, 0, 0)),
                  pl.BlockSpec((1, 1, EXPERT_FF, D_MODEL), lambda s, e, o: (layer, e, 0, 0))],
        out_specs=pl.BlockSpec((sblk, D_MODEL), lambda s, e, o: (s, 0), pipeline_mode=once),
        scratch_shapes=[pltpu.VMEM((rows, D_MODEL), BF16), pltpu.VMEM((rows, D_MODEL), F32),
                        pltpu.VMEM((rows, LANES), F32)],
    )
    return pl.pallas_call(
        functools.partial(_moe_kernel, cap=cap, nsub=nsub),
        name="moe",
        grid_spec=grid_spec,
        out_shape=jax.ShapeDtypeStruct((t, D_MODEL), F32),
        compiler_params=pltpu.CompilerParams(dimension_semantics=("arbitrary",) * 2,
                                             vmem_limit_bytes=MOE_VMEM_LIMIT),
    )(offs, h2, pos.reshape(N_EXPERTS, 1, t), aff_t.reshape(N_EXPERTS, 1, t), wg, wu, wd)


def _post_kernel(x_ref, f_ref, mod_ref, nw_ref, o_ref):
    mod = mod_ref[0]
    o_ref[...] = x_ref[...] + mod[5:6, :] * (_rms(f_ref[...]) * nw_ref[...])


def _post(x1, f, mod, nw, seq_len):
    t = x1.shape[0]
    tm = min(512, seq_len)
    per_seq = seq_len // tm
    row = lambda i: (i, 0)
    return pl.pallas_call(
        _post_kernel,
        name="post",
        grid=(t // tm,),
        in_specs=[pl.BlockSpec((tm, D_MODEL), row), pl.BlockSpec((tm, D_MODEL), row),
                  pl.BlockSpec((1, 6, D_MODEL), lambda i: (i // per_seq, 0, 0)),
                  pl.BlockSpec((1, D_MODEL), lambda i: (0, 0))],
        out_specs=pl.BlockSpec((tm, D_MODEL), row),
        out_shape=jax.ShapeDtypeStruct((t, D_MODEL), F32),
        compiler_params=_params(("arbitrary",)),
    )(x1, f, mod, nw.reshape(1, D_MODEL))


def _pad_lanes(v, offset=0):
    out = jnp.zeros((1, LANES), F32)
    return out.at[0, offset:offset + v.shape[0]].set(v.astype(F32))


def _split_bf16(w):
    hi = w.astype(BF16)
    return jnp.stack([hi, (w - hi.astype(F32)).astype(BF16)])


def _prep_layer(p, l):
    w_in = p["w_in"][l]
    cols, off = [], 0
    for (_, width, stored, _) in _IN_SEGS:
        seg = w_in[:, off:off + width]
        if stored != width:
            seg = jnp.pad(seg, ((0, 0), (0, stored - width)))
        cols.append(seg)
        off += width
    q = {
        "w_ada": p["w_ada"][l].astype(BF16), "b_ada": p["b_ada"][l],
        "w_in": jnp.concatenate(cols, axis=1).astype(BF16),
        "w_out": p["w_out"][l].astype(BF16),
        "w_router": _split_bf16(jnp.pad(p["w_router"][l], ((0, 0), (0, LANES - N_EXPERTS)))),
        "ssd_bias": [_pad_lanes(p["ssd_dt_bias"][l, d]) for d in range(2)],
        "ssd_alog": [_pad_lanes(p["ssd_a_log"][l, d]) for d in range(2)],
        "ssd_d": jnp.repeat(p["ssd_d"][l], SSD_P).reshape(1, SSD_W),
        "ssd_norm": p["ssd_norm"][l].reshape(1, SSD_W),
        "dn_bias": jnp.concatenate([_pad_lanes(p["dn_dt_bias"][l, d], DN_H) for d in range(2)], axis=0),
        "dn_alog": jnp.concatenate([_pad_lanes(p["dn_a_log"][l, d], DN_H) for d in range(2)], axis=0),
        "dn_norm": p["dn_norm"][l].reshape(1, DN_HD),
    }
    for name in ("norm_mix_pre", "norm_mix_post", "norm_ffn_pre", "norm_ffn_post", "conv_ssd_w", "conv_ssd_b",
                 "conv_dn_w", "q_norm", "k_norm"):
        q[name] = p[name][l]
    return q


def _trunk(x, c, layers, experts, sblk):
    batch, seq_len, _ = x.shape
    t = batch * seq_len
    cap = EC_CAPACITY * t // N_EXPERTS
    cos_t, sin_t = _rope_tables(seq_len)
    x = x.reshape(t, D_MODEL)
    for l, q in enumerate(layers):
        mod = _ada(c, q["w_ada"], q["b_ada"])
        z_ssd, xbc, dtp, qkv_dn, z_dn, ba_dn, qa, ka, va = _inproj(x, mod, q["norm_mix_pre"], q["w_in"], seq_len)
        xact = _conv(xbc, q["conv_ssd_w"], q["conv_ssd_b"], seq_len)
        qkvact = _conv(qkv_dn, q["conv_dn_w"], jnp.zeros((DN_CONV_DIM,), F32), seq_len)
        yb = _ssd(xact, dtp, q["ssd_bias"][1], q["ssd_alog"][1], batch, seq_len, True)
        y_ssd = _ssd(xact, dtp, q["ssd_bias"][0], q["ssd_alog"][0], batch, seq_len, False,
                     z=z_ssd, yb=yb, dskip=q["ssd_d"], nw=q["ssd_norm"])
        o_f, o_b = _dn(qkvact, ba_dn, q["dn_bias"], q["dn_alog"], batch, seq_len)
        qh, kh = _qkprep(qa, ka, cos_t, sin_t, q["q_norm"], q["k_norm"], seq_len)
        y_att = _flash(qh, kh, va, batch, seq_len)
        x1, h2, aff_t = _outproj(y_ssd, o_f, o_b, z_dn, y_att, x, mod, q["dn_norm"], q["norm_mix_post"],
                                 q["norm_ffn_pre"], q["w_out"], q["w_router"], seq_len)
        pos, offs = _route(aff_t, cap)
        f = _moe(offs, h2, pos, aff_t, *experts, l, cap, sblk)
        x = _post(x1, f, mod, q["norm_ffn_post"], seq_len)
    return x.reshape(batch, seq_len, D_MODEL)


def kernel(x_prompt, x_sample, c_prompt, c_sample, w_ada, b_ada, norm_mix_pre, norm_mix_post, w_in, conv_ssd_w, conv_ssd_b, ssd_dt_bias, ssd_a_log, ssd_d, ssd_norm, conv_dn_w, dn_dt_bias, dn_a_log, dn_norm, q_norm, k_norm, w_out, norm_ffn_pre, norm_ffn_post, w_router, w_gate, w_up, w_down):
    p = dict(w_ada=w_ada, b_ada=b_ada, norm_mix_pre=norm_mix_pre, norm_mix_post=norm_mix_post,
             w_in=w_in, conv_ssd_w=conv_ssd_w, conv_ssd_b=conv_ssd_b, ssd_dt_bias=ssd_dt_bias,
             ssd_a_log=ssd_a_log, ssd_d=ssd_d, ssd_norm=ssd_norm, conv_dn_w=conv_dn_w,
             dn_dt_bias=dn_dt_bias, dn_a_log=dn_a_log, dn_norm=dn_norm, q_norm=q_norm, k_norm=k_norm,
             w_out=w_out, norm_ffn_pre=norm_ffn_pre, norm_ffn_post=norm_ffn_post,
             w_router=w_router, w_gate=w_gate, w_up=w_up, w_down=w_down)
    layers = [_prep_layer(p, l) for l in range(w_in.shape[0])]
    experts = (w_gate.astype(BF16), w_up.astype(BF16), w_down.astype(BF16))
    y_prompt = _trunk(x_prompt, c_prompt, layers, experts, MOE_BLOCK)
    y_sample = _trunk(x_sample, c_sample, layers, experts, MOE_BLOCK)
    return (y_prompt, y_sample)
```

```python
import functools

import jax
import jax.numpy as jnp
import numpy as np
from jax import lax
from jax.experimental import pallas as pl
from jax.experimental.pallas import tpu as pltpu

F32 = jnp.float32
BF16 = jnp.bfloat16
I32 = jnp.int32

D_MODEL = 1024
SSD_W = 1024
SSD_P = 64
SSD_H = 16
SSD_G = 2
SSD_N = 128
SSD_CONV_DIM = SSD_W + 2 * SSD_G * SSD_N
DN_W = 512
DN_HD = 128
DN_H = 4
DN_CONV_DIM = 3 * DN_W
ATT_W = 512
ATT_HD = 128
ATT_H = 4
ATT_KV = 2
ATT_KV_W = ATT_KV * ATT_HD
GRID_W = 64
ROPE_THETA = 10000.0
N_EXPERTS = 16
EC_CAPACITY = 2
EXPERT_FF = 2048
EPS = 1e-6
CONV_K = 5

LANES = 128
SUBLANES = 8
VMEM_LIMIT = 56 * 1024 * 1024
MOE_VMEM_LIMIT = 60 * 1024 * 1024

_IN_SEGS = (
    ("z_ssd", SSD_W, SSD_W, BF16),
    ("xbc", SSD_CONV_DIM, SSD_CONV_DIM, BF16),
    ("dt", SSD_H, LANES, F32),
    ("qkv_dn", DN_CONV_DIM, DN_CONV_DIM, BF16),
    ("z_dn", DN_W, DN_W, BF16),
    ("ba_dn", 2 * DN_H, LANES, F32),
    ("q", ATT_W, ATT_W, BF16),
    ("k", ATT_KV_W, ATT_KV_W, BF16),
    ("v", ATT_KV_W, ATT_KV_W, BF16),
)


def _params(sem):
    return pltpu.CompilerParams(dimension_semantics=sem, vmem_limit_bytes=VMEM_LIMIT)


def _sigmoid(x):
    return 1.0 / (1.0 + jnp.exp(-x))


def _silu(x):
    return x * _sigmoid(x)


def _softplus(x):
    return jnp.maximum(x, 0.0) + jnp.log(1.0 + jnp.exp(-jnp.abs(x)))


def _rms(x):
    return x * lax.rsqrt(jnp.mean(x * x, axis=-1, keepdims=True) + EPS)


def _dot(a, b):
    return jnp.dot(a, b, preferred_element_type=F32)


def _dot_nt(a, b):
    return lax.dot_general(a, b, (((1,), (1,)), ((), ())), preferred_element_type=F32)


def _dot_tn(a, b):
    return lax.dot_general(a, b, (((0,), (0,)), ((), ())), preferred_element_type=F32)


def _cumsum_mm(a, b, split_lhs=False):
    x = a if split_lhs else b
    hi = x.astype(BF16)
    r1 = x - hi.astype(F32)
    mid = r1.astype(BF16)
    lo = (r1 - mid.astype(F32)).astype(BF16)
    if split_lhs:
        return _dot(hi, b) + _dot(mid, b) + _dot(lo, b)
    return _dot(a, hi) + _dot(a, mid) + _dot(a, lo)


def _ada_kernel(c_ref, w_ref, b_ref, o_ref):
    o_ref[...] = _dot(_silu(c_ref[...]).astype(BF16), w_ref[...]) + b_ref[...]


def _ada(c, w_bf, b):
    nb = c.shape[0]
    rows = -(-nb // SUBLANES) * SUBLANES
    cp = jnp.zeros((rows, D_MODEL), F32).at[:nb].set(c)
    n = w_bf.shape[1]
    tn = 1024
    out = pl.pallas_call(
        _ada_kernel,
        name="ada",
        grid=(n // tn,),
        in_specs=[pl.BlockSpec((rows, D_MODEL), lambda j: (0, 0)),
                  pl.BlockSpec((D_MODEL, tn), lambda j: (0, j)),
                  pl.BlockSpec((1, tn), lambda j: (0, j))],
        out_specs=pl.BlockSpec((rows, tn), lambda j: (0, j)),
        out_shape=jax.ShapeDtypeStruct((rows, n), F32),
        compiler_params=_params(("arbitrary",)),
    )(cp, w_bf, b.reshape(1, n))
    return out[:nb].reshape(nb, 6, D_MODEL)


def _inproj_kernel(x_ref, mod_ref, nw_ref, w_ref, *out_refs):
    mod = mod_ref[0]
    h = _rms(x_ref[...]) * nw_ref[...]
    h = h * (1.0 + mod[1:2, :]) + mod[0:1, :]
    hb = h.astype(BF16)
    off = 0
    for o_ref in out_refs:
        width = o_ref.shape[1]
        o_ref[...] = _dot(hb, w_ref[:, off:off + width]).astype(o_ref.dtype)
        off += width


def _inproj(x, mod, nw, w_p, seq_len):
    t = x.shape[0]
    tm = min(512, seq_len)
    per_seq = seq_len // tm
    ntot = w_p.shape[1]
    out_shapes = [jax.ShapeDtypeStruct((t, sw), dt) for (_, _, sw, dt) in _IN_SEGS]
    out_specs = [pl.BlockSpec((tm, sw), lambda i: (i, 0)) for (_, _, sw, _) in _IN_SEGS]
    return pl.pallas_call(
        _inproj_kernel,
        name="inproj",
        grid=(t // tm,),
        in_specs=[pl.BlockSpec((tm, D_MODEL), lambda i: (i, 0)),
                  pl.BlockSpec((1, 6, D_MODEL), lambda i: (i // per_seq, 0, 0)),
                  pl.BlockSpec((1, D_MODEL), lambda i: (0, 0)),
                  pl.BlockSpec((D_MODEL, ntot), lambda i: (0, 0))],
        out_specs=out_specs,
        out_shape=out_shapes,
        compiler_params=_params(("arbitrary",)),
    )(x, mod, nw.reshape(1, D_MODEL), w_p)


_HALO = 16


def _conv_kernel(prev_ref, x_ref, next_ref, w_ref, b_ref, o_ref, buf_ref, *, per_seq):
    i = pl.program_id(0)
    tm = x_ref.shape[0]
    first = (i % per_seq) == 0
    last = (i % per_seq) == per_seq - 1
    prev = prev_ref[...].astype(F32)
    nxt = next_ref[...].astype(F32)
    buf_ref[0:_HALO, :] = jnp.where(first, jnp.zeros_like(prev), prev)
    buf_ref[_HALO:_HALO + tm, :] = x_ref[...].astype(F32)
    buf_ref[_HALO + tm:2 * _HALO + tm, :] = jnp.where(last, jnp.zeros_like(nxt), nxt)
    w = w_ref[...]
    acc = jnp.zeros((tm, x_ref.shape[1]), F32) + b_ref[...]
    for k in range(CONV_K):
        acc = acc + buf_ref[_HALO - CONV_K // 2 + k:_HALO - CONV_K // 2 + k + tm, :] * w[k:k + 1, :]
    o_ref[...] = _silu(acc).astype(o_ref.dtype)


def _conv(x, w, b, seq_len):
    t, c = x.shape
    tm = min(512, seq_len)
    per_seq = seq_len // tm
    hb = tm // _HALO
    nh = t // _HALO
    wp = jnp.zeros((SUBLANES, c), F32).at[:CONV_K].set(w)
    return pl.pallas_call(
        functools.partial(_conv_kernel, per_seq=per_seq),
        name="conv",
        grid=(t // tm,),
        in_specs=[pl.BlockSpec((_HALO, c), lambda i: (jnp.maximum(i * hb - 1, 0), 0)),
                  pl.BlockSpec((tm, c), lambda i: (i, 0)),
                  pl.BlockSpec((_HALO, c), lambda i: (jnp.minimum((i + 1) * hb, nh - 1), 0)),
                  pl.BlockSpec((SUBLANES, c), lambda i: (0, 0)),
                  pl.BlockSpec((1, c), lambda i: (0, 0))],
        out_specs=pl.BlockSpec((tm, c), lambda i: (i, 0)),
        out_shape=jax.ShapeDtypeStruct((t, c), BF16),
        scratch_shapes=[pltpu.VMEM((tm + 2 * _HALO, c), F32)],
        compiler_params=_params(("arbitrary",)),
    )(x, x, x, wp, b.reshape(1, c))


SSD_Q = 128


def _ssd_kernel(xf_ref, dtf_ref, xb_ref, dtb_ref, bias_ref, alog_ref, sel_ref, yf_ref, yb_ref, h_ref):
    q = SSD_Q
    dirs = (0, 1)

    @pl.when(pl.program_id(1) == 0)
    def _():
        h_ref[...] = jnp.zeros_like(h_ref)

    r = lax.broadcasted_iota(I32, (q, q), 0)
    s = lax.broadcasted_iota(I32, (q, q), 1)
    mask = (r >= s, s >= r)
    tri = [jnp.where(m, 1.0, 0.0).astype(BF16) for m in mask]
    lo = s < SSD_P
    last = (q - 1, 0)

    dtp = (dtf_ref[...], dtb_ref[...])
    dt = [_softplus(dtp[d] + bias_ref[d:d + 1, :]) for d in dirs]
    dta = [dt[d] * (-jnp.exp(alog_ref[d:d + 1, :])) for d in dirs]
    acs = [_cumsum_mm(tri[d], dta[d]) for d in dirs]
    acs_t = [acs[d].T for d in dirs]
    dt_t = [dt[d].T for d in dirs]
    eacs = [jnp.exp(acs[d]) for d in dirs]
    wcols = [jnp.exp(acs[d][last[d]:last[d] + 1, :] - acs[d]) * dt[d] for d in dirs]
    cols = jnp.concatenate([eacs[0], wcols[0], eacs[1], wcols[1]], axis=0).astype(BF16)
    spread = _dot(cols, sel_ref[...])
    eacs_w = [spread[2 * d * q:(2 * d + 1) * q] for d in dirs]
    wcols_w = [spread[(2 * d + 1) * q:(2 * d + 2) * q] for d in dirs]
    dec_w = _cumsum_mm(jnp.concatenate([eacs[d][last[d]:last[d] + 1, :] for d in dirs], axis=0), sel_ref[...],
                       split_lhs=True)

    xact = (xf_ref[...], xb_ref[...])
    ys = ([], [])
    for g in range(SSD_G):
        bm = [xact[d][:, SSD_W + g * SSD_N:SSD_W + (g + 1) * SSD_N] for d in dirs]
        cm = [xact[d][:, SSD_W + (SSD_G + g) * SSD_N:SSD_W + (SSD_G + g + 1) * SSD_N] for d in dirs]
        cb = [_dot_nt(cm[d], bm[d]) for d in dirs]
        hg = [h_ref[d, g] for d in dirs]
        cmh = [_dot(cm[d], hg[d].astype(BF16)) for d in dirs]
        xw_parts = ([], [])
        for pp in range(4):
            j0 = g * 8 + pp * 2
            lanes = slice((g * 4 + pp) * LANES, (g * 4 + pp + 1) * LANES)
            xpair = [xact[d][:, lanes].astype(F32) for d in dirs]
            ms = ([], [])
            for j in (j0, j0 + 1):
                seg = [acs[d][:, j:j + 1] - acs_t[d][j:j + 1, :] for d in dirs]
                lm = [jnp.exp(jnp.where(mask[d], seg[d], -jnp.inf)) for d in dirs]
                for d in dirs:
                    ms[d].append((cb[d] * lm[d] * dt_t[d][j:j + 1, :]).astype(BF16))
            mcat = [jnp.concatenate(ms[d], axis=1) for d in dirs]
            x2 = [jnp.concatenate([jnp.where(lo, xpair[d], 0.0), jnp.where(lo, 0.0, xpair[d])], axis=0).astype(BF16)
                  for d in dirs]
            yd = [_dot(mcat[d], x2[d]) for d in dirs]
            for d in dirs:
                ys[d].append(yd[d] + cmh[d][:, pp * LANES:(pp + 1) * LANES] * eacs_w[d][:, lanes])
                xw_parts[d].append((xpair[d] * wcols_w[d][:, lanes]).astype(BF16))
        half = slice(g * (SSD_W // SSD_G), (g + 1) * (SSD_W // SSD_G))
        upd = [_dot_tn(bm[d], jnp.concatenate(xw_parts[d], axis=1)) for d in dirs]
        for d in dirs:
            h_ref[d, g] = hg[d] * dec_w[d:d + 1, half] + upd[d]
    yf_ref[...] = jnp.concatenate(ys[0], axis=1)
    yb_ref[...] = jnp.concatenate(ys[1], axis=1)


def _ssd(xact, dtp, bias, alog, batch, seq_len):
    t = xact.shape[0]
    nc = seq_len // SSD_Q
    fwd = lambda b, c: (b * nc + c, 0)
    bwd = lambda b, c: (b * nc + nc - 1 - c, 0)
    const = lambda b, c: (0, 0)
    sel = jnp.repeat(jnp.eye(LANES, SSD_H, dtype=BF16), SSD_P, axis=1)
    return pl.pallas_call(
        _ssd_kernel,
        name="ssd",
        grid=(batch, nc),
        in_specs=[pl.BlockSpec((SSD_Q, SSD_CONV_DIM), fwd), pl.BlockSpec((SSD_Q, LANES), fwd),
                  pl.BlockSpec((SSD_Q, SSD_CONV_DIM), bwd), pl.BlockSpec((SSD_Q, LANES), bwd),
                  pl.BlockSpec((2, LANES), const), pl.BlockSpec((2, LANES), const),
                  pl.BlockSpec((LANES, SSD_W), const)],
        out_specs=[pl.BlockSpec((SSD_Q, SSD_W), fwd), pl.BlockSpec((SSD_Q, SSD_W), bwd)],
        out_shape=[jax.ShapeDtypeStruct((t, SSD_W), F32)] * 2,
        scratch_shapes=[pltpu.VMEM((2, SSD_G, SSD_N, SSD_W // SSD_G), F32)],
        compiler_params=_params(("arbitrary", "arbitrary")),
    )(xact, dtp, xact, dtp, bias, alog, sel)


DN_BLK = 128
DN_C = 64


def _dn_masks(reverse):
    n = DN_BLK
    r = lax.broadcasted_iota(I32, (n, n), 0)
    s = lax.broadcasted_iota(I32, (n, n), 1)
    same = (r >= DN_C) == (s >= DN_C)
    incl = same & ((s >= r) if reverse else (r >= s))
    strict = same & ((s > r) if reverse else (r > s))
    return r, incl, strict, jnp.where(incl, 1.0, 0.0).astype(BF16), jnp.where(r == s, 1.0, 0.0)


def _bdot(a, b):
    return lax.dot_general(a, b, (((2,), (1,)), ((0,), (0,))), preferred_element_type=F32)


def _bdot_nt(a, b):
    return lax.dot_general(a, b, (((2,), (2,)), ((0,), (0,))), preferred_element_type=F32)


def _bdot_tn(a, b):
    return lax.dot_general(a, b, (((1,), (1,)), ((0,), (0,))), preferred_element_type=F32)


def _dn_group(qkvs, bas, dtb, alog, states, reverse):
    n = DN_BLK
    nseq = len(qkvs)
    r, incl, strict, tri, eye = _dn_masks(reverse)
    ba = jnp.concatenate(bas, axis=1)
    beta = _sigmoid(ba)
    gl = -jnp.exp(jnp.concatenate([alog] * nseq, axis=1)) * _softplus(ba + jnp.concatenate([dtb] * nseq, axis=1))
    gcs = _cumsum_mm(tri, gl)
    if reverse:
        t0, t1 = gcs[0:1, :], gcs[DN_C:DN_C + 1, :]
    else:
        t0, t1 = gcs[DN_C - 1:DN_C, :], gcs[n - 1:n, :]
    eg = jnp.exp(gcs)
    ekd = jnp.exp(jnp.where(r[:, 0:1] < DN_C, t0, t1) - gcs)
    dec0 = jnp.exp(t0)
    dec1 = jnp.exp(t1)

    qhb, qg, khb, kd, kbs, kes, vbs, decays, d0s, d1s = [], [], [], [], [], [], [], [], [], []
    for b in range(nseq):
        gcs_t = gcs[:, b * LANES:(b + 1) * LANES].T
        for h in range(DN_H):
            lb = b * LANES + h
            la = lb + DN_H
            qh = qkvs[b][:, h * DN_HD:(h + 1) * DN_HD].astype(F32)
            kh = qkvs[b][:, DN_W + h * DN_HD:DN_W + (h + 1) * DN_HD].astype(F32)
            vh = qkvs[b][:, 2 * DN_W + h * DN_HD:2 * DN_W + (h + 1) * DN_HD].astype(F32)
            qh = qh * lax.rsqrt(jnp.sum(qh * qh, axis=-1, keepdims=True) + EPS) * (DN_HD ** -0.5)
            kh = kh * lax.rsqrt(jnp.sum(kh * kh, axis=-1, keepdims=True) + EPS)
            bcol = beta[:, lb:lb + 1]
            decays.append(jnp.exp(jnp.where(incl, gcs[:, la:la + 1] - gcs_t[DN_H + h:DN_H + h + 1, :], -jnp.inf)))
            kb = kh * bcol
            qhb.append(qh.astype(BF16))
            qg.append((qh * eg[:, la:la + 1]).astype(BF16))
            khb.append(kh.astype(BF16))
            kd.append((kh * ekd[:, la:la + 1]).astype(BF16))
            kbs.append(kb.astype(BF16))
            kes.append((kb * eg[:, la:la + 1]).astype(BF16))
            vbs.append((vh * bcol).astype(BF16))
            d0s.append(dec0[:, la:la + 1])
            d1s.append(dec1[:, la:la + 1])
    decay = jnp.stack(decays)
    khb = jnp.stack(khb)
    kd = jnp.stack(kd)
    qg = jnp.stack(qg)
    nm = jnp.where(strict, _bdot_nt(jnp.stack(kbs), khb) * decay, 0.0)
    p = -nm
    inv = eye + p
    pb = p.astype(BF16)
    for _ in range(5):
        p = _bdot(pb, pb)
        pb = p.astype(BF16)
        inv = inv + _bdot(inv.astype(BF16), pb)
    uw = _bdot(inv.astype(BF16), jnp.concatenate([jnp.stack(vbs), jnp.stack(kes)], axis=2))
    u = uw[:, :, :DN_HD]
    w = uw[:, :, DN_HD:].astype(BF16)
    qk = (_bdot_nt(jnp.stack(qhb), khb) * decay).astype(BF16)
    dec = (jnp.stack(d0s), jnp.stack(d1s))
    st = states
    zeros_c = jnp.zeros((nseq * DN_H, DN_C, DN_HD), F32)
    o_parts = [None, None]
    for ci in ((1, 0) if reverse else (0, 1)):
        rows = slice(ci * DN_C, (ci + 1) * DN_C)
        ws = _bdot(jnp.concatenate([w[:, rows], qg[:, rows]], axis=1), st.astype(BF16))
        vnew = u[:, rows] - ws[:, :DN_C]
        vpad = jnp.concatenate([vnew, zeros_c] if ci == 0 else [zeros_c, vnew], axis=1).astype(BF16)
        o_parts[ci] = ws[:, DN_C:] + _bdot(qk[:, rows], vpad)
        st = st * dec[ci] + _bdot_tn(kd[:, rows], vnew.astype(BF16))
    o = jnp.concatenate(o_parts, axis=1)
    outs = [jnp.concatenate([o[b * DN_H + h] for h in range(DN_H)], axis=1) for b in range(nseq)]
    return outs, st


def _dn_kernel(qf_ref, baf_ref, qb_ref, bab_ref, dtb_ref, alog_ref, of_ref, ob_ref, s_ref):
    @pl.when(pl.program_id(0) == 0)
    def _():
        s_ref[...] = jnp.zeros_like(s_ref)

    nseq = qf_ref.shape[0]
    for d, (q_ref, ba_ref, o_ref) in enumerate(((qf_ref, baf_ref, of_ref), (qb_ref, bab_ref, ob_ref))):
        outs, st = _dn_group([q_ref[b] for b in range(nseq)], [ba_ref[b] for b in range(nseq)],
                             dtb_ref[d:d + 1, :], alog_ref[d:d + 1, :], s_ref[d], bool(d))
        for b in range(nseq):
            o_ref[b] = outs[b]
        s_ref[d] = st


def _dn(qkv, ba, dtb, alog, batch, seq_len):
    nb = seq_len // DN_BLK
    qkv3 = qkv.reshape(batch, seq_len, DN_CONV_DIM)
    ba3 = ba.reshape(batch, seq_len, LANES)
    fwd = lambda c: (0, c, 0)
    bwd = lambda c: (0, nb - 1 - c, 0)
    const = lambda c: (0, 0)
    of, ob = pl.pallas_call(
        _dn_kernel,
        name="dn",
        grid=(nb,),
        in_specs=[pl.BlockSpec((batch, DN_BLK, DN_CONV_DIM), fwd), pl.BlockSpec((batch, DN_BLK, LANES), fwd),
                  pl.BlockSpec((batch, DN_BLK, DN_CONV_DIM), bwd), pl.BlockSpec((batch, DN_BLK, LANES), bwd),
                  pl.BlockSpec((2, LANES), const), pl.BlockSpec((2, LANES), const)],
        out_specs=[pl.BlockSpec((batch, DN_BLK, DN_W), fwd), pl.BlockSpec((batch, DN_BLK, DN_W), bwd)],
        out_shape=[jax.ShapeDtypeStruct((batch, seq_len, DN_W), F32)] * 2,
        scratch_shapes=[pltpu.VMEM((2, batch * DN_H, DN_HD, DN_HD), F32)],
        compiler_params=_params(("arbitrary",)),
    )(qkv3, ba3, qkv3, ba3, dtb, alog)
    t = batch * seq_len
    return of.reshape(t, DN_W), ob.reshape(t, DN_W)


def _rope_tables(seq_len):
    rows = seq_len // GRID_W
    row_idx = jnp.repeat(jnp.arange(rows, dtype=F32), GRID_W)
    col_idx = jnp.tile(jnp.arange(GRID_W, dtype=F32), rows)
    axis_dim = ATT_HD // 2
    inv_freq = jnp.power(ROPE_THETA, -jnp.arange(0, axis_dim, 2, dtype=F32) / axis_dim)
    ra = row_idx[:, None] * inv_freq
    ca = col_idx[:, None] * inv_freq
    cos_t = jnp.concatenate([jnp.cos(ra), jnp.cos(ra), jnp.cos(ca), jnp.cos(ca)], axis=1)
    sin_t = jnp.concatenate([-jnp.sin(ra), jnp.sin(ra), -jnp.sin(ca), jnp.sin(ca)], axis=1)
    return cos_t, sin_t


def _qkprep_kernel(q_ref, k_ref, cos_ref, sin_ref, qn_ref, kn_ref, qo_ref, ko_ref):
    cos_t = cos_ref[...]
    sin_t = sin_ref[...]
    lane = lax.broadcasted_iota(I32, cos_t.shape, 1)
    low = (lane & (ATT_HD // 4)) == 0

    def prep(x, nw, scale):
        x = _rms(x.astype(F32)) * nw
        partner = jnp.where(low, pltpu.roll(x, ATT_HD - ATT_HD // 4, 1), pltpu.roll(x, ATT_HD // 4, 1))
        return (x * cos_t + partner * sin_t) * scale

    qscale = ATT_HD ** -0.5 * float(np.log2(np.e))
    qs = [prep(q_ref[:, h * ATT_HD:(h + 1) * ATT_HD], qn_ref[...], qscale) for h in range(ATT_H)]
    ks = [prep(k_ref[:, h * ATT_HD:(h + 1) * ATT_HD], kn_ref[...], 1.0) for h in range(ATT_KV)]
    qo_ref[...] = jnp.concatenate(qs, axis=1).astype(qo_ref.dtype)
    ko_ref[...] = jnp.concatenate(ks, axis=1).astype(ko_ref.dtype)


def _qkprep(q, k, cos_t, sin_t, qn, kn, seq_len):
    t = q.shape[0]
    tm = min(512, seq_len)
    per_seq = seq_len // tm
    return pl.pallas_call(
        _qkprep_kernel,
        name="qkprep",
        grid=(t // tm,),
        in_specs=[pl.BlockSpec((tm, ATT_W), lambda i: (i, 0)),
                  pl.BlockSpec((tm, ATT_KV_W), lambda i: (i, 0)),
                  pl.BlockSpec((tm, ATT_HD), lambda i: (i % per_seq, 0)),
                  pl.BlockSpec((tm, ATT_HD), lambda i: (i % per_seq, 0)),
                  pl.BlockSpec((1, ATT_HD), lambda i: (0, 0)),
                  pl.BlockSpec((1, ATT_HD), lambda i: (0, 0))],
        out_specs=[pl.BlockSpec((tm, ATT_W), lambda i: (i, 0)),
                   pl.BlockSpec((tm, ATT_KV_W), lambda i: (i, 0))],
        out_shape=[jax.ShapeDtypeStruct((t, ATT_W), BF16), jax.ShapeDtypeStruct((t, ATT_KV_W), BF16)],
        compiler_params=_params(("arbitrary",)),
    )(q, k, cos_t, sin_t, qn.reshape(1, ATT_HD), kn.reshape(1, ATT_HD))


def _flash_kernel(q_ref, k_ref, v_ref, o_ref, *, tk):
    tq = q_ref.shape[0]
    nk = k_ref.shape[0] // tk
    q2 = jnp.concatenate([q_ref[:, :ATT_HD], q_ref[:, ATT_HD:]], axis=0)
    m = l = acc = None
    for j in range(nk):
        sc = _dot_nt(q2, k_ref[j * tk:(j + 1) * tk, :])
        mx = jnp.max(sc, axis=-1, keepdims=True)
        if j == 0:
            m = mx
            p = jnp.exp2(sc - m)
            l = jnp.sum(p, axis=-1, keepdims=True)
            acc = _dot(p.astype(BF16), v_ref[j * tk:(j + 1) * tk, :])
        else:
            m_new = jnp.maximum(m, mx)
            p = jnp.exp2(sc - m_new)
            alpha = jnp.exp2(m - m_new)
            l = alpha * l + jnp.sum(p, axis=-1, keepdims=True)
            acc = alpha * acc + _dot(p.astype(BF16), v_ref[j * tk:(j + 1) * tk, :])
            m = m_new
    out = acc / l
    o_ref[...] = jnp.concatenate([out[:tq], out[tq:]], axis=1).astype(o_ref.dtype)


def _flash(qh, kh, v, batch, seq_len):
    t = qh.shape[0]
    tq = min(256, seq_len)
    tk = min(256, seq_len)
    nq = seq_len // tq
    rep_w = (ATT_H // ATT_KV) * ATT_HD
    return pl.pallas_call(
        functools.partial(_flash_kernel, tk=tk),
        name="flash",
        grid=(batch, ATT_KV, nq),
        in_specs=[pl.BlockSpec((tq, rep_w), lambda b, g, i: (b * nq + i, g)),
                  pl.BlockSpec((seq_len, ATT_HD), lambda b, g, i: (b, g)),
                  pl.BlockSpec((seq_len, ATT_HD), lambda b, g, i: (b, g))],
        out_specs=pl.BlockSpec((tq, rep_w), lambda b, g, i: (b * nq + i, g)),
        out_shape=jax.ShapeDtypeStruct((t, ATT_W), BF16),
        compiler_params=_params(("arbitrary",) * 3),
    )(qh, kh, v)


def _outproj_kernel(sf_ref, sb_ref, xs_ref, zs_ref, dsk_ref, nws_ref, of_ref, ob_ref, zd_ref, ya_ref, x_ref, mod_ref,
                    nwd_ref, nwp_ref, nwf_ref, wo_ref, wr_ref, x1_ref, h2_ref, aff_ref):
    y = sf_ref[...] + sb_ref[...] + dsk_ref[...] * xs_ref[...].astype(F32)
    y = y * _silu(zs_ref[...].astype(F32))
    half = SSD_W // SSD_G
    nws = nws_ref[...]
    ys = jnp.concatenate([_rms(y[:, g * half:(g + 1) * half]) * nws[:, g * half:(g + 1) * half]
                          for g in range(SSD_G)], axis=1).astype(BF16)
    yd = []
    for h in range(DN_H):
        cols = slice(h * DN_HD, (h + 1) * DN_HD)
        o = _rms(of_ref[:, cols] + ob_ref[:, cols]) * nwd_ref[...]
        yd.append((o * _silu(zd_ref[:, cols].astype(F32))).astype(BF16))
    yd = jnp.concatenate(yd, axis=1)
    m = (_dot(ys, wo_ref[0:SSD_W, :]) + _dot(yd, wo_ref[SSD_W:SSD_W + DN_W, :])
         + _dot(ya_ref[...], wo_ref[SSD_W + DN_W:, :]))
    mod = mod_ref[0]
    x1 = x_ref[...] + mod[2:3, :] * (_rms(m) * nwp_ref[...])
    x1_ref[...] = x1
    h2 = _rms(x1) * nwf_ref[...] * (1.0 + mod[4:5, :]) + mod[3:4, :]
    h2_hi = h2.astype(BF16)
    h2_ref[...] = h2_hi
    h2_lo = (h2 - h2_hi.astype(F32)).astype(BF16)
    logits = _dot(h2_hi, wr_ref[0]) + _dot(h2_lo, wr_ref[0]) + _dot(h2_hi, wr_ref[1])
    lane = lax.broadcasted_iota(I32, logits.shape, 1)
    logits = jnp.where(lane < N_EXPERTS, logits, -jnp.inf)
    ex = jnp.exp(logits - jnp.max(logits, axis=-1, keepdims=True))
    aff = ex / jnp.sum(ex, axis=-1, keepdims=True)
    aff_ref[...] = aff.T[0:N_EXPERTS, :]


def _outproj(sf, sb, xact, zs, dsk, nws, of, ob, zd, ya, x, mod, nwd, nwp, nwf, wo_bf, wr_p, seq_len):
    t = x.shape[0]
    tm = min(512, seq_len)
    per_seq = seq_len // tm
    row = lambda i: (i, 0)
    const = lambda i: (0, 0)
    return pl.pallas_call(
        _outproj_kernel,
        name="outproj",
        grid=(t // tm,),
        in_specs=[pl.BlockSpec((tm, SSD_W), row), pl.BlockSpec((tm, SSD_W), row),
                  pl.BlockSpec((tm, SSD_W), row),
                  pl.BlockSpec((tm, SSD_W), row), pl.BlockSpec((1, SSD_W), const), pl.BlockSpec((1, SSD_W), const),
                  pl.BlockSpec((tm, DN_W), row), pl.BlockSpec((tm, DN_W), row),
                  pl.BlockSpec((tm, DN_W), row), pl.BlockSpec((tm, ATT_W), row),
                  pl.BlockSpec((tm, D_MODEL), row),
                  pl.BlockSpec((1, 6, D_MODEL), lambda i: (i // per_seq, 0, 0)),
                  pl.BlockSpec((1, DN_HD), const),
                  pl.BlockSpec((1, D_MODEL), const), pl.BlockSpec((1, D_MODEL), const),
                  pl.BlockSpec((2 * D_MODEL, D_MODEL), const),
                  pl.BlockSpec((2, D_MODEL, LANES), lambda i: (0, 0, 0))],
        out_specs=[pl.BlockSpec((tm, D_MODEL), row), pl.BlockSpec((tm, D_MODEL), row),
                   pl.BlockSpec((N_EXPERTS, tm), lambda i: (0, i))],
        out_shape=[jax.ShapeDtypeStruct((t, D_MODEL), F32), jax.ShapeDtypeStruct((t, D_MODEL), BF16),
                   jax.ShapeDtypeStruct((N_EXPERTS, t), F32)],
        compiler_params=_params(("arbitrary",)),
    )(sf, sb, xact, zs, dsk, nws, of, ob, zd, ya, x, mod, nwd, nwp.reshape(1, D_MODEL), nwf.reshape(1, D_MODEL),
      wo_bf, wr_p)


def _route_kernel(aff_ref, pos_ref, offs_ref, *, cap):
    ne, t = aff_ref.shape
    nb = t // LANES
    capf = float(cap)

    def bits_of(x):
        return pltpu.bitcast(x, I32)

    def bis(i, thr):
        cand = thr | jnp.left_shift(jnp.int32(1), 30 - i)
        cnt = jnp.sum(jnp.where(bits_of(aff_ref[...]) >= cand, 1.0, 0.0), axis=1, keepdims=True)
        return jnp.where(cnt >= capf, cand, thr)

    thr = lax.fori_loop(0, 31, bis, jnp.zeros((ne, 1), I32))
    n_gt = jnp.sum(jnp.where(bits_of(aff_ref[...]) > thr, 1.0, 0.0), axis=1, keepdims=True)
    need_eq = capf - n_gt

    r = lax.broadcasted_iota(I32, (LANES, LANES), 0)
    s = lax.broadcasted_iota(I32, (LANES, LANES), 1)
    triu = jnp.where(r <= s, 1.0, 0.0).astype(BF16)
    lane_nb = lax.broadcasted_iota(I32, (ne, nb), 1)

    offs_ref[...] = jnp.zeros_like(offs_ref)

    def tile(i, carry):
        run_sel, run_eq = carry
        start = pl.multiple_of(i * LANES, LANES)
        b = bits_of(aff_ref[:, pl.ds(start, LANES)])
        gt = b > thr
        eq = jnp.where(b == thr, 1.0, 0.0)
        eq_rank = _dot(eq.astype(BF16), triu) - eq + run_eq
        sel = jnp.where(gt | ((eq > 0.0) & (eq_rank < need_eq)), 1.0, 0.0)
        pos = _dot(sel.astype(BF16), triu) - sel + run_sel
        pos_ref[:, pl.ds(start, LANES)] = jnp.where(sel > 0.0, pos, -1.0).astype(I32)
        offs_ref[...] = jnp.where(lane_nb == i, run_sel.astype(I32), offs_ref[...])
        return (run_sel + jnp.sum(sel, axis=1, keepdims=True),
                run_eq + jnp.sum(eq, axis=1, keepdims=True))

    zero = jnp.zeros((ne, 1), F32)
    lax.fori_loop(0, nb, tile, (zero, zero))


def _route(aff_t, cap):
    ne, t = aff_t.shape
    nb = t // LANES
    return pl.pallas_call(
        functools.partial(_route_kernel, cap=cap),
        name="route",
        out_shape=[jax.ShapeDtypeStruct((ne, t), I32), jax.ShapeDtypeStruct((ne, nb), I32)],
        compiler_params=pltpu.CompilerParams(vmem_limit_bytes=VMEM_LIMIT),
    )(aff_t)


MOE_SUB = 128
MOE_ALIGN = 16
MOE_WIN_SMALL = 32 + MOE_ALIGN
MOE_WIN_FULL = MOE_SUB + MOE_ALIGN
MOE_TILE = 288
MOE_HALF = MOE_TILE // 2
MOE_FC = 512
MOE_BLOCK = 2048


def _moe_kernel(offs_ref, h_ref, pos_ref, gate_ref, wg_ref, wu_ref, wd_ref, o_ref,
                xe_ref, ye_ref, gb_ref, *, cap, nsub):
    sb = pl.program_id(0)
    e = pl.program_id(1)
    nsb = pl.num_programs(0)
    nblk = nsb * nsub
    start = offs_ref[e, sb * nsub]
    end = jnp.where(sb == nsb - 1, cap, offs_ref[e, jnp.minimum((sb + 1) * nsub, nblk - 1)])
    count = end - start
    nfull = count // MOE_TILE

    @pl.when((sb == 0) & (e == 0))
    def _():
        xe_ref[...] = jnp.zeros_like(xe_ref)
        ye_ref[...] = jnp.zeros_like(ye_ref)
        gb_ref[...] = jnp.zeros_like(gb_ref)

    @pl.when(e == 0)
    def _():
        o_ref[...] = jnp.zeros_like(o_ref)

    def window(j):
        lo = offs_ref[e, sb * nsub + j] - start
        hi = (offs_ref[e, sb * nsub + j + 1] - start) if j + 1 < nsub else count
        a = pl.multiple_of((lo // MOE_ALIGN) * MOE_ALIGN, MOE_ALIGN)
        return lo, hi, a

    wins = [window(j) for j in range(nsub)]
    all_small = functools.reduce(jnp.logical_and, [hi - a <= MOE_WIN_SMALL for (_, hi, a) in wins])

    def onehots(win):
        rid = lax.broadcasted_iota(I32, (win, MOE_SUB), 0)
        out = []
        for j, (_, _, a) in enumerate(wins):
            posrow = pos_ref[0, :, j * MOE_SUB:(j + 1) * MOE_SUB]
            out.append((posrow - (start + a) == rid) & (posrow >= 0))
        return out, jnp.stack([jnp.where(oh, 1.0, 0.0).astype(BF16) for oh in out])

    def for_window_size(body):
        @pl.when(all_small)
        def _():
            body(MOE_WIN_SMALL)

        @pl.when(jnp.logical_not(all_small))
        def _():
            body(MOE_WIN_FULL)

    def gather(win):
        ohs, oh_b = onehots(win)
        comp = _bdot(oh_b, h_ref[...].reshape(nsub, MOE_SUB, D_MODEL))
        rid = lax.broadcasted_iota(I32, (MOE_ALIGN, 1), 0)
        for j, (lo, _, a) in enumerate(wins):
            gate = jnp.sum(jnp.where(ohs[j], gate_ref[0, :, j * MOE_SUB:(j + 1) * MOE_SUB], 0.0),
                           axis=1, keepdims=True)
            gate = jnp.broadcast_to(gate, (win, LANES))
            keep = rid < (lo - a)
            a1 = pl.multiple_of(a + MOE_ALIGN, MOE_ALIGN)
            xe_ref[pl.ds(a, MOE_ALIGN), :] = jnp.where(keep, xe_ref[pl.ds(a, MOE_ALIGN), :].astype(F32),
                                                       comp[j, :MOE_ALIGN]).astype(BF16)
            xe_ref[pl.ds(a1, win - MOE_ALIGN), :] = comp[j, MOE_ALIGN:].astype(BF16)
            gb_ref[pl.ds(a, MOE_ALIGN), :] = jnp.where(keep, gb_ref[pl.ds(a, MOE_ALIGN), :], gate[:MOE_ALIGN])
            gb_ref[pl.ds(a1, win - MOE_ALIGN), :] = gate[MOE_ALIGN:]

    def scatter(win):
        _, oh_b = onehots(win)
        yw = jnp.stack([(ye_ref[pl.ds(a, win), :] * gb_ref[pl.ds(a, win), 0:1]).astype(BF16) for (_, _, a) in wins])
        o_ref[...] += _bdot_tn(oh_b, yw).reshape(nsub * MOE_SUB, D_MODEL)

    for_window_size(gather)

    nfc = wg_ref.shape[3] // MOE_FC

    def ffn_rows(r0, rows):
        x = xe_ref[pl.ds(r0, rows), :]
        y = None
        for fc in range(nfc):
            cols = slice(fc * MOE_FC, (fc + 1) * MOE_FC)
            hid = (_silu(_dot(x, wg_ref[0, 0, :, cols])) * _dot(x, wu_ref[0, 0, :, cols])).astype(BF16)
            part = _dot(hid, wd_ref[0, 0, cols, :])
            y = part if y is None else y + part
        ye_ref[pl.ds(r0, rows), :] = y

    def ffn_tile(i, carry):
        ffn_rows(pl.multiple_of(i * MOE_TILE, MOE_ALIGN), MOE_TILE)
        return carry

    lax.fori_loop(0, nfull, ffn_tile, 0)
    rem = count - nfull * MOE_TILE
    rem0 = pl.multiple_of(nfull * MOE_TILE, MOE_ALIGN)

    @pl.when(rem > MOE_HALF)
    def _():
        ffn_rows(rem0, MOE_TILE)

    @pl.when((rem > 0) & (rem <= MOE_HALF))
    def _():
        ffn_rows(rem0, MOE_HALF)

    for_window_size(scatter)


def _moe(offs, h2, pos, aff_t, wg, wu, wd, layer, cap, sblk):
    t = h2.shape[0]
    nsb = t // sblk
    nsub = sblk // MOE_SUB
    rows = sblk + MOE_TILE + MOE_WIN_FULL
    once = pl.Buffered(1)
    grid_spec = pltpu.PrefetchScalarGridSpec(
        num_scalar_prefetch=1,
        grid=(nsb, N_EXPERTS),
        in_specs=[pl.BlockSpec((sblk, D_MODEL), lambda s, e, o: (s, 0), pipeline_mode=once),
                  pl.BlockSpec((1, 1, sblk), lambda s, e, o: (e, 0, s)),
                  pl.BlockSpec((1, 1, sblk), lambda s, e, o: (e, 0, s)),
                  pl.BlockSpec((1, 1, D_MODEL, EXPERT_FF), lambda s, e, o: (layer, e, 0, 0)),
                  pl.BlockSpec((1, 1, D_MODEL, EXPERT_FF), lambda s, e, o: (layer, e, 0, 0)),
                  pl.BlockSpec((1, 1, EXPERT_FF, D_MODEL), lambda s, e, o: (layer, e, 0, 0))],
        out_specs=pl.BlockSpec((sblk, D_MODEL), lambda s, e, o: (s, 0), pipeline_mode=once),
        scratch_shapes=[pltpu.VMEM((rows, D_MODEL), BF16), pltpu.VMEM((rows, D_MODEL), F32),
                        pltpu.VMEM((rows, LANES), F32)],
    )
    return pl.pallas_call(
        functools.partial(_moe_kernel, cap=cap, nsub=nsub),
        name="moe",
        grid_spec=grid_spec,
        out_shape=jax.ShapeDtypeStruct((t, D_MODEL), F32),
        compiler_params=pltpu.CompilerParams(dimension_semantics=("arbitrary",) * 2,
                                             vmem_limit_bytes=MOE_VMEM_LIMIT),
    )(offs, h2, pos.reshape(N_EXPERTS, 1, t), aff_t.reshape(N_EXPERTS, 1, t), wg, wu, wd)


def _post_kernel(x_ref, f_ref, mod_ref, nw_ref, o_ref):
    mod = mod_ref[0]
    o_ref[...] = x_ref[...] + mod[5:6, :] * (_rms(f_ref[...]) * nw_ref[...])


def _post(x1, f, mod, nw, seq_len):
    t = x1.shape[0]
    tm = min(512, seq_len)
    per_seq = seq_len // tm
    row = lambda i: (i, 0)
    return pl.pallas_call(
        _post_kernel,
        name="post",
        grid=(t // tm,),
        in_specs=[pl.BlockSpec((tm, D_MODEL), row), pl.BlockSpec((tm, D_MODEL), row),
                  pl.BlockSpec((1, 6, D_MODEL), lambda i: (i // per_seq, 0, 0)),
                  pl.BlockSpec((1, D_MODEL), lambda i: (0, 0))],
        out_specs=pl.BlockSpec((tm, D_MODEL), row),
        out_shape=jax.ShapeDtypeStruct((t, D_MODEL), F32),
        compiler_params=_params(("arbitrary",)),
    )(x1, f, mod, nw.reshape(1, D_MODEL))


def _pad_lanes(v, offset=0):
    out = jnp.zeros((1, LANES), F32)
    return out.at[0, offset:offset + v.shape[0]].set(v.astype(F32))


def _split_bf16(w):
    hi = w.astype(BF16)
    return jnp.stack([hi, (w - hi.astype(F32)).astype(BF16)])


def _prep_layer(p, l):
    w_in = p["w_in"][l]
    cols, off = [], 0
    for (_, width, stored, _) in _IN_SEGS:
        seg = w_in[:, off:off + width]
        if stored != width:
            seg = jnp.pad(seg, ((0, 0), (0, stored - width)))
        cols.append(seg)
        off += width
    q = {
        "w_ada": p["w_ada"][l].astype(BF16), "b_ada": p["b_ada"][l],
        "w_in": jnp.concatenate(cols, axis=1).astype(BF16),
        "w_out": p["w_out"][l].astype(BF16),
        "w_router": _split_bf16(jnp.pad(p["w_router"][l], ((0, 0), (0, LANES - N_EXPERTS)))),
        "ssd_bias": jnp.concatenate([_pad_lanes(p["ssd_dt_bias"][l, d]) for d in range(2)], axis=0),
        "ssd_alog": jnp.concatenate([_pad_lanes(p["ssd_a_log"][l, d]) for d in range(2)], axis=0),
        "ssd_d": jnp.repeat(p["ssd_d"][l], SSD_P).reshape(1, SSD_W),
        "ssd_norm": p["ssd_norm"][l].reshape(1, SSD_W),
        "dn_bias": jnp.concatenate([_pad_lanes(p["dn_dt_bias"][l, d], DN_H) for d in range(2)], axis=0),
        "dn_alog": jnp.concatenate([_pad_lanes(p["dn_a_log"][l, d], DN_H) for d in range(2)], axis=0),
        "dn_norm": p["dn_norm"][l].reshape(1, DN_HD),
    }
    for name in ("norm_mix_pre", "norm_mix_post", "norm_ffn_pre", "norm_ffn_post", "conv_ssd_w", "conv_ssd_b",
                 "conv_dn_w", "q_norm", "k_norm"):
        q[name] = p[name][l]
    return q


def _trunk(x, c, layers, experts, sblk):
    batch, seq_len, _ = x.shape
    t = batch * seq_len
    cap = EC_CAPACITY * t // N_EXPERTS
    cos_t, sin_t = _rope_tables(seq_len)
    x = x.reshape(t, D_MODEL)
    for l, q in enumerate(layers):
        mod = _ada(c, q["w_ada"], q["b_ada"])
        z_ssd, xbc, dtp, qkv_dn, z_dn, ba_dn, qa, ka, va = _inproj(x, mod, q["norm_mix_pre"], q["w_in"], seq_len)
        xact = _conv(xbc, q["conv_ssd_w"], q["conv_ssd_b"], seq_len)
        qkvact = _conv(qkv_dn, q["conv_dn_w"], jnp.zeros((DN_CONV_DIM,), F32), seq_len)
        s_f, s_b = _ssd(xact, dtp, q["ssd_bias"], q["ssd_alog"], batch, seq_len)
        o_f, o_b = _dn(qkvact, ba_dn, q["dn_bias"], q["dn_alog"], batch, seq_len)
        qh, kh = _qkprep(qa, ka, cos_t, sin_t, q["q_norm"], q["k_norm"], seq_len)
        y_att = _flash(qh, kh, va, batch, seq_len)
        x1, h2, aff_t = _outproj(s_f, s_b, xact, z_ssd, q["ssd_d"], q["ssd_norm"], o_f, o_b, z_dn, y_att, x, mod,
                                 q["dn_norm"], q["norm_mix_post"], q["norm_ffn_pre"], q["w_out"], q["w_router"],
                                 seq_len)
        pos, offs = _route(aff_t, cap)
        f = _moe(offs, h2, pos, aff_t, *experts, l, cap, sblk)
        x = _post(x1, f, mod, q["norm_ffn_post"], seq_len)
    return x.reshape(batch, seq_len, D_MODEL)


def kernel(x_prompt, x_sample, c_prompt, c_sample, w_ada, b_ada, norm_mix_pre, norm_mix_post, w_in, conv_ssd_w, conv_ssd_b, ssd_dt_bias, ssd_a_log, ssd_d, ssd_norm, conv_dn_w, dn_dt_bias, dn_a_log, dn_norm, q_norm, k_norm, w_out, norm_ffn_pre, norm_ffn_post, w_router, w_gate, w_up, w_down):
    p = dict(w_ada=w_ada, b_ada=b_ada, norm_mix_pre=norm_mix_pre, norm_mix_post=norm_mix_post,
             w_in=w_in, conv_ssd_w=conv_ssd_w, conv_ssd_b=conv_ssd_b, ssd_dt_bias=ssd_dt_bias,
             ssd_a_log=ssd_a_log, ssd_d=ssd_d, ssd_norm=ssd_norm, conv_dn_w=conv_dn_w,
             dn_dt_bias=dn_dt_bias, dn_a_log=dn_a_log, dn_norm=dn_norm, q_norm=q_norm, k_norm=k_norm,
             w_out=w_out, norm_ffn_pre=norm_ffn_pre, norm_ffn_post=norm_ffn_post,
             w_router=w_router, w_gate=w_gate, w_up=w_up, w_down=w_down)
    layers = [_prep_layer(p, l) for l in range(w_in.shape[0])]
    experts = (w_gate.astype(BF16), w_up.astype(BF16), w_down.astype(BF16))
    y_prompt = _trunk(x_prompt, c_prompt, layers, experts, MOE_BLOCK)
    y_sample = _trunk(x_sample, c_sample, layers, experts, MOE_BLOCK)
    return (y_prompt, y_sample)
```

```python
import functools

import jax
import jax.numpy as jnp
import numpy as np
from jax import lax
from jax.experimental import pallas as pl
from jax.experimental.pallas import tpu as pltpu

F32 = jnp.float32
BF16 = jnp.bfloat16
I32 = jnp.int32

D_MODEL = 1024
SSD_W = 1024
SSD_P = 64
SSD_H = 16
SSD_G = 2
SSD_N = 128
SSD_CONV_DIM = SSD_W + 2 * SSD_G * SSD_N
DN_W = 512
DN_HD = 128
DN_H = 4
DN_CONV_DIM = 3 * DN_W
ATT_W = 512
ATT_HD = 128
ATT_H = 4
ATT_KV = 2
ATT_KV_W = ATT_KV * ATT_HD
GRID_W = 64
ROPE_THETA = 10000.0
N_EXPERTS = 16
EC_CAPACITY = 2
EXPERT_FF = 2048
EPS = 1e-6
CONV_K = 5

LANES = 128
SUBLANES = 8
VMEM_LIMIT = 56 * 1024 * 1024
MOE_VMEM_LIMIT = 60 * 1024 * 1024

_IN_SEGS = (
    ("z_ssd", SSD_W, SSD_W, BF16),
    ("xbc", SSD_CONV_DIM, SSD_CONV_DIM, BF16),
    ("dt", SSD_H, LANES, F32),
    ("qkv_dn", DN_CONV_DIM, DN_CONV_DIM, BF16),
    ("z_dn", DN_W, DN_W, BF16),
    ("ba_dn", 2 * DN_H, LANES, F32),
    ("q", ATT_W, ATT_W, BF16),
    ("k", ATT_KV_W, ATT_KV_W, BF16),
    ("v", ATT_KV_W, ATT_KV_W, BF16),
)


def _params(sem):
    return pltpu.CompilerParams(dimension_semantics=sem, vmem_limit_bytes=VMEM_LIMIT)


def _sigmoid(x):
    return 1.0 / (1.0 + jnp.exp(-x))


def _silu(x):
    return x * _sigmoid(x)


def _softplus(x):
    return jnp.maximum(x, 0.0) + jnp.log(1.0 + jnp.exp(-jnp.abs(x)))


def _rms(x):
    return x * lax.rsqrt(jnp.mean(x * x, axis=-1, keepdims=True) + EPS)


def _dot(a, b):
    return jnp.dot(a, b, preferred_element_type=F32)


def _dot_nt(a, b):
    return lax.dot_general(a, b, (((1,), (1,)), ((), ())), preferred_element_type=F32)


def _dot_tn(a, b):
    return lax.dot_general(a, b, (((0,), (0,)), ((), ())), preferred_element_type=F32)


def _cumsum_mm(a, b, split_lhs=False):
    x = a if split_lhs else b
    hi = x.astype(BF16)
    r1 = x - hi.astype(F32)
    mid = r1.astype(BF16)
    lo = (r1 - mid.astype(F32)).astype(BF16)
    if split_lhs:
        return _dot(hi, b) + _dot(mid, b) + _dot(lo, b)
    return _dot(a, hi) + _dot(a, mid) + _dot(a, lo)


def _ada_kernel(c_ref, w_ref, b_ref, o_ref):
    o_ref[...] = _dot(_silu(c_ref[...]).astype(BF16), w_ref[...]) + b_ref[...]


def _ada(c, w_bf, b):
    nb = c.shape[0]
    rows = -(-nb // SUBLANES) * SUBLANES
    cp = jnp.zeros((rows, D_MODEL), F32).at[:nb].set(c)
    n = w_bf.shape[1]
    tn = 1024
    out = pl.pallas_call(
        _ada_kernel,
        name="ada",
        grid=(n // tn,),
        in_specs=[pl.BlockSpec((rows, D_MODEL), lambda j: (0, 0)),
                  pl.BlockSpec((D_MODEL, tn), lambda j: (0, j)),
                  pl.BlockSpec((1, tn), lambda j: (0, j))],
        out_specs=pl.BlockSpec((rows, tn), lambda j: (0, j)),
        out_shape=jax.ShapeDtypeStruct((rows, n), F32),
        compiler_params=_params(("arbitrary",)),
    )(cp, w_bf, b.reshape(1, n))
    return out[:nb].reshape(nb, 6, D_MODEL)


_HALO = SUBLANES
_CONV_SEGS = ("xbc", "qkv_dn")


def _inproj_kernel(x_ref, xp_ref, xn_ref, mod_ref, nw_ref, w_ref, cw_ref, cb_ref, *rest, per_seq):
    out_refs, bufs = rest[:len(_IN_SEGS)], rest[len(_IN_SEGS):]
    i = pl.program_id(0)
    tm = x_ref.shape[0]
    mod = mod_ref[0]

    def norm_mod(x):
        return _rms(x) * nw_ref[...] * (1.0 + mod[1:2, :]) + mod[0:1, :]

    keep_prev = jnp.where((i % per_seq) == 0, 0.0, 1.0)
    keep_next = jnp.where((i % per_seq) == per_seq - 1, 0.0, 1.0)
    hrow = lax.broadcasted_iota(I32, (2 * _HALO, 1), 0)
    halo = norm_mod(jnp.concatenate([xp_ref[...], xn_ref[...]], axis=0)) * jnp.where(hrow < _HALO, keep_prev, keep_next)
    hb = norm_mod(x_ref[...]).astype(BF16)
    hcat = jnp.concatenate([hb, halo.astype(BF16)], axis=0)
    off = 0
    ci = 0
    for o_ref, (name, _, width, _) in zip(out_refs, _IN_SEGS):
        wseg = w_ref[:, off:off + width]
        off += width
        if name not in _CONV_SEGS:
            o_ref[...] = _dot(hb, wseg).astype(o_ref.dtype)
            continue
        buf = bufs[ci]
        full = _dot(hcat, wseg)
        buf[_HALO:_HALO + tm, :] = full[0:tm]
        buf[0:_HALO, :] = full[tm:tm + _HALO]
        buf[_HALO + tm:2 * _HALO + tm, :] = full[tm + _HALO:tm + 2 * _HALO]
        acc = jnp.zeros((tm, width), F32) + cb_ref[ci:ci + 1, :]
        for k in range(CONV_K):
            lo = _HALO - CONV_K // 2 + k
            acc = acc + buf[lo:lo + tm, :] * cw_ref[ci, k:k + 1, :]
        o_ref[...] = _silu(acc).astype(o_ref.dtype)
        ci += 1


def _inproj(x, mod, nw, w_p, conv_w, conv_b, seq_len):
    t = x.shape[0]
    tm = min(512, seq_len)
    per_seq = seq_len // tm
    ntot = w_p.shape[1]
    hb = tm // _HALO
    nh = t // _HALO
    cw = conv_w.shape[2]
    out_shapes = [jax.ShapeDtypeStruct((t, sw), dt) for (_, _, sw, dt) in _IN_SEGS]
    out_specs = [pl.BlockSpec((tm, sw), lambda i: (i, 0)) for (_, _, sw, _) in _IN_SEGS]
    return pl.pallas_call(
        functools.partial(_inproj_kernel, per_seq=per_seq),
        name="inproj",
        grid=(t // tm,),
        in_specs=[pl.BlockSpec((tm, D_MODEL), lambda i: (i, 0)),
                  pl.BlockSpec((_HALO, D_MODEL), lambda i: (jnp.maximum(i * hb - 1, 0), 0)),
                  pl.BlockSpec((_HALO, D_MODEL), lambda i: (jnp.minimum((i + 1) * hb, nh - 1), 0)),
                  pl.BlockSpec((1, 6, D_MODEL), lambda i: (i // per_seq, 0, 0)),
                  pl.BlockSpec((1, D_MODEL), lambda i: (0, 0)),
                  pl.BlockSpec((D_MODEL, ntot), lambda i: (0, 0)),
                  pl.BlockSpec((len(_CONV_SEGS), SUBLANES, cw), lambda i: (0, 0, 0)),
                  pl.BlockSpec((len(_CONV_SEGS), cw), lambda i: (0, 0))],
        out_specs=out_specs,
        out_shape=out_shapes,
        scratch_shapes=[pltpu.VMEM((tm + 2 * _HALO, cw), F32) for _ in _CONV_SEGS],
        compiler_params=_params(("arbitrary",)),
    )(x, x, x, mod, nw.reshape(1, D_MODEL), w_p, conv_w, conv_b)


SSD_Q = 128


def _ssd_kernel(xf_ref, dtf_ref, xb_ref, dtb_ref, bias_ref, alog_ref, sel_ref, yf_ref, yb_ref, h_ref):
    q = SSD_Q
    dirs = (0, 1)

    @pl.when(pl.program_id(1) == 0)
    def _():
        h_ref[...] = jnp.zeros_like(h_ref)

    r = lax.broadcasted_iota(I32, (q, q), 0)
    s = lax.broadcasted_iota(I32, (q, q), 1)
    mask = (r >= s, s >= r)
    tri = [jnp.where(m, 1.0, 0.0).astype(BF16) for m in mask]
    lo = s < SSD_P
    last = (q - 1, 0)

    dtp = (dtf_ref[...], dtb_ref[...])
    dt = [_softplus(dtp[d] + bias_ref[d:d + 1, :]) for d in dirs]
    dta = [dt[d] * (-jnp.exp(alog_ref[d:d + 1, :])) for d in dirs]
    acs = [_cumsum_mm(tri[d], dta[d]) for d in dirs]
    acs_t = [acs[d].T for d in dirs]
    dt_t = [dt[d].T for d in dirs]
    eacs = [jnp.exp(acs[d]) for d in dirs]
    wcols = [jnp.exp(acs[d][last[d]:last[d] + 1, :] - acs[d]) * dt[d] for d in dirs]
    cols = jnp.concatenate([eacs[0], wcols[0], eacs[1], wcols[1]], axis=0).astype(BF16)
    spread = _dot(cols, sel_ref[...])
    eacs_w = [spread[2 * d * q:(2 * d + 1) * q] for d in dirs]
    wcols_w = [spread[(2 * d + 1) * q:(2 * d + 2) * q] for d in dirs]
    dec_w = _cumsum_mm(jnp.concatenate([eacs[d][last[d]:last[d] + 1, :] for d in dirs], axis=0), sel_ref[...],
                       split_lhs=True)

    xact = (xf_ref[...], xb_ref[...])
    ys = ([], [])
    for g in range(SSD_G):
        bm = [xact[d][:, SSD_W + g * SSD_N:SSD_W + (g + 1) * SSD_N] for d in dirs]
        cm = [xact[d][:, SSD_W + (SSD_G + g) * SSD_N:SSD_W + (SSD_G + g + 1) * SSD_N] for d in dirs]
        cb = [_dot_nt(cm[d], bm[d]) for d in dirs]
        hg = [h_ref[d, g] for d in dirs]
        cmh = [_dot(cm[d], hg[d].astype(BF16)) for d in dirs]
        xw_parts = ([], [])
        for pp in range(4):
            j0 = g * 8 + pp * 2
            lanes = slice((g * 4 + pp) * LANES, (g * 4 + pp + 1) * LANES)
            xpair = [xact[d][:, lanes].astype(F32) for d in dirs]
            ms = ([], [])
            for j in (j0, j0 + 1):
                seg = [acs[d][:, j:j + 1] - acs_t[d][j:j + 1, :] for d in dirs]
                lm = [jnp.exp(jnp.where(mask[d], seg[d], -jnp.inf)) for d in dirs]
                for d in dirs:
                    ms[d].append((cb[d] * lm[d] * dt_t[d][j:j + 1, :]).astype(BF16))
            mcat = [jnp.concatenate(ms[d], axis=1) for d in dirs]
            x2 = [jnp.concatenate([jnp.where(lo, xpair[d], 0.0), jnp.where(lo, 0.0, xpair[d])], axis=0).astype(BF16)
                  for d in dirs]
            yd = [_dot(mcat[d], x2[d]) for d in dirs]
            for d in dirs:
                ys[d].append(yd[d] + cmh[d][:, pp * LANES:(pp + 1) * LANES] * eacs_w[d][:, lanes])
                xw_parts[d].append((xpair[d] * wcols_w[d][:, lanes]).astype(BF16))
        half = slice(g * (SSD_W // SSD_G), (g + 1) * (SSD_W // SSD_G))
        upd = [_dot_tn(bm[d], jnp.concatenate(xw_parts[d], axis=1)) for d in dirs]
        for d in dirs:
            h_ref[d, g] = hg[d] * dec_w[d:d + 1, half] + upd[d]
    yf_ref[...] = jnp.concatenate(ys[0], axis=1)
    yb_ref[...] = jnp.concatenate(ys[1], axis=1)


def _ssd(xact, dtp, bias, alog, batch, seq_len):
    t = xact.shape[0]
    nc = seq_len // SSD_Q
    fwd = lambda b, c: (b * nc + c, 0)
    bwd = lambda b, c: (b * nc + nc - 1 - c, 0)
    const = lambda b, c: (0, 0)
    sel = jnp.repeat(jnp.eye(LANES, SSD_H, dtype=BF16), SSD_P, axis=1)
    return pl.pallas_call(
        _ssd_kernel,
        name="ssd",
        grid=(batch, nc),
        in_specs=[pl.BlockSpec((SSD_Q, SSD_CONV_DIM), fwd), pl.BlockSpec((SSD_Q, LANES), fwd),
                  pl.BlockSpec((SSD_Q, SSD_CONV_DIM), bwd), pl.BlockSpec((SSD_Q, LANES), bwd),
                  pl.BlockSpec((2, LANES), const), pl.BlockSpec((2, LANES), const),
                  pl.BlockSpec((LANES, SSD_W), const)],
        out_specs=[pl.BlockSpec((SSD_Q, SSD_W), fwd), pl.BlockSpec((SSD_Q, SSD_W), bwd)],
        out_shape=[jax.ShapeDtypeStruct((t, SSD_W), F32)] * 2,
        scratch_shapes=[pltpu.VMEM((2, SSD_G, SSD_N, SSD_W // SSD_G), F32)],
        compiler_params=_params(("arbitrary", "arbitrary")),
    )(xact, dtp, xact, dtp, bias, alog, sel)


DN_BLK = 128
DN_C = 64


def _dn_masks(reverse):
    n = DN_BLK
    r = lax.broadcasted_iota(I32, (n, n), 0)
    s = lax.broadcasted_iota(I32, (n, n), 1)
    same = (r >= DN_C) == (s >= DN_C)
    incl = same & ((s >= r) if reverse else (r >= s))
    strict = same & ((s > r) if reverse else (r > s))
    return r, incl, strict, jnp.where(incl, 1.0, 0.0).astype(BF16), jnp.where(r == s, 1.0, 0.0)


def _bdot(a, b):
    return lax.dot_general(a, b, (((2,), (1,)), ((0,), (0,))), preferred_element_type=F32)


def _bdot_nt(a, b):
    return lax.dot_general(a, b, (((2,), (2,)), ((0,), (0,))), preferred_element_type=F32)


def _bdot_tn(a, b):
    return lax.dot_general(a, b, (((1,), (1,)), ((0,), (0,))), preferred_element_type=F32)


def _dn_group(qkvs, bas, dtb, alog, states, reverse):
    n = DN_BLK
    nseq = len(qkvs)
    r, incl, strict, tri, eye = _dn_masks(reverse)
    ba = jnp.concatenate(bas, axis=1)
    beta = _sigmoid(ba)
    gl = -jnp.exp(jnp.concatenate([alog] * nseq, axis=1)) * _softplus(ba + jnp.concatenate([dtb] * nseq, axis=1))
    gcs = _cumsum_mm(tri, gl)
    if reverse:
        t0, t1 = gcs[0:1, :], gcs[DN_C:DN_C + 1, :]
    else:
        t0, t1 = gcs[DN_C - 1:DN_C, :], gcs[n - 1:n, :]
    eg = jnp.exp(gcs)
    ekd = jnp.exp(jnp.where(r[:, 0:1] < DN_C, t0, t1) - gcs)
    dec0 = jnp.exp(t0)
    dec1 = jnp.exp(t1)

    qhb, qg, khb, kd, kbs, kes, vbs, decays, d0s, d1s = [], [], [], [], [], [], [], [], [], []
    for b in range(nseq):
        gcs_t = gcs[:, b * LANES:(b + 1) * LANES].T
        for h in range(DN_H):
            lb = b * LANES + h
            la = lb + DN_H
            qh = qkvs[b][:, h * DN_HD:(h + 1) * DN_HD].astype(F32)
            kh = qkvs[b][:, DN_W + h * DN_HD:DN_W + (h + 1) * DN_HD].astype(F32)
            vh = qkvs[b][:, 2 * DN_W + h * DN_HD:2 * DN_W + (h + 1) * DN_HD].astype(F32)
            qh = qh * lax.rsqrt(jnp.sum(qh * qh, axis=-1, keepdims=True) + EPS) * (DN_HD ** -0.5)
            kh = kh * lax.rsqrt(jnp.sum(kh * kh, axis=-1, keepdims=True) + EPS)
            bcol = beta[:, lb:lb + 1]
            decays.append(jnp.exp(jnp.where(incl, gcs[:, la:la + 1] - gcs_t[DN_H + h:DN_H + h + 1, :], -jnp.inf)))
            kb = kh * bcol
            qhb.append(qh.astype(BF16))
            qg.append((qh * eg[:, la:la + 1]).astype(BF16))
            khb.append(kh.astype(BF16))
            kd.append((kh * ekd[:, la:la + 1]).astype(BF16))
            kbs.append(kb.astype(BF16))
            kes.append((kb * eg[:, la:la + 1]).astype(BF16))
            vbs.append((vh * bcol).astype(BF16))
            d0s.append(dec0[:, la:la + 1])
            d1s.append(dec1[:, la:la + 1])
    decay = jnp.stack(decays)
    khb = jnp.stack(khb)
    kd = jnp.stack(kd)
    qg = jnp.stack(qg)
    nm = jnp.where(strict, _bdot_nt(jnp.stack(kbs), khb) * decay, 0.0)
    p = -nm
    inv = eye + p
    pb = p.astype(BF16)
    for _ in range(5):
        p = _bdot(pb, pb)
        pb = p.astype(BF16)
        inv = inv + _bdot(inv.astype(BF16), pb)
    uw = _bdot(inv.astype(BF16), jnp.concatenate([jnp.stack(vbs), jnp.stack(kes)], axis=2))
    u = uw[:, :, :DN_HD]
    w = uw[:, :, DN_HD:].astype(BF16)
    qk = (_bdot_nt(jnp.stack(qhb), khb) * decay).astype(BF16)
    dec = (jnp.stack(d0s), jnp.stack(d1s))
    st = states
    zeros_c = jnp.zeros((nseq * DN_H, DN_C, DN_HD), F32)
    o_parts = [None, None]
    for ci in ((1, 0) if reverse else (0, 1)):
        rows = slice(ci * DN_C, (ci + 1) * DN_C)
        ws = _bdot(jnp.concatenate([w[:, rows], qg[:, rows]], axis=1), st.astype(BF16))
        vnew = u[:, rows] - ws[:, :DN_C]
        vpad = jnp.concatenate([vnew, zeros_c] if ci == 0 else [zeros_c, vnew], axis=1).astype(BF16)
        o_parts[ci] = ws[:, DN_C:] + _bdot(qk[:, rows], vpad)
        st = st * dec[ci] + _bdot_tn(kd[:, rows], vnew.astype(BF16))
    o = jnp.concatenate(o_parts, axis=1)
    outs = [jnp.concatenate([o[b * DN_H + h] for h in range(DN_H)], axis=1) for b in range(nseq)]
    return outs, st


def _dn_kernel(qf_ref, baf_ref, qb_ref, bab_ref, dtb_ref, alog_ref, of_ref, ob_ref, s_ref):
    @pl.when(pl.program_id(0) == 0)
    def _():
        s_ref[...] = jnp.zeros_like(s_ref)

    nseq = qf_ref.shape[0]
    for d, (q_ref, ba_ref, o_ref) in enumerate(((qf_ref, baf_ref, of_ref), (qb_ref, bab_ref, ob_ref))):
        outs, st = _dn_group([q_ref[b] for b in range(nseq)], [ba_ref[b] for b in range(nseq)],
                             dtb_ref[d:d + 1, :], alog_ref[d:d + 1, :], s_ref[d], bool(d))
        for b in range(nseq):
            o_ref[b] = outs[b]
        s_ref[d] = st


def _dn(qkv, ba, dtb, alog, batch, seq_len):
    nb = seq_len // DN_BLK
    qkv3 = qkv.reshape(batch, seq_len, DN_CONV_DIM)
    ba3 = ba.reshape(batch, seq_len, LANES)
    fwd = lambda c: (0, c, 0)
    bwd = lambda c: (0, nb - 1 - c, 0)
    const = lambda c: (0, 0)
    of, ob = pl.pallas_call(
        _dn_kernel,
        name="dn",
        grid=(nb,),
        in_specs=[pl.BlockSpec((batch, DN_BLK, DN_CONV_DIM), fwd), pl.BlockSpec((batch, DN_BLK, LANES), fwd),
                  pl.BlockSpec((batch, DN_BLK, DN_CONV_DIM), bwd), pl.BlockSpec((batch, DN_BLK, LANES), bwd),
                  pl.BlockSpec((2, LANES), const), pl.BlockSpec((2, LANES), const)],
        out_specs=[pl.BlockSpec((batch, DN_BLK, DN_W), fwd), pl.BlockSpec((batch, DN_BLK, DN_W), bwd)],
        out_shape=[jax.ShapeDtypeStruct((batch, seq_len, DN_W), F32)] * 2,
        scratch_shapes=[pltpu.VMEM((2, batch * DN_H, DN_HD, DN_HD), F32)],
        compiler_params=_params(("arbitrary",)),
    )(qkv3, ba3, qkv3, ba3, dtb, alog)
    t = batch * seq_len
    return of.reshape(t, DN_W), ob.reshape(t, DN_W)


def _rope_tables(seq_len):
    rows = seq_len // GRID_W
    row_idx = jnp.repeat(jnp.arange(rows, dtype=F32), GRID_W)
    col_idx = jnp.tile(jnp.arange(GRID_W, dtype=F32), rows)
    axis_dim = ATT_HD // 2
    inv_freq = jnp.power(ROPE_THETA, -jnp.arange(0, axis_dim, 2, dtype=F32) / axis_dim)
    ra = row_idx[:, None] * inv_freq
    ca = col_idx[:, None] * inv_freq
    cos_t = jnp.concatenate([jnp.cos(ra), jnp.cos(ra), jnp.cos(ca), jnp.cos(ca)], axis=1)
    sin_t = jnp.concatenate([-jnp.sin(ra), jnp.sin(ra), -jnp.sin(ca), jnp.sin(ca)], axis=1)
    return cos_t, sin_t


def _qkprep_kernel(q_ref, k_ref, cos_ref, sin_ref, qn_ref, kn_ref, qo_ref, ko_ref):
    cos_t = cos_ref[...]
    sin_t = sin_ref[...]
    lane = lax.broadcasted_iota(I32, cos_t.shape, 1)
    low = (lane & (ATT_HD // 4)) == 0

    def prep(x, nw, scale):
        x = _rms(x.astype(F32)) * nw
        partner = jnp.where(low, pltpu.roll(x, ATT_HD - ATT_HD // 4, 1), pltpu.roll(x, ATT_HD // 4, 1))
        return (x * cos_t + partner * sin_t) * scale

    qscale = ATT_HD ** -0.5 * float(np.log2(np.e))
    qs = [prep(q_ref[:, h * ATT_HD:(h + 1) * ATT_HD], qn_ref[...], qscale) for h in range(ATT_H)]
    ks = [prep(k_ref[:, h * ATT_HD:(h + 1) * ATT_HD], kn_ref[...], 1.0) for h in range(ATT_KV)]
    qo_ref[...] = jnp.concatenate(qs, axis=1).astype(qo_ref.dtype)
    ko_ref[...] = jnp.concatenate(ks, axis=1).astype(ko_ref.dtype)


def _qkprep(q, k, cos_t, sin_t, qn, kn, seq_len):
    t = q.shape[0]
    tm = min(512, seq_len)
    per_seq = seq_len // tm
    return pl.pallas_call(
        _qkprep_kernel,
        name="qkprep",
        grid=(t // tm,),
        in_specs=[pl.BlockSpec((tm, ATT_W), lambda i: (i, 0)),
                  pl.BlockSpec((tm, ATT_KV_W), lambda i: (i, 0)),
                  pl.BlockSpec((tm, ATT_HD), lambda i: (i % per_seq, 0)),
                  pl.BlockSpec((tm, ATT_HD), lambda i: (i % per_seq, 0)),
                  pl.BlockSpec((1, ATT_HD), lambda i: (0, 0)),
                  pl.BlockSpec((1, ATT_HD), lambda i: (0, 0))],
        out_specs=[pl.BlockSpec((tm, ATT_W), lambda i: (i, 0)),
                   pl.BlockSpec((tm, ATT_KV_W), lambda i: (i, 0))],
        out_shape=[jax.ShapeDtypeStruct((t, ATT_W), BF16), jax.ShapeDtypeStruct((t, ATT_KV_W), BF16)],
        compiler_params=_params(("arbitrary",)),
    )(q, k, cos_t, sin_t, qn.reshape(1, ATT_HD), kn.reshape(1, ATT_HD))


def _flash_kernel(q_ref, k_ref, v_ref, o_ref, *, tk):
    tq = q_ref.shape[0]
    nk = k_ref.shape[0] // tk
    q2 = jnp.concatenate([q_ref[:, :ATT_HD], q_ref[:, ATT_HD:]], axis=0)
    m = l = acc = None
    for j in range(nk):
        sc = _dot_nt(q2, k_ref[j * tk:(j + 1) * tk, :])
        mx = jnp.max(sc, axis=-1, keepdims=True)
        if j == 0:
            m = mx
            p = jnp.exp2(sc - m)
            l = jnp.sum(p, axis=-1, keepdims=True)
            acc = _dot(p.astype(BF16), v_ref[j * tk:(j + 1) * tk, :])
        else:
            m_new = jnp.maximum(m, mx)
            p = jnp.exp2(sc - m_new)
            alpha = jnp.exp2(m - m_new)
            l = alpha * l + jnp.sum(p, axis=-1, keepdims=True)
            acc = alpha * acc + _dot(p.astype(BF16), v_ref[j * tk:(j + 1) * tk, :])
            m = m_new
    out = acc / l
    o_ref[...] = jnp.concatenate([out[:tq], out[tq:]], axis=1).astype(o_ref.dtype)


def _flash(qh, kh, v, batch, seq_len):
    t = qh.shape[0]
    tq = min(256, seq_len)
    tk = min(256, seq_len)
    nq = seq_len // tq
    rep_w = (ATT_H // ATT_KV) * ATT_HD
    return pl.pallas_call(
        functools.partial(_flash_kernel, tk=tk),
        name="flash",
        grid=(batch, ATT_KV, nq),
        in_specs=[pl.BlockSpec((tq, rep_w), lambda b, g, i: (b * nq + i, g)),
                  pl.BlockSpec((seq_len, ATT_HD), lambda b, g, i: (b, g)),
                  pl.BlockSpec((seq_len, ATT_HD), lambda b, g, i: (b, g))],
        out_specs=pl.BlockSpec((tq, rep_w), lambda b, g, i: (b * nq + i, g)),
        out_shape=jax.ShapeDtypeStruct((t, ATT_W), BF16),
        compiler_params=_params(("arbitrary",) * 3),
    )(qh, kh, v)


def _outproj_kernel(sf_ref, sb_ref, xs_ref, zs_ref, dsk_ref, nws_ref, of_ref, ob_ref, zd_ref, ya_ref, x_ref, mod_ref,
                    nwd_ref, nwp_ref, nwf_ref, wo_ref, wr_ref, x1_ref, h2_ref, aff_ref):
    y = sf_ref[...] + sb_ref[...] + dsk_ref[...] * xs_ref[...].astype(F32)
    y = y * _silu(zs_ref[...].astype(F32))
    half = SSD_W // SSD_G
    nws = nws_ref[...]
    ys = jnp.concatenate([_rms(y[:, g * half:(g + 1) * half]) * nws[:, g * half:(g + 1) * half]
                          for g in range(SSD_G)], axis=1).astype(BF16)
    yd = []
    for h in range(DN_H):
        cols = slice(h * DN_HD, (h + 1) * DN_HD)
        o = _rms(of_ref[:, cols] + ob_ref[:, cols]) * nwd_ref[...]
        yd.append((o * _silu(zd_ref[:, cols].astype(F32))).astype(BF16))
    yd = jnp.concatenate(yd, axis=1)
    m = (_dot(ys, wo_ref[0:SSD_W, :]) + _dot(yd, wo_ref[SSD_W:SSD_W + DN_W, :])
         + _dot(ya_ref[...], wo_ref[SSD_W + DN_W:, :]))
    mod = mod_ref[0]
    x1 = x_ref[...] + mod[2:3, :] * (_rms(m) * nwp_ref[...])
    x1_ref[...] = x1
    h2 = _rms(x1) * nwf_ref[...] * (1.0 + mod[4:5, :]) + mod[3:4, :]
    h2_hi = h2.astype(BF16)
    h2_ref[...] = h2_hi
    h2_lo = (h2 - h2_hi.astype(F32)).astype(BF16)
    logits = _dot(h2_hi, wr_ref[0]) + _dot(h2_lo, wr_ref[0]) + _dot(h2_hi, wr_ref[1])
    lane = lax.broadcasted_iota(I32, logits.shape, 1)
    logits = jnp.where(lane < N_EXPERTS, logits, -jnp.inf)
    ex = jnp.exp(logits - jnp.max(logits, axis=-1, keepdims=True))
    aff = ex / jnp.sum(ex, axis=-1, keepdims=True)
    aff_ref[...] = aff.T[0:N_EXPERTS, :]


def _outproj(sf, sb, xact, zs, dsk, nws, of, ob, zd, ya, x, mod, nwd, nwp, nwf, wo_bf, wr_p, seq_len):
    t = x.shape[0]
    tm = min(512, seq_len)
    per_seq = seq_len // tm
    row = lambda i: (i, 0)
    const = lambda i: (0, 0)
    return pl.pallas_call(
        _outproj_kernel,
        name="outproj",
        grid=(t // tm,),
        in_specs=[pl.BlockSpec((tm, SSD_W), row), pl.BlockSpec((tm, SSD_W), row),
                  pl.BlockSpec((tm, SSD_W), row),
                  pl.BlockSpec((tm, SSD_W), row), pl.BlockSpec((1, SSD_W), const), pl.BlockSpec((1, SSD_W), const),
                  pl.BlockSpec((tm, DN_W), row), pl.BlockSpec((tm, DN_W), row),
                  pl.BlockSpec((tm, DN_W), row), pl.BlockSpec((tm, ATT_W), row),
                  pl.BlockSpec((tm, D_MODEL), row),
                  pl.BlockSpec((1, 6, D_MODEL), lambda i: (i // per_seq, 0, 0)),
                  pl.BlockSpec((1, DN_HD), const),
                  pl.BlockSpec((1, D_MODEL), const), pl.BlockSpec((1, D_MODEL), const),
                  pl.BlockSpec((2 * D_MODEL, D_MODEL), const),
                  pl.BlockSpec((2, D_MODEL, LANES), lambda i: (0, 0, 0))],
        out_specs=[pl.BlockSpec((tm, D_MODEL), row), pl.BlockSpec((tm, D_MODEL), row),
                   pl.BlockSpec((N_EXPERTS, tm), lambda i: (0, i))],
        out_shape=[jax.ShapeDtypeStruct((t, D_MODEL), F32), jax.ShapeDtypeStruct((t, D_MODEL), BF16),
                   jax.ShapeDtypeStruct((N_EXPERTS, t), F32)],
        compiler_params=_params(("arbitrary",)),
    )(sf, sb, xact, zs, dsk, nws, of, ob, zd, ya, x, mod, nwd, nwp.reshape(1, D_MODEL), nwf.reshape(1, D_MODEL),
      wo_bf, wr_p)


def _route_kernel(aff_ref, pos_ref, offs_ref, *, cap):
    ne, t = aff_ref.shape
    nb = t // LANES
    capf = float(cap)

    def bits_of(x):
        return pltpu.bitcast(x, I32)

    def bis(i, thr):
        cand = thr | jnp.left_shift(jnp.int32(1), 30 - i)
        cnt = jnp.sum(jnp.where(bits_of(aff_ref[...]) >= cand, 1.0, 0.0), axis=1, keepdims=True)
        return jnp.where(cnt >= capf, cand, thr)

    thr = lax.fori_loop(0, 31, bis, jnp.zeros((ne, 1), I32))
    n_gt = jnp.sum(jnp.where(bits_of(aff_ref[...]) > thr, 1.0, 0.0), axis=1, keepdims=True)
    need_eq = capf - n_gt

    r = lax.broadcasted_iota(I32, (LANES, LANES), 0)
    s = lax.broadcasted_iota(I32, (LANES, LANES), 1)
    triu = jnp.where(r <= s, 1.0, 0.0).astype(BF16)
    lane_nb = lax.broadcasted_iota(I32, (ne, nb), 1)

    offs_ref[...] = jnp.zeros_like(offs_ref)

    def tile(i, carry):
        run_sel, run_eq = carry
        start = pl.multiple_of(i * LANES, LANES)
        b = bits_of(aff_ref[:, pl.ds(start, LANES)])
        gt = b > thr
        eq = jnp.where(b == thr, 1.0, 0.0)
        eq_rank = _dot(eq.astype(BF16), triu) - eq + run_eq
        sel = jnp.where(gt | ((eq > 0.0) & (eq_rank < need_eq)), 1.0, 0.0)
        pos = _dot(sel.astype(BF16), triu) - sel + run_sel
        pos_ref[:, pl.ds(start, LANES)] = jnp.where(sel > 0.0, pos, -1.0).astype(I32)
        offs_ref[...] = jnp.where(lane_nb == i, run_sel.astype(I32), offs_ref[...])
        return (run_sel + jnp.sum(sel, axis=1, keepdims=True),
                run_eq + jnp.sum(eq, axis=1, keepdims=True))

    zero = jnp.zeros((ne, 1), F32)
    lax.fori_loop(0, nb, tile, (zero, zero))


def _route(aff_t, cap):
    ne, t = aff_t.shape
    nb = t // LANES
    return pl.pallas_call(
        functools.partial(_route_kernel, cap=cap),
        name="route",
        out_shape=[jax.ShapeDtypeStruct((ne, t), I32), jax.ShapeDtypeStruct((ne, nb), I32)],
        compiler_params=pltpu.CompilerParams(vmem_limit_bytes=VMEM_LIMIT),
    )(aff_t)


MOE_SUB = 128
MOE_ALIGN = 16
MOE_WIN_SMALL = 32 + MOE_ALIGN
MOE_WIN_FULL = MOE_SUB + MOE_ALIGN
MOE_TILE = 288
MOE_HALF = MOE_TILE // 2
MOE_FC = 512
MOE_BLOCK = 2048


def _moe_kernel(offs_ref, h_ref, pos_ref, gate_ref, wg_ref, wu_ref, wd_ref, o_ref,
                xe_ref, ye_ref, gb_ref, *, cap, nsub):
    sb = pl.program_id(0)
    e = pl.program_id(1)
    nsb = pl.num_programs(0)
    nblk = nsb * nsub
    start = offs_ref[e, sb * nsub]
    end = jnp.where(sb == nsb - 1, cap, offs_ref[e, jnp.minimum((sb + 1) * nsub, nblk - 1)])
    count = end - start
    nfull = count // MOE_TILE

    @pl.when((sb == 0) & (e == 0))
    def _():
        xe_ref[...] = jnp.zeros_like(xe_ref)
        ye_ref[...] = jnp.zeros_like(ye_ref)
        gb_ref[...] = jnp.zeros_like(gb_ref)

    @pl.when(e == 0)
    def _():
        o_ref[...] = jnp.zeros_like(o_ref)

    def window(j):
        lo = offs_ref[e, sb * nsub + j] - start
        hi = (offs_ref[e, sb * nsub + j + 1] - start) if j + 1 < nsub else count
        a = pl.multiple_of((lo // MOE_ALIGN) * MOE_ALIGN, MOE_ALIGN)
        return lo, hi, a

    wins = [window(j) for j in range(nsub)]
    all_small = functools.reduce(jnp.logical_and, [hi - a <= MOE_WIN_SMALL for (_, hi, a) in wins])

    def onehots(win):
        rid = lax.broadcasted_iota(I32, (win, MOE_SUB), 0)
        out = []
        for j, (_, _, a) in enumerate(wins):
            posrow = pos_ref[0, :, j * MOE_SUB:(j + 1) * MOE_SUB]
            out.append((posrow - (start + a) == rid) & (posrow >= 0))
        return out, jnp.stack([jnp.where(oh, 1.0, 0.0).astype(BF16) for oh in out])

    def for_window_size(body):
        @pl.when(all_small)
        def _():
            body(MOE_WIN_SMALL)

        @pl.when(jnp.logical_not(all_small))
        def _():
            body(MOE_WIN_FULL)

    def gather(win):
        ohs, oh_b = onehots(win)
        comp = _bdot(oh_b, h_ref[...].reshape(nsub, MOE_SUB, D_MODEL))
        rid = lax.broadcasted_iota(I32, (MOE_ALIGN, 1), 0)
        for j, (lo, _, a) in enumerate(wins):
            gate = jnp.sum(jnp.where(ohs[j], gate_ref[0, :, j * MOE_SUB:(j + 1) * MOE_SUB], 0.0),
                           axis=1, keepdims=True)
            gate = jnp.broadcast_to(gate, (win, LANES))
            keep = rid < (lo - a)
            a1 = pl.multiple_of(a + MOE_ALIGN, MOE_ALIGN)
            xe_ref[pl.ds(a, MOE_ALIGN), :] = jnp.where(keep, xe_ref[pl.ds(a, MOE_ALIGN), :].astype(F32),
                                                       comp[j, :MOE_ALIGN]).astype(BF16)
            xe_ref[pl.ds(a1, win - MOE_ALIGN), :] = comp[j, MOE_ALIGN:].astype(BF16)
            gb_ref[pl.ds(a, MOE_ALIGN), :] = jnp.where(keep, gb_ref[pl.ds(a, MOE_ALIGN), :], gate[:MOE_ALIGN])
            gb_ref[pl.ds(a1, win - MOE_ALIGN), :] = gate[MOE_ALIGN:]

    def scatter(win):
        _, oh_b = onehots(win)
        yw = jnp.stack([(ye_ref[pl.ds(a, win), :] * gb_ref[pl.ds(a, win), 0:1]).astype(BF16) for (_, _, a) in wins])
        o_ref[...] += _bdot_tn(oh_b, yw).reshape(nsub * MOE_SUB, D_MODEL)

    for_window_size(gather)

    nfc = wg_ref.shape[3] // MOE_FC

    def ffn_rows(r0, rows):
        x = xe_ref[pl.ds(r0, rows), :]
        y = None
        for fc in range(nfc):
            cols = slice(fc * MOE_FC, (fc + 1) * MOE_FC)
            hid = (_silu(_dot(x, wg_ref[0, 0, :, cols])) * _dot(x, wu_ref[0, 0, :, cols])).astype(BF16)
            part = _dot(hid, wd_ref[0, 0, cols, :])
            y = part if y is None else y + part
        ye_ref[pl.ds(r0, rows), :] = y

    def ffn_tile(i, carry):
        ffn_rows(pl.multiple_of(i * MOE_TILE, MOE_ALIGN), MOE_TILE)
        return carry

    lax.fori_loop(0, nfull, ffn_tile, 0)
    rem = count - nfull * MOE_TILE
    rem0 = pl.multiple_of(nfull * MOE_TILE, MOE_ALIGN)

    @pl.when(rem > MOE_HALF)
    def _():
        ffn_rows(rem0, MOE_TILE)

    @pl.when((rem > 0) & (rem <= MOE_HALF))
    def _():
        ffn_rows(rem0, MOE_HALF)

    for_window_size(scatter)


def _moe(offs, h2, pos, aff_t, wg, wu, wd, layer, cap, sblk):
    t = h2.shape[0]
    nsb = t // sblk
    nsub = sblk // MOE_SUB
    rows = sblk + MOE_TILE + MOE_WIN_FULL
    once = pl.Buffered(1)
    grid_spec = pltpu.PrefetchScalarGridSpec(
        num_scalar_prefetch=1,
        grid=(nsb, N_EXPERTS),
        in_specs=[pl.BlockSpec((sblk, D_MODEL), lambda s, e, o: (s, 0), pipeline_mode=once),
                  pl.BlockSpec((1, 1, sblk), lambda s, e, o: (e, 0, s)),
                  pl.BlockSpec((1, 1, sblk), lambda s, e, o: (e, 0, s)),
                  pl.BlockSpec((1, 1, D_MODEL, EXPERT_FF), lambda s, e, o: (layer, e, 0, 0)),
                  pl.BlockSpec((1, 1, D_MODEL, EXPERT_FF), lambda s, e, o: (layer, e, 0, 0)),
                  pl.BlockSpec((1, 1, EXPERT_FF, D_MODEL), lambda s, e, o: (layer, e, 0, 0))],
        out_specs=pl.BlockSpec((sblk, D_MODEL), lambda s, e, o: (s, 0), pipeline_mode=once),
        scratch_shapes=[pltpu.VMEM((rows, D_MODEL), BF16), pltpu.VMEM((rows, D_MODEL), F32),
                        pltpu.VMEM((rows, LANES), F32)],
    )
    return pl.pallas_call(
        functools.partial(_moe_kernel, cap=cap, nsub=nsub),
        name="moe",
        grid_spec=grid_spec,
        out_shape=jax.ShapeDtypeStruct((t, D_MODEL), F32),
        compiler_params=pltpu.CompilerParams(dimension_semantics=("arbitrary",) * 2,
                                             vmem_limit_bytes=MOE_VMEM_LIMIT),
    )(offs, h2, pos.reshape(N_EXPERTS, 1, t), aff_t.reshape(N_EXPERTS, 1, t), wg, wu, wd)


def _post_kernel(x_ref, f_ref, mod_ref, nw_ref, o_ref):
    mod = mod_ref[0]
    o_ref[...] = x_ref[...] + mod[5:6, :] * (_rms(f_ref[...]) * nw_ref[...])


def _post(x1, f, mod, nw, seq_len):
    t = x1.shape[0]
    tm = min(512, seq_len)
    per_seq = seq_len // tm
    row = lambda i: (i, 0)
    return pl.pallas_call(
        _post_kernel,
        name="post",
        grid=(t // tm,),
        in_specs=[pl.BlockSpec((tm, D_MODEL), row), pl.BlockSpec((tm, D_MODEL), row),
                  pl.BlockSpec((1, 6, D_MODEL), lambda i: (i // per_seq, 0, 0)),
                  pl.BlockSpec((1, D_MODEL), lambda i: (0, 0))],
        out_specs=pl.BlockSpec((tm, D_MODEL), row),
        out_shape=jax.ShapeDtypeStruct((t, D_MODEL), F32),
        compiler_params=_params(("arbitrary",)),
    )(x1, f, mod, nw.reshape(1, D_MODEL))


def _pad_lanes(v, offset=0):
    out = jnp.zeros((1, LANES), F32)
    return out.at[0, offset:offset + v.shape[0]].set(v.astype(F32))


def _split_bf16(w):
    hi = w.astype(BF16)
    return jnp.stack([hi, (w - hi.astype(F32)).astype(BF16)])


def _prep_layer(p, l):
    w_in = p["w_in"][l]
    cols, off = [], 0
    for (_, width, stored, _) in _IN_SEGS:
        seg = w_in[:, off:off + width]
        if stored != width:
            seg = jnp.pad(seg, ((0, 0), (0, stored - width)))
        cols.append(seg)
        off += width
    q = {
        "w_ada": p["w_ada"][l].astype(BF16), "b_ada": p["b_ada"][l],
        "w_in": jnp.concatenate(cols, axis=1).astype(BF16),
        "w_out": p["w_out"][l].astype(BF16),
        "w_router": _split_bf16(jnp.pad(p["w_router"][l], ((0, 0), (0, LANES - N_EXPERTS)))),
        "ssd_bias": jnp.concatenate([_pad_lanes(p["ssd_dt_bias"][l, d]) for d in range(2)], axis=0),
        "ssd_alog": jnp.concatenate([_pad_lanes(p["ssd_a_log"][l, d]) for d in range(2)], axis=0),
        "ssd_d": jnp.repeat(p["ssd_d"][l], SSD_P).reshape(1, SSD_W),
        "ssd_norm": p["ssd_norm"][l].reshape(1, SSD_W),
        "dn_bias": jnp.concatenate([_pad_lanes(p["dn_dt_bias"][l, d], DN_H) for d in range(2)], axis=0),
        "dn_alog": jnp.concatenate([_pad_lanes(p["dn_a_log"][l, d], DN_H) for d in range(2)], axis=0),
        "dn_norm": p["dn_norm"][l].reshape(1, DN_HD),
    }
    q["conv_w"] = jnp.stack([jnp.pad(p[name][l], ((0, SUBLANES - CONV_K), (0, 0))) for name in ("conv_ssd_w", "conv_dn_w")])
    q["conv_b"] = jnp.stack([p["conv_ssd_b"][l], jnp.zeros((DN_CONV_DIM,), F32)])
    for name in ("norm_mix_pre", "norm_mix_post", "norm_ffn_pre", "norm_ffn_post", "q_norm", "k_norm"):
        q[name] = p[name][l]
    return q


def _trunk(x, c, layers, experts, sblk):
    batch, seq_len, _ = x.shape
    t = batch * seq_len
    cap = EC_CAPACITY * t // N_EXPERTS
    cos_t, sin_t = _rope_tables(seq_len)
    x = x.reshape(t, D_MODEL)
    for l, q in enumerate(layers):
        mod = _ada(c, q["w_ada"], q["b_ada"])
        z_ssd, xact, dtp, qkvact, z_dn, ba_dn, qa, ka, va = _inproj(x, mod, q["norm_mix_pre"], q["w_in"],
                                                                    q["conv_w"], q["conv_b"], seq_len)
        s_f, s_b = _ssd(xact, dtp, q["ssd_bias"], q["ssd_alog"], batch, seq_len)
        o_f, o_b = _dn(qkvact, ba_dn, q["dn_bias"], q["dn_alog"], batch, seq_len)
        qh, kh = _qkprep(qa, ka, cos_t, sin_t, q["q_norm"], q["k_norm"], seq_len)
        y_att = _flash(qh, kh, va, batch, seq_len)
        x1, h2, aff_t = _outproj(s_f, s_b, xact, z_ssd, q["ssd_d"], q["ssd_norm"], o_f, o_b, z_dn, y_att, x, mod,
                                 q["dn_norm"], q["norm_mix_post"], q["norm_ffn_pre"], q["w_out"], q["w_router"],
                                 seq_len)
        pos, offs = _route(aff_t, cap)
        f = _moe(offs, h2, pos, aff_t, *experts, l, cap, sblk)
        x = _post(x1, f, mod, q["norm_ffn_post"], seq_len)
    return x.reshape(batch, seq_len, D_MODEL)


def kernel(x_prompt, x_sample, c_prompt, c_sample, w_ada, b_ada, norm_mix_pre, norm_mix_post, w_in, conv_ssd_w, conv_ssd_b, ssd_dt_bias, ssd_a_log, ssd_d, ssd_norm, conv_dn_w, dn_dt_bias, dn_a_log, dn_norm, q_norm, k_norm, w_out, norm_ffn_pre, norm_ffn_post, w_router, w_gate, w_up, w_down):
    p = dict(w_ada=w_ada, b_ada=b_ada, norm_mix_pre=norm_mix_pre, norm_mix_post=norm_mix_post,
             w_in=w_in, conv_ssd_w=conv_ssd_w, conv_ssd_b=conv_ssd_b, ssd_dt_bias=ssd_dt_bias,
             ssd_a_log=ssd_a_log, ssd_d=ssd_d, ssd_norm=ssd_norm, conv_dn_w=conv_dn_w,
             dn_dt_bias=dn_dt_bias, dn_a_log=dn_a_log, dn_norm=dn_norm, q_norm=q_norm, k_norm=k_norm,
             w_out=w_out, norm_ffn_pre=norm_ffn_pre, norm_ffn_post=norm_ffn_post,
             w_router=w_router, w_gate=w_gate, w_up=w_up, w_down=w_down)
    layers = [_prep_layer(p, l) for l in range(w_in.shape[0])]
    experts = (w_gate.astype(BF16), w_up.astype(BF16), w_down.astype(BF16))
    y_prompt = _trunk(x_prompt, c_prompt, layers, experts, MOE_BLOCK)
    y_sample = _trunk(x_sample, c_sample, layers, experts, MOE_BLOCK)
    return (y_prompt, y_sample)
```

```python
import functools

import jax
import jax.numpy as jnp
import numpy as np
from jax import lax
from jax.experimental import pallas as pl
from jax.experimental.pallas import tpu as pltpu

F32 = jnp.float32
BF16 = jnp.bfloat16
I32 = jnp.int32

D_MODEL = 1024
SSD_W = 1024
SSD_P = 64
SSD_H = 16
SSD_G = 2
SSD_N = 128
SSD_CONV_DIM = SSD_W + 2 * SSD_G * SSD_N
DN_W = 512
DN_HD = 128
DN_H = 4
DN_CONV_DIM = 3 * DN_W
ATT_W = 512
ATT_HD = 128
ATT_H = 4
ATT_KV = 2
ATT_KV_W = ATT_KV * ATT_HD
GRID_W = 64
ROPE_THETA = 10000.0
N_EXPERTS = 16
EC_CAPACITY = 2
EXPERT_FF = 2048
EPS = 1e-6
CONV_K = 5

LANES = 128
SUBLANES = 8
VMEM_LIMIT = 56 * 1024 * 1024
MOE_VMEM_LIMIT = 60 * 1024 * 1024

_IN_SEGS = (
    ("z_ssd", SSD_W, SSD_W, BF16),
    ("xbc", SSD_CONV_DIM, SSD_CONV_DIM, BF16),
    ("dt", SSD_H, LANES, F32),
    ("qkv_dn", DN_CONV_DIM, DN_CONV_DIM, BF16),
    ("z_dn", DN_W, DN_W, BF16),
    ("ba_dn", 2 * DN_H, LANES, F32),
    ("q", ATT_W, ATT_W, BF16),
    ("k", ATT_KV_W, ATT_KV_W, BF16),
    ("v", ATT_KV_W, ATT_KV_W, BF16),
)


def _params(sem):
    return pltpu.CompilerParams(dimension_semantics=sem, vmem_limit_bytes=VMEM_LIMIT)


def _sigmoid(x):
    return 1.0 / (1.0 + jnp.exp(-x))


def _silu(x):
    return x * _sigmoid(x)


def _softplus(x):
    return jnp.maximum(x, 0.0) + jnp.log(1.0 + jnp.exp(-jnp.abs(x)))


def _rms(x):
    return x * lax.rsqrt(jnp.mean(x * x, axis=-1, keepdims=True) + EPS)


def _dot(a, b):
    return jnp.dot(a, b, preferred_element_type=F32)


def _dot_nt(a, b):
    return lax.dot_general(a, b, (((1,), (1,)), ((), ())), preferred_element_type=F32)


def _dot_tn(a, b):
    return lax.dot_general(a, b, (((0,), (0,)), ((), ())), preferred_element_type=F32)


def _cumsum_mm(a, b, split_lhs=False):
    x = a if split_lhs else b
    hi = x.astype(BF16)
    r1 = x - hi.astype(F32)
    mid = r1.astype(BF16)
    lo = (r1 - mid.astype(F32)).astype(BF16)
    if split_lhs:
        return _dot(hi, b) + _dot(mid, b) + _dot(lo, b)
    return _dot(a, hi) + _dot(a, mid) + _dot(a, lo)


def _ada_kernel(c_ref, w_ref, b_ref, o_ref):
    o_ref[...] = _dot(_silu(c_ref[...]).astype(BF16), w_ref[...]) + b_ref[...]


def _ada(c, w_bf, b):
    nb = c.shape[0]
    rows = -(-nb // SUBLANES) * SUBLANES
    cp = jnp.zeros((rows, D_MODEL), F32).at[:nb].set(c)
    n = w_bf.shape[1]
    tn = 1024
    out = pl.pallas_call(
        _ada_kernel,
        name="ada",
        grid=(n // tn,),
        in_specs=[pl.BlockSpec((rows, D_MODEL), lambda j: (0, 0)),
                  pl.BlockSpec((D_MODEL, tn), lambda j: (0, j)),
                  pl.BlockSpec((1, tn), lambda j: (0, j))],
        out_specs=pl.BlockSpec((rows, tn), lambda j: (0, j)),
        out_shape=jax.ShapeDtypeStruct((rows, n), F32),
        compiler_params=_params(("arbitrary",)),
    )(cp, w_bf, b.reshape(1, n))
    return out[:nb].reshape(nb, 6, D_MODEL)


_HALO = SUBLANES
_CONV_SEGS = ("xbc", "qkv_dn")


def _inproj_kernel(x_ref, xp_ref, xn_ref, mod_ref, nw_ref, w_ref, cw_ref, cb_ref, *rest, per_seq):
    out_refs, bufs = rest[:len(_IN_SEGS)], rest[len(_IN_SEGS):]
    i = pl.program_id(0)
    tm = x_ref.shape[0]
    mod = mod_ref[0]

    def norm_mod(x):
        return _rms(x) * nw_ref[...] * (1.0 + mod[1:2, :]) + mod[0:1, :]

    keep_prev = jnp.where((i % per_seq) == 0, 0.0, 1.0)
    keep_next = jnp.where((i % per_seq) == per_seq - 1, 0.0, 1.0)
    hrow = lax.broadcasted_iota(I32, (2 * _HALO, 1), 0)
    halo = norm_mod(jnp.concatenate([xp_ref[...], xn_ref[...]], axis=0)) * jnp.where(hrow < _HALO, keep_prev, keep_next)
    hb = norm_mod(x_ref[...]).astype(BF16)
    hcat = jnp.concatenate([hb, halo.astype(BF16)], axis=0)
    off = 0
    ci = 0
    for o_ref, (name, _, width, _) in zip(out_refs, _IN_SEGS):
        wseg = w_ref[:, off:off + width]
        off += width
        if name not in _CONV_SEGS:
            o_ref[...] = _dot(hb, wseg).astype(o_ref.dtype)
            continue
        buf = bufs[ci]
        full = _dot(hcat, wseg)
        buf[_HALO:_HALO + tm, :] = full[0:tm]
        buf[0:_HALO, :] = full[tm:tm + _HALO]
        buf[_HALO + tm:2 * _HALO + tm, :] = full[tm + _HALO:tm + 2 * _HALO]
        acc = jnp.zeros((tm, width), F32) + cb_ref[ci:ci + 1, :]
        for k in range(CONV_K):
            lo = _HALO - CONV_K // 2 + k
            acc = acc + buf[lo:lo + tm, :] * cw_ref[ci, k:k + 1, :]
        o_ref[...] = _silu(acc).astype(o_ref.dtype)
        ci += 1


def _inproj(x, mod, nw, w_p, conv_w, conv_b, seq_len):
    t = x.shape[0]
    tm = min(512, seq_len)
    per_seq = seq_len // tm
    ntot = w_p.shape[1]
    hb = tm // _HALO
    nh = t // _HALO
    cw = conv_w.shape[2]
    out_shapes = [jax.ShapeDtypeStruct((t, sw), dt) for (_, _, sw, dt) in _IN_SEGS]
    out_specs = [pl.BlockSpec((tm, sw), lambda i: (i, 0)) for (_, _, sw, _) in _IN_SEGS]
    return pl.pallas_call(
        functools.partial(_inproj_kernel, per_seq=per_seq),
        name="inproj",
        grid=(t // tm,),
        in_specs=[pl.BlockSpec((tm, D_MODEL), lambda i: (i, 0)),
                  pl.BlockSpec((_HALO, D_MODEL), lambda i: (jnp.maximum(i * hb - 1, 0), 0)),
                  pl.BlockSpec((_HALO, D_MODEL), lambda i: (jnp.minimum((i + 1) * hb, nh - 1), 0)),
                  pl.BlockSpec((1, 6, D_MODEL), lambda i: (i // per_seq, 0, 0)),
                  pl.BlockSpec((1, D_MODEL), lambda i: (0, 0)),
                  pl.BlockSpec((D_MODEL, ntot), lambda i: (0, 0)),
                  pl.BlockSpec((len(_CONV_SEGS), SUBLANES, cw), lambda i: (0, 0, 0)),
                  pl.BlockSpec((len(_CONV_SEGS), cw), lambda i: (0, 0))],
        out_specs=out_specs,
        out_shape=out_shapes,
        scratch_shapes=[pltpu.VMEM((tm + 2 * _HALO, cw), F32) for _ in _CONV_SEGS],
        compiler_params=_params(("arbitrary",)),
    )(x, x, x, mod, nw.reshape(1, D_MODEL), w_p, conv_w, conv_b)


SSD_Q = 128


def _ssd_kernel(xf_ref, dtf_ref, xb_ref, dtb_ref, bias_ref, alog_ref, sel_ref, yf_ref, yb_ref, h_ref):
    q = SSD_Q
    dirs = (0, 1)

    @pl.when(pl.program_id(1) == 0)
    def _():
        h_ref[...] = jnp.zeros_like(h_ref)

    r = lax.broadcasted_iota(I32, (q, q), 0)
    s = lax.broadcasted_iota(I32, (q, q), 1)
    mask = (r >= s, s >= r)
    tri = [jnp.where(m, 1.0, 0.0).astype(BF16) for m in mask]
    lo = s < SSD_P
    last = (q - 1, 0)

    dtp = (dtf_ref[...], dtb_ref[...])
    dt = [_softplus(dtp[d] + bias_ref[d:d + 1, :]) for d in dirs]
    dta = [dt[d] * (-jnp.exp(alog_ref[d:d + 1, :])) for d in dirs]
    acs = [_cumsum_mm(tri[d], dta[d]) for d in dirs]
    acs_t = [acs[d].T for d in dirs]
    dt_t = [dt[d].T for d in dirs]
    eacs = [jnp.exp(acs[d]) for d in dirs]
    wcols = [jnp.exp(acs[d][last[d]:last[d] + 1, :] - acs[d]) * dt[d] for d in dirs]
    cols = jnp.concatenate([eacs[0], wcols[0], eacs[1], wcols[1]], axis=0).astype(BF16)
    spread = _dot(cols, sel_ref[...])
    eacs_w = [spread[2 * d * q:(2 * d + 1) * q] for d in dirs]
    wcols_w = [spread[(2 * d + 1) * q:(2 * d + 2) * q] for d in dirs]
    dec_w = _cumsum_mm(jnp.concatenate([eacs[d][last[d]:last[d] + 1, :] for d in dirs], axis=0), sel_ref[...],
                       split_lhs=True)

    xact = (xf_ref[...], xb_ref[...])
    ys = ([], [])
    for g in range(SSD_G):
        bm = [xact[d][:, SSD_W + g * SSD_N:SSD_W + (g + 1) * SSD_N] for d in dirs]
        cm = [xact[d][:, SSD_W + (SSD_G + g) * SSD_N:SSD_W + (SSD_G + g + 1) * SSD_N] for d in dirs]
        cb = [_dot_nt(cm[d], bm[d]) for d in dirs]
        hg = [h_ref[d, g] for d in dirs]
        cmh = [_dot(cm[d], hg[d].astype(BF16)) for d in dirs]
        xw_parts = ([], [])
        for pp in range(4):
            j0 = g * 8 + pp * 2
            lanes = slice((g * 4 + pp) * LANES, (g * 4 + pp + 1) * LANES)
            xpair = [xact[d][:, lanes].astype(F32) for d in dirs]
            ms = ([], [])
            for j in (j0, j0 + 1):
                seg = [acs[d][:, j:j + 1] - acs_t[d][j:j + 1, :] for d in dirs]
                lm = [jnp.exp(jnp.where(mask[d], seg[d], -jnp.inf)) for d in dirs]
                for d in dirs:
                    ms[d].append((cb[d] * lm[d] * dt_t[d][j:j + 1, :]).astype(BF16))
            mcat = [jnp.concatenate(ms[d], axis=1) for d in dirs]
            x2 = [jnp.concatenate([jnp.where(lo, xpair[d], 0.0), jnp.where(lo, 0.0, xpair[d])], axis=0).astype(BF16)
                  for d in dirs]
            yd = [_dot(mcat[d], x2[d]) for d in dirs]
            for d in dirs:
                ys[d].append(yd[d] + cmh[d][:, pp * LANES:(pp + 1) * LANES] * eacs_w[d][:, lanes])
                xw_parts[d].append((xpair[d] * wcols_w[d][:, lanes]).astype(BF16))
        half = slice(g * (SSD_W // SSD_G), (g + 1) * (SSD_W // SSD_G))
        upd = [_dot_tn(bm[d], jnp.concatenate(xw_parts[d], axis=1)) for d in dirs]
        for d in dirs:
            h_ref[d, g] = hg[d] * dec_w[d:d + 1, half] + upd[d]
    yf_ref[...] = jnp.concatenate(ys[0], axis=1)
    yb_ref[...] = jnp.concatenate(ys[1], axis=1)


def _ssd(xact, dtp, bias, alog, batch, seq_len):
    t = xact.shape[0]
    nc = seq_len // SSD_Q
    fwd = lambda b, c: (b * nc + c, 0)
    bwd = lambda b, c: (b * nc + nc - 1 - c, 0)
    const = lambda b, c: (0, 0)
    sel = jnp.repeat(jnp.eye(LANES, SSD_H, dtype=BF16), SSD_P, axis=1)
    return pl.pallas_call(
        _ssd_kernel,
        name="ssd",
        grid=(batch, nc),
        in_specs=[pl.BlockSpec((SSD_Q, SSD_CONV_DIM), fwd), pl.BlockSpec((SSD_Q, LANES), fwd),
                  pl.BlockSpec((SSD_Q, SSD_CONV_DIM), bwd), pl.BlockSpec((SSD_Q, LANES), bwd),
                  pl.BlockSpec((2, LANES), const), pl.BlockSpec((2, LANES), const),
                  pl.BlockSpec((LANES, SSD_W), const)],
        out_specs=[pl.BlockSpec((SSD_Q, SSD_W), fwd), pl.BlockSpec((SSD_Q, SSD_W), bwd)],
        out_shape=[jax.ShapeDtypeStruct((t, SSD_W), F32)] * 2,
        scratch_shapes=[pltpu.VMEM((2, SSD_G, SSD_N, SSD_W // SSD_G), F32)],
        compiler_params=_params(("arbitrary", "arbitrary")),
    )(xact, dtp, xact, dtp, bias, alog, sel)


DN_BLK = 128
DN_C = 64


def _dn_masks(reverse):
    n = DN_BLK
    r = lax.broadcasted_iota(I32, (n, n), 0)
    s = lax.broadcasted_iota(I32, (n, n), 1)
    same = (r >= DN_C) == (s >= DN_C)
    incl = same & ((s >= r) if reverse else (r >= s))
    strict = same & ((s > r) if reverse else (r > s))
    return r, incl, strict, jnp.where(incl, 1.0, 0.0).astype(BF16), jnp.where(r == s, 1.0, 0.0)


def _bdot(a, b):
    return lax.dot_general(a, b, (((2,), (1,)), ((0,), (0,))), preferred_element_type=F32)


def _bdot_nt(a, b):
    return lax.dot_general(a, b, (((2,), (2,)), ((0,), (0,))), preferred_element_type=F32)


def _bdot_tn(a, b):
    return lax.dot_general(a, b, (((1,), (1,)), ((0,), (0,))), preferred_element_type=F32)


def _dn_group(qkvs, bas, dtb, alog, states, reverse):
    n = DN_BLK
    nseq = len(qkvs)
    r, incl, strict, tri, eye = _dn_masks(reverse)
    ba = jnp.concatenate(bas, axis=1)
    beta = _sigmoid(ba)
    gl = -jnp.exp(jnp.concatenate([alog] * nseq, axis=1)) * _softplus(ba + jnp.concatenate([dtb] * nseq, axis=1))
    gcs = _cumsum_mm(tri, gl)
    if reverse:
        t0, t1 = gcs[0:1, :], gcs[DN_C:DN_C + 1, :]
    else:
        t0, t1 = gcs[DN_C - 1:DN_C, :], gcs[n - 1:n, :]
    eg = jnp.exp(gcs)
    ekd = jnp.exp(jnp.where(r[:, 0:1] < DN_C, t0, t1) - gcs)
    dec0 = jnp.exp(t0)
    dec1 = jnp.exp(t1)

    qhb, qg, khb, kd, kbs, kes, vbs, decays, d0s, d1s = [], [], [], [], [], [], [], [], [], []
    for b in range(nseq):
        gcs_t = gcs[:, b * LANES:(b + 1) * LANES].T
        for h in range(DN_H):
            lb = b * LANES + h
            la = lb + DN_H
            qh = qkvs[b][:, h * DN_HD:(h + 1) * DN_HD].astype(F32)
            kh = qkvs[b][:, DN_W + h * DN_HD:DN_W + (h + 1) * DN_HD].astype(F32)
            vh = qkvs[b][:, 2 * DN_W + h * DN_HD:2 * DN_W + (h + 1) * DN_HD].astype(F32)
            qh = qh * lax.rsqrt(jnp.sum(qh * qh, axis=-1, keepdims=True) + EPS) * (DN_HD ** -0.5)
            kh = kh * lax.rsqrt(jnp.sum(kh * kh, axis=-1, keepdims=True) + EPS)
            bcol = beta[:, lb:lb + 1]
            decays.append(jnp.exp(jnp.where(incl, gcs[:, la:la + 1] - gcs_t[DN_H + h:DN_H + h + 1, :], -jnp.inf)))
            kb = kh * bcol
            qhb.append(qh.astype(BF16))
            qg.append((qh * eg[:, la:la + 1]).astype(BF16))
            khb.append(kh.astype(BF16))
            kd.append((kh * ekd[:, la:la + 1]).astype(BF16))
            kbs.append(kb.astype(BF16))
            kes.append((kb * eg[:, la:la + 1]).astype(BF16))
            vbs.append((vh * bcol).astype(BF16))
            d0s.append(dec0[:, la:la + 1])
            d1s.append(dec1[:, la:la + 1])
    decay = jnp.stack(decays)
    khb = jnp.stack(khb)
    kd = jnp.stack(kd)
    qg = jnp.stack(qg)
    nm = jnp.where(strict, _bdot_nt(jnp.stack(kbs), khb) * decay, 0.0)
    p = -nm
    inv = eye + p
    pb = p.astype(BF16)
    for _ in range(5):
        p = _bdot(pb, pb)
        pb = p.astype(BF16)
        inv = inv + _bdot(inv.astype(BF16), pb)
    uw = _bdot(inv.astype(BF16), jnp.concatenate([jnp.stack(vbs), jnp.stack(kes)], axis=2))
    u = uw[:, :, :DN_HD]
    w = uw[:, :, DN_HD:].astype(BF16)
    qk = (_bdot_nt(jnp.stack(qhb), khb) * decay).astype(BF16)
    dec = (jnp.stack(d0s), jnp.stack(d1s))
    st = states
    zeros_c = jnp.zeros((nseq * DN_H, DN_C, DN_HD), F32)
    o_parts = [None, None]
    for ci in ((1, 0) if reverse else (0, 1)):
        rows = slice(ci * DN_C, (ci + 1) * DN_C)
        ws = _bdot(jnp.concatenate([w[:, rows], qg[:, rows]], axis=1), st.astype(BF16))
        vnew = u[:, rows] - ws[:, :DN_C]
        vpad = jnp.concatenate([vnew, zeros_c] if ci == 0 else [zeros_c, vnew], axis=1).astype(BF16)
        o_parts[ci] = ws[:, DN_C:] + _bdot(qk[:, rows], vpad)
        st = st * dec[ci] + _bdot_tn(kd[:, rows], vnew.astype(BF16))
    o = jnp.concatenate(o_parts, axis=1)
    outs = [jnp.concatenate([o[b * DN_H + h] for h in range(DN_H)], axis=1) for b in range(nseq)]
    return outs, st


def _dn_kernel(qf_ref, baf_ref, qb_ref, bab_ref, dtb_ref, alog_ref, of_ref, ob_ref, s_ref):
    @pl.when(pl.program_id(0) == 0)
    def _():
        s_ref[...] = jnp.zeros_like(s_ref)

    nseq = qf_ref.shape[0]
    for d, (q_ref, ba_ref, o_ref) in enumerate(((qf_ref, baf_ref, of_ref), (qb_ref, bab_ref, ob_ref))):
        outs, st = _dn_group([q_ref[b] for b in range(nseq)], [ba_ref[b] for b in range(nseq)],
                             dtb_ref[d:d + 1, :], alog_ref[d:d + 1, :], s_ref[d], bool(d))
        for b in range(nseq):
            o_ref[b] = outs[b]
        s_ref[d] = st


def _dn(qkv, ba, dtb, alog, batch, seq_len):
    nb = seq_len // DN_BLK
    qkv3 = qkv.reshape(batch, seq_len, DN_CONV_DIM)
    ba3 = ba.reshape(batch, seq_len, LANES)
    fwd = lambda c: (0, c, 0)
    bwd = lambda c: (0, nb - 1 - c, 0)
    const = lambda c: (0, 0)
    of, ob = pl.pallas_call(
        _dn_kernel,
        name="dn",
        grid=(nb,),
        in_specs=[pl.BlockSpec((batch, DN_BLK, DN_CONV_DIM), fwd), pl.BlockSpec((batch, DN_BLK, LANES), fwd),
                  pl.BlockSpec((batch, DN_BLK, DN_CONV_DIM), bwd), pl.BlockSpec((batch, DN_BLK, LANES), bwd),
                  pl.BlockSpec((2, LANES), const), pl.BlockSpec((2, LANES), const)],
        out_specs=[pl.BlockSpec((batch, DN_BLK, DN_W), fwd), pl.BlockSpec((batch, DN_BLK, DN_W), bwd)],
        out_shape=[jax.ShapeDtypeStruct((batch, seq_len, DN_W), F32)] * 2,
        scratch_shapes=[pltpu.VMEM((2, batch * DN_H, DN_HD, DN_HD), F32)],
        compiler_params=_params(("arbitrary",)),
    )(qkv3, ba3, qkv3, ba3, dtb, alog)
    t = batch * seq_len
    return of.reshape(t, DN_W), ob.reshape(t, DN_W)


def _rope_tables(seq_len):
    rows = seq_len // GRID_W
    row_idx = jnp.repeat(jnp.arange(rows, dtype=F32), GRID_W)
    col_idx = jnp.tile(jnp.arange(GRID_W, dtype=F32), rows)
    axis_dim = ATT_HD // 2
    inv_freq = jnp.power(ROPE_THETA, -jnp.arange(0, axis_dim, 2, dtype=F32) / axis_dim)
    ra = row_idx[:, None] * inv_freq
    ca = col_idx[:, None] * inv_freq
    cos_t = jnp.concatenate([jnp.cos(ra), jnp.cos(ra), jnp.cos(ca), jnp.cos(ca)], axis=1)
    sin_t = jnp.concatenate([-jnp.sin(ra), jnp.sin(ra), -jnp.sin(ca), jnp.sin(ca)], axis=1)
    return cos_t, sin_t


def _qkprep_kernel(q_ref, k_ref, cos_ref, sin_ref, qn_ref, kn_ref, qo_ref, ko_ref):
    cos_t = cos_ref[...]
    sin_t = sin_ref[...]
    lane = lax.broadcasted_iota(I32, cos_t.shape, 1)
    low = (lane & (ATT_HD // 4)) == 0

    def prep(x, nw, scale):
        x = _rms(x.astype(F32)) * nw
        partner = jnp.where(low, pltpu.roll(x, ATT_HD - ATT_HD // 4, 1), pltpu.roll(x, ATT_HD // 4, 1))
        return (x * cos_t + partner * sin_t) * scale

    qscale = ATT_HD ** -0.5 * float(np.log2(np.e))
    qs = [prep(q_ref[:, h * ATT_HD:(h + 1) * ATT_HD], qn_ref[...], qscale) for h in range(ATT_H)]
    ks = [prep(k_ref[:, h * ATT_HD:(h + 1) * ATT_HD], kn_ref[...], 1.0) for h in range(ATT_KV)]
    qo_ref[...] = jnp.concatenate(qs, axis=1).astype(qo_ref.dtype)
    ko_ref[...] = jnp.concatenate(ks, axis=1).astype(ko_ref.dtype)


def _qkprep(q, k, cos_t, sin_t, qn, kn, seq_len):
    t = q.shape[0]
    tm = min(512, seq_len)
    per_seq = seq_len // tm
    return pl.pallas_call(
        _qkprep_kernel,
        name="qkprep",
        grid=(t // tm,),
        in_specs=[pl.BlockSpec((tm, ATT_W), lambda i: (i, 0)),
                  pl.BlockSpec((tm, ATT_KV_W), lambda i: (i, 0)),
                  pl.BlockSpec((tm, ATT_HD), lambda i: (i % per_seq, 0)),
                  pl.BlockSpec((tm, ATT_HD), lambda i: (i % per_seq, 0)),
                  pl.BlockSpec((1, ATT_HD), lambda i: (0, 0)),
                  pl.BlockSpec((1, ATT_HD), lambda i: (0, 0))],
        out_specs=[pl.BlockSpec((tm, ATT_W), lambda i: (i, 0)),
                   pl.BlockSpec((tm, ATT_KV_W), lambda i: (i, 0))],
        out_shape=[jax.ShapeDtypeStruct((t, ATT_W), BF16), jax.ShapeDtypeStruct((t, ATT_KV_W), BF16)],
        compiler_params=_params(("arbitrary",)),
    )(q, k, cos_t, sin_t, qn.reshape(1, ATT_HD), kn.reshape(1, ATT_HD))


def _flash_kernel(q_ref, k_ref, v_ref, o_ref, *, tk):
    tq = q_ref.shape[0]
    nk = k_ref.shape[0] // tk
    q2 = jnp.concatenate([q_ref[:, :ATT_HD], q_ref[:, ATT_HD:]], axis=0)
    m = l = acc = None
    for j in range(nk):
        sc = _dot_nt(q2, k_ref[j * tk:(j + 1) * tk, :])
        mx = jnp.max(sc, axis=-1, keepdims=True)
        if j == 0:
            m = mx
            p = jnp.exp2(sc - m)
            l = jnp.sum(p, axis=-1, keepdims=True)
            acc = _dot(p.astype(BF16), v_ref[j * tk:(j + 1) * tk, :])
        else:
            m_new = jnp.maximum(m, mx)
            p = jnp.exp2(sc - m_new)
            alpha = jnp.exp2(m - m_new)
            l = alpha * l + jnp.sum(p, axis=-1, keepdims=True)
            acc = alpha * acc + _dot(p.astype(BF16), v_ref[j * tk:(j + 1) * tk, :])
            m = m_new
    out = acc / l
    o_ref[...] = jnp.concatenate([out[:tq], out[tq:]], axis=1).astype(o_ref.dtype)


def _flash(qh, kh, v, batch, seq_len):
    t = qh.shape[0]
    tq = min(256, seq_len)
    tk = min(256, seq_len)
    nq = seq_len // tq
    rep_w = (ATT_H // ATT_KV) * ATT_HD
    return pl.pallas_call(
        functools.partial(_flash_kernel, tk=tk),
        name="flash",
        grid=(batch, ATT_KV, nq),
        in_specs=[pl.BlockSpec((tq, rep_w), lambda b, g, i: (b * nq + i, g)),
                  pl.BlockSpec((seq_len, ATT_HD), lambda b, g, i: (b, g)),
                  pl.BlockSpec((seq_len, ATT_HD), lambda b, g, i: (b, g))],
        out_specs=pl.BlockSpec((tq, rep_w), lambda b, g, i: (b * nq + i, g)),
        out_shape=jax.ShapeDtypeStruct((t, ATT_W), BF16),
        compiler_params=_params(("arbitrary",) * 3),
    )(qh, kh, v)


def _outproj_kernel(sf_ref, sb_ref, xs_ref, zs_ref, dsk_ref, nws_ref, of_ref, ob_ref, zd_ref, ya_ref, x_ref, mod_ref,
                    nwd_ref, nwp_ref, nwf_ref, wo_ref, wr_ref, x1_ref, h2_ref, aff_ref):
    y = sf_ref[...] + sb_ref[...] + dsk_ref[...] * xs_ref[...].astype(F32)
    y = y * _silu(zs_ref[...].astype(F32))
    half = SSD_W // SSD_G
    nws = nws_ref[...]
    ys = jnp.concatenate([_rms(y[:, g * half:(g + 1) * half]) * nws[:, g * half:(g + 1) * half]
                          for g in range(SSD_G)], axis=1).astype(BF16)
    yd = []
    for h in range(DN_H):
        cols = slice(h * DN_HD, (h + 1) * DN_HD)
        o = _rms(of_ref[:, cols] + ob_ref[:, cols]) * nwd_ref[...]
        yd.append((o * _silu(zd_ref[:, cols].astype(F32))).astype(BF16))
    yd = jnp.concatenate(yd, axis=1)
    m = (_dot(ys, wo_ref[0:SSD_W, :]) + _dot(yd, wo_ref[SSD_W:SSD_W + DN_W, :])
         + _dot(ya_ref[...], wo_ref[SSD_W + DN_W:, :]))
    mod = mod_ref[0]
    x1 = x_ref[...] + mod[2:3, :] * (_rms(m) * nwp_ref[...])
    x1_ref[...] = x1
    h2 = _rms(x1) * nwf_ref[...] * (1.0 + mod[4:5, :]) + mod[3:4, :]
    h2_hi = h2.astype(BF16)
    h2_ref[...] = h2_hi
    h2_lo = (h2 - h2_hi.astype(F32)).astype(BF16)
    logits = _dot(h2_hi, wr_ref[0]) + _dot(h2_lo, wr_ref[0]) + _dot(h2_hi, wr_ref[1])
    lane = lax.broadcasted_iota(I32, logits.shape, 1)
    logits = jnp.where(lane < N_EXPERTS, logits, -jnp.inf)
    ex = jnp.exp(logits - jnp.max(logits, axis=-1, keepdims=True))
    aff = ex / jnp.sum(ex, axis=-1, keepdims=True)
    aff_ref[...] = aff.T[0:N_EXPERTS, :]


def _outproj(sf, sb, xact, zs, dsk, nws, of, ob, zd, ya, x, mod, nwd, nwp, nwf, wo_bf, wr_p, seq_len):
    t = x.shape[0]
    tm = min(512, seq_len)
    per_seq = seq_len // tm
    row = lambda i: (i, 0)
    const = lambda i: (0, 0)
    return pl.pallas_call(
        _outproj_kernel,
        name="outproj",
        grid=(t // tm,),
        in_specs=[pl.BlockSpec((tm, SSD_W), row), pl.BlockSpec((tm, SSD_W), row),
                  pl.BlockSpec((tm, SSD_W), row),
                  pl.BlockSpec((tm, SSD_W), row), pl.BlockSpec((1, SSD_W), const), pl.BlockSpec((1, SSD_W), const),
                  pl.BlockSpec((tm, DN_W), row), pl.BlockSpec((tm, DN_W), row),
                  pl.BlockSpec((tm, DN_W), row), pl.BlockSpec((tm, ATT_W), row),
                  pl.BlockSpec((tm, D_MODEL), row),
                  pl.BlockSpec((1, 6, D_MODEL), lambda i: (i // per_seq, 0, 0)),
                  pl.BlockSpec((1, DN_HD), const),
                  pl.BlockSpec((1, D_MODEL), const), pl.BlockSpec((1, D_MODEL), const),
                  pl.BlockSpec((2 * D_MODEL, D_MODEL), const),
                  pl.BlockSpec((2, D_MODEL, LANES), lambda i: (0, 0, 0))],
        out_specs=[pl.BlockSpec((tm, D_MODEL), row), pl.BlockSpec((tm, D_MODEL), row),
                   pl.BlockSpec((N_EXPERTS, tm), lambda i: (0, i))],
        out_shape=[jax.ShapeDtypeStruct((t, D_MODEL), F32), jax.ShapeDtypeStruct((t, D_MODEL), BF16),
                   jax.ShapeDtypeStruct((N_EXPERTS, t), F32)],
        compiler_params=_params(("arbitrary",)),
    )(sf, sb, xact, zs, dsk, nws, of, ob, zd, ya, x, mod, nwd, nwp.reshape(1, D_MODEL), nwf.reshape(1, D_MODEL),
      wo_bf, wr_p)


def _route_kernel(aff_ref, pos_ref, offs_ref, *, cap):
    ne, t = aff_ref.shape
    nb = t // LANES
    capf = float(cap)

    def bits_of(x):
        return pltpu.bitcast(x, I32)

    def bis(i, thr):
        cand = thr | jnp.left_shift(jnp.int32(1), 30 - i)
        cnt = jnp.sum(jnp.where(bits_of(aff_ref[...]) >= cand, 1.0, 0.0), axis=1, keepdims=True)
        return jnp.where(cnt >= capf, cand, thr)

    thr = lax.fori_loop(0, 31, bis, jnp.zeros((ne, 1), I32))
    n_gt = jnp.sum(jnp.where(bits_of(aff_ref[...]) > thr, 1.0, 0.0), axis=1, keepdims=True)
    need_eq = capf - n_gt

    r = lax.broadcasted_iota(I32, (LANES, LANES), 0)
    s = lax.broadcasted_iota(I32, (LANES, LANES), 1)
    triu = jnp.where(r <= s, 1.0, 0.0).astype(BF16)
    lane_nb = lax.broadcasted_iota(I32, (ne, nb), 1)

    offs_ref[...] = jnp.zeros_like(offs_ref)

    def tile(i, carry):
        run_sel, run_eq = carry
        start = pl.multiple_of(i * LANES, LANES)
        b = bits_of(aff_ref[:, pl.ds(start, LANES)])
        gt = b > thr
        eq = jnp.where(b == thr, 1.0, 0.0)
        eq_rank = _dot(eq.astype(BF16), triu) - eq + run_eq
        sel = jnp.where(gt | ((eq > 0.0) & (eq_rank < need_eq)), 1.0, 0.0)
        pos = _dot(sel.astype(BF16), triu) - sel + run_sel
        pos_ref[:, pl.ds(start, LANES)] = jnp.where(sel > 0.0, pos, -1.0).astype(I32)
        offs_ref[...] = jnp.where(lane_nb == i, run_sel.astype(I32), offs_ref[...])
        return (run_sel + jnp.sum(sel, axis=1, keepdims=True),
                run_eq + jnp.sum(eq, axis=1, keepdims=True))

    zero = jnp.zeros((ne, 1), F32)
    lax.fori_loop(0, nb, tile, (zero, zero))


def _route(aff_t, cap):
    ne, t = aff_t.shape
    nb = t // LANES
    return pl.pallas_call(
        functools.partial(_route_kernel, cap=cap),
        name="route",
        out_shape=[jax.ShapeDtypeStruct((ne, t), I32), jax.ShapeDtypeStruct((ne, nb), I32)],
        compiler_params=pltpu.CompilerParams(vmem_limit_bytes=VMEM_LIMIT),
    )(aff_t)


MOE_SUB = 128
MOE_ALIGN = 16
MOE_WIN_SMALL = 32 + MOE_ALIGN
MOE_WIN_FULL = MOE_SUB + MOE_ALIGN
MOE_TILE = 288
MOE_LAST_TILES = (MOE_TILE // 2, MOE_TILE - 32, MOE_TILE)
MOE_FC = 512
MOE_BLOCK = 2048


def _moe_kernel(offs_ref, h_ref, pos_ref, gate_ref, wg_ref, wu_ref, wd_ref, o_ref,
                xe_ref, ye_ref, gb_ref, *, cap, nsub):
    sb = pl.program_id(0)
    e = pl.program_id(1)
    nsb = pl.num_programs(0)
    nblk = nsb * nsub
    start = offs_ref[e, sb * nsub]
    end = jnp.where(sb == nsb - 1, cap, offs_ref[e, jnp.minimum((sb + 1) * nsub, nblk - 1)])
    count = end - start
    nfull = count // MOE_TILE

    @pl.when((sb == 0) & (e == 0))
    def _():
        xe_ref[...] = jnp.zeros_like(xe_ref)
        ye_ref[...] = jnp.zeros_like(ye_ref)
        gb_ref[...] = jnp.zeros_like(gb_ref)

    @pl.when(e == 0)
    def _():
        o_ref[...] = jnp.zeros_like(o_ref)

    def window(j):
        lo = offs_ref[e, sb * nsub + j] - start
        hi = (offs_ref[e, sb * nsub + j + 1] - start) if j + 1 < nsub else count
        a = pl.multiple_of((lo // MOE_ALIGN) * MOE_ALIGN, MOE_ALIGN)
        return lo, hi, a

    wins = [window(j) for j in range(nsub)]
    all_small = functools.reduce(jnp.logical_and, [hi - a <= MOE_WIN_SMALL for (_, hi, a) in wins])

    def onehots(win):
        rid = lax.broadcasted_iota(I32, (win, MOE_SUB), 0)
        out = []
        for j, (_, _, a) in enumerate(wins):
            posrow = pos_ref[0, :, j * MOE_SUB:(j + 1) * MOE_SUB]
            out.append((posrow - (start + a) == rid) & (posrow >= 0))
        return out, jnp.stack([jnp.where(oh, 1.0, 0.0).astype(BF16) for oh in out])

    def for_window_size(body):
        @pl.when(all_small)
        def _():
            body(MOE_WIN_SMALL)

        @pl.when(jnp.logical_not(all_small))
        def _():
            body(MOE_WIN_FULL)

    def gather(win):
        ohs, oh_b = onehots(win)
        comp = _bdot(oh_b, h_ref[...].reshape(nsub, MOE_SUB, D_MODEL))
        rid = lax.broadcasted_iota(I32, (MOE_ALIGN, 1), 0)
        for j, (lo, _, a) in enumerate(wins):
            gate = jnp.sum(jnp.where(ohs[j], gate_ref[0, :, j * MOE_SUB:(j + 1) * MOE_SUB], 0.0),
                           axis=1, keepdims=True)
            gate = jnp.broadcast_to(gate, (win, LANES))
            keep = rid < (lo - a)
            a1 = pl.multiple_of(a + MOE_ALIGN, MOE_ALIGN)
            xe_ref[pl.ds(a, MOE_ALIGN), :] = jnp.where(keep, xe_ref[pl.ds(a, MOE_ALIGN), :].astype(F32),
                                                       comp[j, :MOE_ALIGN]).astype(BF16)
            xe_ref[pl.ds(a1, win - MOE_ALIGN), :] = comp[j, MOE_ALIGN:].astype(BF16)
            gb_ref[pl.ds(a, MOE_ALIGN), :] = jnp.where(keep, gb_ref[pl.ds(a, MOE_ALIGN), :], gate[:MOE_ALIGN])
            gb_ref[pl.ds(a1, win - MOE_ALIGN), :] = gate[MOE_ALIGN:]

    def scatter(win):
        _, oh_b = onehots(win)
        yw = jnp.stack([(ye_ref[pl.ds(a, win), :] * gb_ref[pl.ds(a, win), 0:1]).astype(BF16) for (_, _, a) in wins])
        o_ref[...] += _bdot_tn(oh_b, yw).reshape(nsub * MOE_SUB, D_MODEL)

    for_window_size(gather)

    nfc = wg_ref.shape[3] // MOE_FC

    def ffn_rows(r0, rows):
        x = xe_ref[pl.ds(r0, rows), :]
        y = None
        for fc in range(nfc):
            cols = slice(fc * MOE_FC, (fc + 1) * MOE_FC)
            hid = (_silu(_dot(x, wg_ref[0, 0, :, cols])) * _dot(x, wu_ref[0, 0, :, cols])).astype(BF16)
            part = _dot(hid, wd_ref[0, 0, cols, :])
            y = part if y is None else y + part
        ye_ref[pl.ds(r0, rows), :] = y

    def ffn_tile(i, carry):
        ffn_rows(pl.multiple_of(i * MOE_TILE, MOE_ALIGN), MOE_TILE)
        return carry

    lax.fori_loop(0, nfull, ffn_tile, 0)
    rem = count - nfull * MOE_TILE
    rem0 = pl.multiple_of(nfull * MOE_TILE, MOE_ALIGN)

    lower = 0
    for height in MOE_LAST_TILES:
        @pl.when((rem > lower) & (rem <= height))
        def _():
            ffn_rows(rem0, height)
        lower = height

    for_window_size(scatter)


def _moe(offs, h2, pos, aff_t, wg, wu, wd, layer, cap, sblk):
    t = h2.shape[0]
    nsb = t // sblk
    nsub = sblk // MOE_SUB
    rows = sblk + MOE_TILE + MOE_WIN_FULL
    once = pl.Buffered(1)
    grid_spec = pltpu.PrefetchScalarGridSpec(
        num_scalar_prefetch=1,
        grid=(nsb, N_EXPERTS),
        in_specs=[pl.BlockSpec((sblk, D_MODEL), lambda s, e, o: (s, 0), pipeline_mode=once),
                  pl.BlockSpec((1, 1, sblk), lambda s, e, o: (e, 0, s)),
                  pl.BlockSpec((1, 1, sblk), lambda s, e, o: (e, 0, s)),
                  pl.BlockSpec((1, 1, D_MODEL, EXPERT_FF), lambda s, e, o: (layer, e, 0, 0)),
                  pl.BlockSpec((1, 1, D_MODEL, EXPERT_FF), lambda s, e, o: (layer, e, 0, 0)),
                  pl.BlockSpec((1, 1, EXPERT_FF, D_MODEL), lambda s, e, o: (layer, e, 0, 0))],
        out_specs=pl.BlockSpec((sblk, D_MODEL), lambda s, e, o: (s, 0), pipeline_mode=once),
        scratch_shapes=[pltpu.VMEM((rows, D_MODEL), BF16), pltpu.VMEM((rows, D_MODEL), F32),
                        pltpu.VMEM((rows, LANES), F32)],
    )
    return pl.pallas_call(
        functools.partial(_moe_kernel, cap=cap, nsub=nsub),
        name="moe",
        grid_spec=grid_spec,
        out_shape=jax.ShapeDtypeStruct((t, D_MODEL), F32),
        compiler_params=pltpu.CompilerParams(dimension_semantics=("arbitrary",) * 2,
                                             vmem_limit_bytes=MOE_VMEM_LIMIT),
    )(offs, h2, pos.reshape(N_EXPERTS, 1, t), aff_t.reshape(N_EXPERTS, 1, t), wg, wu, wd)


def _post_kernel(x_ref, f_ref, mod_ref, nw_ref, o_ref):
    mod = mod_ref[0]
    o_ref[...] = x_ref[...] + mod[5:6, :] * (_rms(f_ref[...]) * nw_ref[...])


def _post(x1, f, mod, nw, seq_len):
    t = x1.shape[0]
    tm = min(512, seq_len)
    per_seq = seq_len // tm
    row = lambda i: (i, 0)
    return pl.pallas_call(
        _post_kernel,
        name="post",
        grid=(t // tm,),
        in_specs=[pl.BlockSpec((tm, D_MODEL), row), pl.BlockSpec((tm, D_MODEL), row),
                  pl.BlockSpec((1, 6, D_MODEL), lambda i: (i // per_seq, 0, 0)),
                  pl.BlockSpec((1, D_MODEL), lambda i: (0, 0))],
        out_specs=pl.BlockSpec((tm, D_MODEL), row),
        out_shape=jax.ShapeDtypeStruct((t, D_MODEL), F32),
        compiler_params=_params(("arbitrary",)),
    )(x1, f, mod, nw.reshape(1, D_MODEL))


def _pad_lanes(v, offset=0):
    out = jnp.zeros((1, LANES), F32)
    return out.at[0, offset:offset + v.shape[0]].set(v.astype(F32))


def _split_bf16(w):
    hi = w.astype(BF16)
    return jnp.stack([hi, (w - hi.astype(F32)).astype(BF16)])


def _prep_layer(p, l):
    w_in = p["w_in"][l]
    cols, off = [], 0
    for (_, width, stored, _) in _IN_SEGS:
        seg = w_in[:, off:off + width]
        if stored != width:
            seg = jnp.pad(seg, ((0, 0), (0, stored - width)))
        cols.append(seg)
        off += width
    q = {
        "w_ada": p["w_ada"][l].astype(BF16), "b_ada": p["b_ada"][l],
        "w_in": jnp.concatenate(cols, axis=1).astype(BF16),
        "w_out": p["w_out"][l].astype(BF16),
        "w_router": _split_bf16(jnp.pad(p["w_router"][l], ((0, 0), (0, LANES - N_EXPERTS)))),
        "ssd_bias": jnp.concatenate([_pad_lanes(p["ssd_dt_bias"][l, d]) for d in range(2)], axis=0),
        "ssd_alog": jnp.concatenate([_pad_lanes(p["ssd_a_log"][l, d]) for d in range(2)], axis=0),
        "ssd_d": jnp.repeat(p["ssd_d"][l], SSD_P).reshape(1, SSD_W),
        "ssd_norm": p["ssd_norm"][l].reshape(1, SSD_W),
        "dn_bias": jnp.concatenate([_pad_lanes(p["dn_dt_bias"][l, d], DN_H) for d in range(2)], axis=0),
        "dn_alog": jnp.concatenate([_pad_lanes(p["dn_a_log"][l, d], DN_H) for d in range(2)], axis=0),
        "dn_norm": p["dn_norm"][l].reshape(1, DN_HD),
    }
    q["conv_w"] = jnp.stack([jnp.pad(p[name][l], ((0, SUBLANES - CONV_K), (0, 0))) for name in ("conv_ssd_w", "conv_dn_w")])
    q["conv_b"] = jnp.stack([p["conv_ssd_b"][l], jnp.zeros((DN_CONV_DIM,), F32)])
    for name in ("norm_mix_pre", "norm_mix_post", "norm_ffn_pre", "norm_ffn_post", "q_norm", "k_norm"):
        q[name] = p[name][l]
    return q


def _trunk(x, c, layers, experts, sblk):
    batch, seq_len, _ = x.shape
    t = batch * seq_len
    cap = EC_CAPACITY * t // N_EXPERTS
    cos_t, sin_t = _rope_tables(seq_len)
    x = x.reshape(t, D_MODEL)
    for l, q in enumerate(layers):
        mod = _ada(c, q["w_ada"], q["b_ada"])
        z_ssd, xact, dtp, qkvact, z_dn, ba_dn, qa, ka, va = _inproj(x, mod, q["norm_mix_pre"], q["w_in"],
                                                                    q["conv_w"], q["conv_b"], seq_len)
        s_f, s_b = _ssd(xact, dtp, q["ssd_bias"], q["ssd_alog"], batch, seq_len)
        o_f, o_b = _dn(qkvact, ba_dn, q["dn_bias"], q["dn_alog"], batch, seq_len)
        qh, kh = _qkprep(qa, ka, cos_t, sin_t, q["q_norm"], q["k_norm"], seq_len)
        y_att = _flash(qh, kh, va, batch, seq_len)
        x1, h2, aff_t = _outproj(s_f, s_b, xact, z_ssd, q["ssd_d"], q["ssd_norm"], o_f, o_b, z_dn, y_att, x, mod,
                                 q["dn_norm"], q["norm_mix_post"], q["norm_ffn_pre"], q["w_out"], q["w_router"],
                                 seq_len)
        pos, offs = _route(aff_t, cap)
        f = _moe(offs, h2, pos, aff_t, *experts, l, cap, sblk)
        x = _post(x1, f, mod, q["norm_ffn_post"], seq_len)
    return x.reshape(batch, seq_len, D_MODEL)


def kernel(x_prompt, x_sample, c_prompt, c_sample, w_ada, b_ada, norm_mix_pre, norm_mix_post, w_in, conv_ssd_w, conv_ssd_b, ssd_dt_bias, ssd_a_log, ssd_d, ssd_norm, conv_dn_w, dn_dt_bias, dn_a_log, dn_norm, q_norm, k_norm, w_out, norm_ffn_pre, norm_ffn_post, w_router, w_gate, w_up, w_down):
    p = dict(w_ada=w_ada, b_ada=b_ada, norm_mix_pre=norm_mix_pre, norm_mix_post=norm_mix_post,
             w_in=w_in, conv_ssd_w=conv_ssd_w, conv_ssd_b=conv_ssd_b, ssd_dt_bias=ssd_dt_bias,
             ssd_a_log=ssd_a_log, ssd_d=ssd_d, ssd_norm=ssd_norm, conv_dn_w=conv_dn_w,
             dn_dt_bias=dn_dt_bias, dn_a_log=dn_a_log, dn_norm=dn_norm, q_norm=q_norm, k_norm=k_norm,
             w_out=w_out, norm_ffn_pre=norm_ffn_pre, norm_ffn_post=norm_ffn_post,
             w_router=w_router, w_gate=w_gate, w_up=w_up, w_down=w_down)
    layers = [_prep_layer(p, l) for l in range(w_in.shape[0])]
    experts = (w_gate.astype(BF16), w_up.astype(BF16), w_down.astype(BF16))
    y_prompt = _trunk(x_prompt, c_prompt, layers, experts, MOE_BLOCK)
    y_sample = _trunk(x_sample, c_sample, layers, experts, MOE_BLOCK)
    return (y_prompt, y_sample)
```

```python
import functools

import jax
import jax.numpy as jnp
import numpy as np
from jax import lax
from jax.experimental import pallas as pl
from jax.experimental.pallas import tpu as pltpu

F32 = jnp.float32
BF16 = jnp.bfloat16
I32 = jnp.int32

D_MODEL = 1024
SSD_W = 1024
SSD_P = 64
SSD_H = 16
SSD_G = 2
SSD_N = 128
SSD_CONV_DIM = SSD_W + 2 * SSD_G * SSD_N
DN_W = 512
DN_HD = 128
DN_H = 4
DN_CONV_DIM = 3 * DN_W
ATT_W = 512
ATT_HD = 128
ATT_H = 4
ATT_KV = 2
ATT_KV_W = ATT_KV * ATT_HD
GRID_W = 64
ROPE_THETA = 10000.0
N_EXPERTS = 16
EC_CAPACITY = 2
EXPERT_FF = 2048
EPS = 1e-6
CONV_K = 5

LANES = 128
SUBLANES = 8
VMEM_LIMIT = 56 * 1024 * 1024
MOE_VMEM_LIMIT = 60 * 1024 * 1024

_IN_SEGS = (
    ("z_ssd", SSD_W, SSD_W, BF16),
    ("xbc", SSD_CONV_DIM, SSD_CONV_DIM, BF16),
    ("dt", SSD_H, LANES, F32),
    ("qkv_dn", DN_CONV_DIM, DN_CONV_DIM, BF16),
    ("z_dn", DN_W, DN_W, BF16),
    ("ba_dn", 2 * DN_H, LANES, F32),
    ("q", ATT_W, ATT_W, BF16),
    ("k", ATT_KV_W, ATT_KV_W, BF16),
    ("v", ATT_KV_W, ATT_KV_W, BF16),
)


def _params(sem):
    return pltpu.CompilerParams(dimension_semantics=sem, vmem_limit_bytes=VMEM_LIMIT)


def _sigmoid(x):
    return 1.0 / (1.0 + jnp.exp(-x))


def _silu(x):
    return x * _sigmoid(x)


def _softplus(x):
    return jnp.maximum(x, 0.0) + jnp.log(1.0 + jnp.exp(-jnp.abs(x)))


def _rms(x):
    return x * lax.rsqrt(jnp.mean(x * x, axis=-1, keepdims=True) + EPS)


def _dot(a, b):
    return jnp.dot(a, b, preferred_element_type=F32)


def _dot_nt(a, b):
    return lax.dot_general(a, b, (((1,), (1,)), ((), ())), preferred_element_type=F32)


def _dot_tn(a, b):
    return lax.dot_general(a, b, (((0,), (0,)), ((), ())), preferred_element_type=F32)


def _cumsum_mm(a, b, split_lhs=False):
    x = a if split_lhs else b
    hi = x.astype(BF16)
    r1 = x - hi.astype(F32)
    mid = r1.astype(BF16)
    lo = (r1 - mid.astype(F32)).astype(BF16)
    if split_lhs:
        return _dot(hi, b) + _dot(mid, b) + _dot(lo, b)
    return _dot(a, hi) + _dot(a, mid) + _dot(a, lo)


def _ada_kernel(c_ref, w_ref, b_ref, o_ref):
    o_ref[...] = _dot(_silu(c_ref[...]).astype(BF16), w_ref[...]) + b_ref[...]


def _ada(c, w_bf, b):
    nb = c.shape[0]
    rows = -(-nb // SUBLANES) * SUBLANES
    cp = jnp.zeros((rows, D_MODEL), F32).at[:nb].set(c)
    n = w_bf.shape[1]
    tn = 1024
    out = pl.pallas_call(
        _ada_kernel,
        name="ada",
        grid=(n // tn,),
        in_specs=[pl.BlockSpec((rows, D_MODEL), lambda j: (0, 0)),
                  pl.BlockSpec((D_MODEL, tn), lambda j: (0, j)),
                  pl.BlockSpec((1, tn), lambda j: (0, j))],
        out_specs=pl.BlockSpec((rows, tn), lambda j: (0, j)),
        out_shape=jax.ShapeDtypeStruct((rows, n), F32),
        compiler_params=_params(("arbitrary",)),
    )(cp, w_bf, b.reshape(1, n))
    return out[:nb].reshape(nb, 6, D_MODEL)


_HALO = SUBLANES
_CONV_SEGS = ("xbc", "qkv_dn")


def _inproj_kernel(x_ref, xp_ref, xn_ref, mod_ref, nw_ref, w_ref, cw_ref, cb_ref, *rest, per_seq):
    out_refs, bufs = rest[:len(_IN_SEGS)], rest[len(_IN_SEGS):]
    i = pl.program_id(0)
    tm = x_ref.shape[0]
    mod = mod_ref[0]

    def norm_mod(x):
        return _rms(x) * nw_ref[...] * (1.0 + mod[1:2, :]) + mod[0:1, :]

    keep_prev = jnp.where((i % per_seq) == 0, 0.0, 1.0)
    keep_next = jnp.where((i % per_seq) == per_seq - 1, 0.0, 1.0)
    hrow = lax.broadcasted_iota(I32, (2 * _HALO, 1), 0)
    halo = norm_mod(jnp.concatenate([xp_ref[...], xn_ref[...]], axis=0)) * jnp.where(hrow < _HALO, keep_prev, keep_next)
    hb = norm_mod(x_ref[...]).astype(BF16)
    hcat = jnp.concatenate([hb, halo.astype(BF16)], axis=0)
    off = 0
    ci = 0
    for o_ref, (name, _, width, _) in zip(out_refs, _IN_SEGS):
        wseg = w_ref[:, off:off + width]
        off += width
        if name not in _CONV_SEGS:
            o_ref[...] = _dot(hb, wseg).astype(o_ref.dtype)
            continue
        buf = bufs[ci]
        full = _dot(hcat, wseg)
        buf[_HALO:_HALO + tm, :] = full[0:tm]
        buf[0:_HALO, :] = full[tm:tm + _HALO]
        buf[_HALO + tm:2 * _HALO + tm, :] = full[tm + _HALO:tm + 2 * _HALO]
        acc = jnp.zeros((tm, width), F32) + cb_ref[ci:ci + 1, :]
        for k in range(CONV_K):
            lo = _HALO - CONV_K // 2 + k
            acc = acc + buf[lo:lo + tm, :] * cw_ref[ci, k:k + 1, :]
        o_ref[...] = _silu(acc).astype(o_ref.dtype)
        ci += 1


def _inproj(x, mod, nw, w_p, conv_w, conv_b, seq_len):
    t = x.shape[0]
    tm = min(512, seq_len)
    per_seq = seq_len // tm
    ntot = w_p.shape[1]
    hb = tm // _HALO
    nh = t // _HALO
    cw = conv_w.shape[2]
    out_shapes = [jax.ShapeDtypeStruct((t, sw), dt) for (_, _, sw, dt) in _IN_SEGS]
    out_specs = [pl.BlockSpec((tm, sw), lambda i: (i, 0)) for (_, _, sw, _) in _IN_SEGS]
    return pl.pallas_call(
        functools.partial(_inproj_kernel, per_seq=per_seq),
        name="inproj",
        grid=(t // tm,),
        in_specs=[pl.BlockSpec((tm, D_MODEL), lambda i: (i, 0)),
                  pl.BlockSpec((_HALO, D_MODEL), lambda i: (jnp.maximum(i * hb - 1, 0), 0)),
                  pl.BlockSpec((_HALO, D_MODEL), lambda i: (jnp.minimum((i + 1) * hb, nh - 1), 0)),
                  pl.BlockSpec((1, 6, D_MODEL), lambda i: (i // per_seq, 0, 0)),
                  pl.BlockSpec((1, D_MODEL), lambda i: (0, 0)),
                  pl.BlockSpec((D_MODEL, ntot), lambda i: (0, 0)),
                  pl.BlockSpec((len(_CONV_SEGS), SUBLANES, cw), lambda i: (0, 0, 0)),
                  pl.BlockSpec((len(_CONV_SEGS), cw), lambda i: (0, 0))],
        out_specs=out_specs,
        out_shape=out_shapes,
        scratch_shapes=[pltpu.VMEM((tm + 2 * _HALO, cw), F32) for _ in _CONV_SEGS],
        compiler_params=_params(("arbitrary",)),
    )(x, x, x, mod, nw.reshape(1, D_MODEL), w_p, conv_w, conv_b)


SSD_Q = 128


def _ssd_kernel(xf_ref, dtf_ref, xb_ref, dtb_ref, bias_ref, alog_ref, sel_ref, yf_ref, yb_ref, h_ref):
    q = SSD_Q
    dirs = (0, 1)

    @pl.when(pl.program_id(1) == 0)
    def _():
        h_ref[...] = jnp.zeros_like(h_ref)

    r = lax.broadcasted_iota(I32, (q, q), 0)
    s = lax.broadcasted_iota(I32, (q, q), 1)
    mask = (r >= s, s >= r)
    tri = [jnp.where(m, 1.0, 0.0).astype(BF16) for m in mask]
    lo = s < SSD_P
    last = (q - 1, 0)

    dtp = (dtf_ref[...], dtb_ref[...])
    dt = [_softplus(dtp[d] + bias_ref[d:d + 1, :]) for d in dirs]
    dta = [dt[d] * (-jnp.exp(alog_ref[d:d + 1, :])) for d in dirs]
    acs = [_cumsum_mm(tri[d], dta[d]) for d in dirs]
    acs_t = [acs[d].T for d in dirs]
    dt_t = [dt[d].T for d in dirs]
    eacs = [jnp.exp(acs[d]) for d in dirs]
    wcols = [jnp.exp(acs[d][last[d]:last[d] + 1, :] - acs[d]) * dt[d] for d in dirs]
    cols = jnp.concatenate([eacs[0], wcols[0], eacs[1], wcols[1]], axis=0).astype(BF16)
    spread = _dot(cols, sel_ref[...])
    eacs_w = [spread[2 * d * q:(2 * d + 1) * q] for d in dirs]
    wcols_w = [spread[(2 * d + 1) * q:(2 * d + 2) * q] for d in dirs]
    dec_w = _cumsum_mm(jnp.concatenate([eacs[d][last[d]:last[d] + 1, :] for d in dirs], axis=0), sel_ref[...],
                       split_lhs=True)

    xact = (xf_ref[...], xb_ref[...])
    ys = ([], [])
    for g in range(SSD_G):
        bm = [xact[d][:, SSD_W + g * SSD_N:SSD_W + (g + 1) * SSD_N] for d in dirs]
        cm = [xact[d][:, SSD_W + (SSD_G + g) * SSD_N:SSD_W + (SSD_G + g + 1) * SSD_N] for d in dirs]
        cb = [_dot_nt(cm[d], bm[d]) for d in dirs]
        hg = [h_ref[d, g] for d in dirs]
        cmh = [_dot(cm[d], hg[d].astype(BF16)) for d in dirs]
        xw_parts = ([], [])
        for pp in range(4):
            j0 = g * 8 + pp * 2
            lanes = slice((g * 4 + pp) * LANES, (g * 4 + pp + 1) * LANES)
            xpair = [xact[d][:, lanes].astype(F32) for d in dirs]
            ms = ([], [])
            for j in (j0, j0 + 1):
                seg = [acs[d][:, j:j + 1] - acs_t[d][j:j + 1, :] for d in dirs]
                lm = [jnp.exp(jnp.where(mask[d], seg[d], -jnp.inf)) for d in dirs]
                for d in dirs:
                    ms[d].append((cb[d] * lm[d] * dt_t[d][j:j + 1, :]).astype(BF16))
            mcat = [jnp.concatenate(ms[d], axis=1) for d in dirs]
            x2 = [jnp.concatenate([jnp.where(lo, xpair[d], 0.0), jnp.where(lo, 0.0, xpair[d])], axis=0).astype(BF16)
                  for d in dirs]
            yd = [_dot(mcat[d], x2[d]) for d in dirs]
            for d in dirs:
                ys[d].append(yd[d] + cmh[d][:, pp * LANES:(pp + 1) * LANES] * eacs_w[d][:, lanes])
                xw_parts[d].append((xpair[d] * wcols_w[d][:, lanes]).astype(BF16))
        half = slice(g * (SSD_W // SSD_G), (g + 1) * (SSD_W // SSD_G))
        upd = [_dot_tn(bm[d], jnp.concatenate(xw_parts[d], axis=1)) for d in dirs]
        for d in dirs:
            h_ref[d, g] = hg[d] * dec_w[d:d + 1, half] + upd[d]
    yf_ref[...] = jnp.concatenate(ys[0], axis=1)
    yb_ref[...] = jnp.concatenate(ys[1], axis=1)


def _ssd(xact, dtp, bias, alog, batch, seq_len):
    t = xact.shape[0]
    nc = seq_len // SSD_Q
    fwd = lambda b, c: (b * nc + c, 0)
    bwd = lambda b, c: (b * nc + nc - 1 - c, 0)
    const = lambda b, c: (0, 0)
    sel = jnp.repeat(jnp.eye(LANES, SSD_H, dtype=BF16), SSD_P, axis=1)
    return pl.pallas_call(
        _ssd_kernel,
        name="ssd",
        grid=(batch, nc),
        in_specs=[pl.BlockSpec((SSD_Q, SSD_CONV_DIM), fwd), pl.BlockSpec((SSD_Q, LANES), fwd),
                  pl.BlockSpec((SSD_Q, SSD_CONV_DIM), bwd), pl.BlockSpec((SSD_Q, LANES), bwd),
                  pl.BlockSpec((2, LANES), const), pl.BlockSpec((2, LANES), const),
                  pl.BlockSpec((LANES, SSD_W), const)],
        out_specs=[pl.BlockSpec((SSD_Q, SSD_W), fwd), pl.BlockSpec((SSD_Q, SSD_W), bwd)],
        out_shape=[jax.ShapeDtypeStruct((t, SSD_W), F32)] * 2,
        scratch_shapes=[pltpu.VMEM((2, SSD_G, SSD_N, SSD_W // SSD_G), F32)],
        compiler_params=_params(("arbitrary", "arbitrary")),
    )(xact, dtp, xact, dtp, bias, alog, sel)


DN_BLK = 128
DN_C = 64


def _dn_masks(reverse):
    n = DN_BLK
    r = lax.broadcasted_iota(I32, (n, n), 0)
    s = lax.broadcasted_iota(I32, (n, n), 1)
    same = (r >= DN_C) == (s >= DN_C)
    incl = same & ((s >= r) if reverse else (r >= s))
    strict = same & ((s > r) if reverse else (r > s))
    return r, incl, strict, jnp.where(incl, 1.0, 0.0).astype(BF16), jnp.where(r == s, 1.0, 0.0)


def _bdot(a, b):
    return lax.dot_general(a, b, (((2,), (1,)), ((0,), (0,))), preferred_element_type=F32)


def _bdot_nt(a, b):
    return lax.dot_general(a, b, (((2,), (2,)), ((0,), (0,))), preferred_element_type=F32)


def _bdot_tn(a, b):
    return lax.dot_general(a, b, (((1,), (1,)), ((0,), (0,))), preferred_element_type=F32)


def _dn_group(qkvs, bas, dtb, alog, states, reverse):
    n = DN_BLK
    nseq = len(qkvs)
    r, incl, strict, tri, eye = _dn_masks(reverse)
    ba = jnp.concatenate(bas, axis=1)
    beta = _sigmoid(ba)
    gl = -jnp.exp(jnp.concatenate([alog] * nseq, axis=1)) * _softplus(ba + jnp.concatenate([dtb] * nseq, axis=1))
    gcs = _cumsum_mm(tri, gl)
    if reverse:
        t0, t1 = gcs[0:1, :], gcs[DN_C:DN_C + 1, :]
    else:
        t0, t1 = gcs[DN_C - 1:DN_C, :], gcs[n - 1:n, :]
    eg = jnp.exp(gcs)
    ekd = jnp.exp(jnp.where(r[:, 0:1] < DN_C, t0, t1) - gcs)
    dec0 = jnp.exp(t0)
    dec1 = jnp.exp(t1)

    qhb, qg, khb, kd, kbs, kes, vbs, decays, d0s, d1s = [], [], [], [], [], [], [], [], [], []
    for b in range(nseq):
        gcs_t = gcs[:, b * LANES:(b + 1) * LANES].T
        for h in range(DN_H):
            lb = b * LANES + h
            la = lb + DN_H
            qh = qkvs[b][:, h * DN_HD:(h + 1) * DN_HD].astype(F32)
            kh = qkvs[b][:, DN_W + h * DN_HD:DN_W + (h + 1) * DN_HD].astype(F32)
            vh = qkvs[b][:, 2 * DN_W + h * DN_HD:2 * DN_W + (h + 1) * DN_HD].astype(F32)
            qh = qh * lax.rsqrt(jnp.sum(qh * qh, axis=-1, keepdims=True) + EPS) * (DN_HD ** -0.5)
            kh = kh * lax.rsqrt(jnp.sum(kh * kh, axis=-1, keepdims=True) + EPS)
            bcol = beta[:, lb:lb + 1]
            decays.append(jnp.exp(jnp.where(incl, gcs[:, la:la + 1] - gcs_t[DN_H + h:DN_H + h + 1, :], -jnp.inf)))
            kb = kh * bcol
            qhb.append(qh.astype(BF16))
            qg.append((qh * eg[:, la:la + 1]).astype(BF16))
            khb.append(kh.astype(BF16))
            kd.append((kh * ekd[:, la:la + 1]).astype(BF16))
            kbs.append(kb.astype(BF16))
            kes.append((kb * eg[:, la:la + 1]).astype(BF16))
            vbs.append((vh * bcol).astype(BF16))
            d0s.append(dec0[:, la:la + 1])
            d1s.append(dec1[:, la:la + 1])
    decay = jnp.stack(decays)
    khb = jnp.stack(khb)
    kd = jnp.stack(kd)
    qg = jnp.stack(qg)
    nm = jnp.where(strict, _bdot_nt(jnp.stack(kbs), khb) * decay, 0.0)
    p = -nm
    inv = eye + p
    pb = p.astype(BF16)
    for _ in range(5):
        p = _bdot(pb, pb)
        pb = p.astype(BF16)
        inv = inv + _bdot(inv.astype(BF16), pb)
    uw = _bdot(inv.astype(BF16), jnp.concatenate([jnp.stack(vbs), jnp.stack(kes)], axis=2))
    u = uw[:, :, :DN_HD]
    w = uw[:, :, DN_HD:].astype(BF16)
    qk = (_bdot_nt(jnp.stack(qhb), khb) * decay).astype(BF16)
    dec = (jnp.stack(d0s), jnp.stack(d1s))
    st = states
    zeros_c = jnp.zeros((nseq * DN_H, DN_C, DN_HD), F32)
    o_parts = [None, None]
    for ci in ((1, 0) if reverse else (0, 1)):
        rows = slice(ci * DN_C, (ci + 1) * DN_C)
        ws = _bdot(jnp.concatenate([w[:, rows], qg[:, rows]], axis=1), st.astype(BF16))
        vnew = u[:, rows] - ws[:, :DN_C]
        vpad = jnp.concatenate([vnew, zeros_c] if ci == 0 else [zeros_c, vnew], axis=1).astype(BF16)
        o_parts[ci] = ws[:, DN_C:] + _bdot(qk[:, rows], vpad)
        st = st * dec[ci] + _bdot_tn(kd[:, rows], vnew.astype(BF16))
    o = jnp.concatenate(o_parts, axis=1)
    outs = [jnp.concatenate([o[b * DN_H + h] for h in range(DN_H)], axis=1) for b in range(nseq)]
    return outs, st


def _dn_kernel(qf_ref, baf_ref, qb_ref, bab_ref, dtb_ref, alog_ref, of_ref, ob_ref, s_ref):
    @pl.when(pl.program_id(0) == 0)
    def _():
        s_ref[...] = jnp.zeros_like(s_ref)

    nseq = qf_ref.shape[0]
    for d, (q_ref, ba_ref, o_ref) in enumerate(((qf_ref, baf_ref, of_ref), (qb_ref, bab_ref, ob_ref))):
        outs, st = _dn_group([q_ref[b] for b in range(nseq)], [ba_ref[b] for b in range(nseq)],
                             dtb_ref[d:d + 1, :], alog_ref[d:d + 1, :], s_ref[d], bool(d))
        for b in range(nseq):
            o_ref[b] = outs[b]
        s_ref[d] = st


def _dn(qkv, ba, dtb, alog, batch, seq_len):
    nb = seq_len // DN_BLK
    qkv3 = qkv.reshape(batch, seq_len, DN_CONV_DIM)
    ba3 = ba.reshape(batch, seq_len, LANES)
    fwd = lambda c: (0, c, 0)
    bwd = lambda c: (0, nb - 1 - c, 0)
    const = lambda c: (0, 0)
    of, ob = pl.pallas_call(
        _dn_kernel,
        name="dn",
        grid=(nb,),
        in_specs=[pl.BlockSpec((batch, DN_BLK, DN_CONV_DIM), fwd), pl.BlockSpec((batch, DN_BLK, LANES), fwd),
                  pl.BlockSpec((batch, DN_BLK, DN_CONV_DIM), bwd), pl.BlockSpec((batch, DN_BLK, LANES), bwd),
                  pl.BlockSpec((2, LANES), const), pl.BlockSpec((2, LANES), const)],
        out_specs=[pl.BlockSpec((batch, DN_BLK, DN_W), fwd), pl.BlockSpec((batch, DN_BLK, DN_W), bwd)],
        out_shape=[jax.ShapeDtypeStruct((batch, seq_len, DN_W), F32)] * 2,
        scratch_shapes=[pltpu.VMEM((2, batch * DN_H, DN_HD, DN_HD), F32)],
        compiler_params=_params(("arbitrary",)),
    )(qkv3, ba3, qkv3, ba3, dtb, alog)
    t = batch * seq_len
    return of.reshape(t, DN_W), ob.reshape(t, DN_W)


def _rope_tables(seq_len):
    rows = seq_len // GRID_W
    row_idx = jnp.repeat(jnp.arange(rows, dtype=F32), GRID_W)
    col_idx = jnp.tile(jnp.arange(GRID_W, dtype=F32), rows)
    axis_dim = ATT_HD // 2
    inv_freq = jnp.power(ROPE_THETA, -jnp.arange(0, axis_dim, 2, dtype=F32) / axis_dim)
    ra = row_idx[:, None] * inv_freq
    ca = col_idx[:, None] * inv_freq
    cos_t = jnp.concatenate([jnp.cos(ra), jnp.cos(ra), jnp.cos(ca), jnp.cos(ca)], axis=1)
    sin_t = jnp.concatenate([-jnp.sin(ra), jnp.sin(ra), -jnp.sin(ca), jnp.sin(ca)], axis=1)
    return cos_t, sin_t


def _qkprep_kernel(q_ref, k_ref, cos_ref, sin_ref, qn_ref, kn_ref, qo_ref, ko_ref):
    cos_t = cos_ref[...]
    sin_t = sin_ref[...]
    lane = lax.broadcasted_iota(I32, cos_t.shape, 1)
    low = (lane & (ATT_HD // 4)) == 0

    def prep(x, nw, scale):
        x = _rms(x.astype(F32)) * nw
        partner = jnp.where(low, pltpu.roll(x, ATT_HD - ATT_HD // 4, 1), pltpu.roll(x, ATT_HD // 4, 1))
        return (x * cos_t + partner * sin_t) * scale

    qscale = ATT_HD ** -0.5 * float(np.log2(np.e))
    qs = [prep(q_ref[:, h * ATT_HD:(h + 1) * ATT_HD], qn_ref[...], qscale) for h in range(ATT_H)]
    ks = [prep(k_ref[:, h * ATT_HD:(h + 1) * ATT_HD], kn_ref[...], 1.0) for h in range(ATT_KV)]
    qo_ref[...] = jnp.concatenate(qs, axis=1).astype(qo_ref.dtype)
    ko_ref[...] = jnp.concatenate(ks, axis=1).astype(ko_ref.dtype)


def _qkprep(q, k, cos_t, sin_t, qn, kn, seq_len):
    t = q.shape[0]
    tm = min(512, seq_len)
    per_seq = seq_len // tm
    return pl.pallas_call(
        _qkprep_kernel,
        name="qkprep",
        grid=(t // tm,),
        in_specs=[pl.BlockSpec((tm, ATT_W), lambda i: (i, 0)),
                  pl.BlockSpec((tm, ATT_KV_W), lambda i: (i, 0)),
                  pl.BlockSpec((tm, ATT_HD), lambda i: (i % per_seq, 0)),
                  pl.BlockSpec((tm, ATT_HD), lambda i: (i % per_seq, 0)),
                  pl.BlockSpec((1, ATT_HD), lambda i: (0, 0)),
                  pl.BlockSpec((1, ATT_HD), lambda i: (0, 0))],
        out_specs=[pl.BlockSpec((tm, ATT_W), lambda i: (i, 0)),
                   pl.BlockSpec((tm, ATT_KV_W), lambda i: (i, 0))],
        out_shape=[jax.ShapeDtypeStruct((t, ATT_W), BF16), jax.ShapeDtypeStruct((t, ATT_KV_W), BF16)],
        compiler_params=_params(("arbitrary",)),
    )(q, k, cos_t, sin_t, qn.reshape(1, ATT_HD), kn.reshape(1, ATT_HD))


def _flash_kernel(q_ref, k_ref, v_ref, o_ref, *, tk):
    tq = q_ref.shape[0]
    nk = k_ref.shape[0] // tk
    q2 = jnp.concatenate([q_ref[:, :ATT_HD], q_ref[:, ATT_HD:]], axis=0)
    m = l = acc = None
    for j in range(nk):
        sc = _dot_nt(q2, k_ref[j * tk:(j + 1) * tk, :])
        mx = jnp.max(sc, axis=-1, keepdims=True)
        if j == 0:
            m = mx
            p = jnp.exp2(sc - m)
            l = jnp.sum(p, axis=-1, keepdims=True)
            acc = _dot(p.astype(BF16), v_ref[j * tk:(j + 1) * tk, :])
        else:
            m_new = jnp.maximum(m, mx)
            p = jnp.exp2(sc - m_new)
            alpha = jnp.exp2(m - m_new)
            l = alpha * l + jnp.sum(p, axis=-1, keepdims=True)
            acc = alpha * acc + _dot(p.astype(BF16), v_ref[j * tk:(j + 1) * tk, :])
            m = m_new
    out = acc / l
    o_ref[...] = jnp.concatenate([out[:tq], out[tq:]], axis=1).astype(o_ref.dtype)


def _flash(qh, kh, v, batch, seq_len):
    t = qh.shape[0]
    tq = min(256, seq_len)
    tk = min(256, seq_len)
    nq = seq_len // tq
    rep_w = (ATT_H // ATT_KV) * ATT_HD
    return pl.pallas_call(
        functools.partial(_flash_kernel, tk=tk),
        name="flash",
        grid=(batch, ATT_KV, nq),
        in_specs=[pl.BlockSpec((tq, rep_w), lambda b, g, i: (b * nq + i, g)),
                  pl.BlockSpec((seq_len, ATT_HD), lambda b, g, i: (b, g)),
                  pl.BlockSpec((seq_len, ATT_HD), lambda b, g, i: (b, g))],
        out_specs=pl.BlockSpec((tq, rep_w), lambda b, g, i: (b * nq + i, g)),
        out_shape=jax.ShapeDtypeStruct((t, ATT_W), BF16),
        compiler_params=_params(("arbitrary",) * 3),
    )(qh, kh, v)


def _outproj_kernel(sf_ref, sb_ref, xs_ref, zs_ref, dsk_ref, nws_ref, of_ref, ob_ref, zd_ref, ya_ref, x_ref, mod_ref,
                    nwd_ref, nwp_ref, nwf_ref, wo_ref, wr_ref, x1_ref, h2_ref, aff_ref):
    y = sf_ref[...] + sb_ref[...] + dsk_ref[...] * xs_ref[...].astype(F32)
    y = y * _silu(zs_ref[...].astype(F32))
    half = SSD_W // SSD_G
    nws = nws_ref[...]
    ys = jnp.concatenate([_rms(y[:, g * half:(g + 1) * half]) * nws[:, g * half:(g + 1) * half]
                          for g in range(SSD_G)], axis=1).astype(BF16)
    yd = []
    for h in range(DN_H):
        cols = slice(h * DN_HD, (h + 1) * DN_HD)
        o = _rms(of_ref[:, cols] + ob_ref[:, cols]) * nwd_ref[...]
        yd.append((o * _silu(zd_ref[:, cols].astype(F32))).astype(BF16))
    yd = jnp.concatenate(yd, axis=1)
    m = (_dot(ys, wo_ref[0:SSD_W, :]) + _dot(yd, wo_ref[SSD_W:SSD_W + DN_W, :])
         + _dot(ya_ref[...], wo_ref[SSD_W + DN_W:, :]))
    mod = mod_ref[0]
    x1 = x_ref[...] + mod[2:3, :] * (_rms(m) * nwp_ref[...])
    x1_ref[...] = x1
    h2 = _rms(x1) * nwf_ref[...] * (1.0 + mod[4:5, :]) + mod[3:4, :]
    h2_hi = h2.astype(BF16)
    h2_ref[...] = h2_hi
    h2_lo = (h2 - h2_hi.astype(F32)).astype(BF16)
    logits = _dot(h2_hi, wr_ref[0]) + _dot(h2_lo, wr_ref[0]) + _dot(h2_hi, wr_ref[1])
    lane = lax.broadcasted_iota(I32, logits.shape, 1)
    logits = jnp.where(lane < N_EXPERTS, logits, -jnp.inf)
    ex = jnp.exp(logits - jnp.max(logits, axis=-1, keepdims=True))
    aff = ex / jnp.sum(ex, axis=-1, keepdims=True)
    aff_ref[...] = aff.T[0:N_EXPERTS, :]


def _outproj(sf, sb, xact, zs, dsk, nws, of, ob, zd, ya, x, mod, nwd, nwp, nwf, wo_bf, wr_p, seq_len):
    t = x.shape[0]
    tm = min(512, seq_len)
    per_seq = seq_len // tm
    row = lambda i: (i, 0)
    const = lambda i: (0, 0)
    return pl.pallas_call(
        _outproj_kernel,
        name="outproj",
        grid=(t // tm,),
        in_specs=[pl.BlockSpec((tm, SSD_W), row), pl.BlockSpec((tm, SSD_W), row),
                  pl.BlockSpec((tm, SSD_W), row),
                  pl.BlockSpec((tm, SSD_W), row), pl.BlockSpec((1, SSD_W), const), pl.BlockSpec((1, SSD_W), const),
                  pl.BlockSpec((tm, DN_W), row), pl.BlockSpec((tm, DN_W), row),
                  pl.BlockSpec((tm, DN_W), row), pl.BlockSpec((tm, ATT_W), row),
                  pl.BlockSpec((tm, D_MODEL), row),
                  pl.BlockSpec((1, 6, D_MODEL), lambda i: (i // per_seq, 0, 0)),
                  pl.BlockSpec((1, DN_HD), const),
                  pl.BlockSpec((1, D_MODEL), const), pl.BlockSpec((1, D_MODEL), const),
                  pl.BlockSpec((2 * D_MODEL, D_MODEL), const),
                  pl.BlockSpec((2, D_MODEL, LANES), lambda i: (0, 0, 0))],
        out_specs=[pl.BlockSpec((tm, D_MODEL), row), pl.BlockSpec((tm, D_MODEL), row),
                   pl.BlockSpec((N_EXPERTS, tm), lambda i: (0, i))],
        out_shape=[jax.ShapeDtypeStruct((t, D_MODEL), F32), jax.ShapeDtypeStruct((t, D_MODEL), BF16),
                   jax.ShapeDtypeStruct((N_EXPERTS, t), F32)],
        compiler_params=_params(("arbitrary",)),
    )(sf, sb, xact, zs, dsk, nws, of, ob, zd, ya, x, mod, nwd, nwp.reshape(1, D_MODEL), nwf.reshape(1, D_MODEL),
      wo_bf, wr_p)


def _route_kernel(aff_ref, pos_ref, offs_ref, *, cap):
    ne, t = aff_ref.shape
    nb = t // LANES
    capf = float(cap)

    def bits_of(x):
        return pltpu.bitcast(x, I32)

    def bis(i, thr):
        cand = thr | jnp.left_shift(jnp.int32(1), 30 - i)
        cnt = jnp.sum(jnp.where(bits_of(aff_ref[...]) >= cand, 1.0, 0.0), axis=1, keepdims=True)
        return jnp.where(cnt >= capf, cand, thr)

    thr = lax.fori_loop(0, 31, bis, jnp.zeros((ne, 1), I32))
    n_gt = jnp.sum(jnp.where(bits_of(aff_ref[...]) > thr, 1.0, 0.0), axis=1, keepdims=True)
    need_eq = capf - n_gt

    r = lax.broadcasted_iota(I32, (LANES, LANES), 0)
    s = lax.broadcasted_iota(I32, (LANES, LANES), 1)
    triu = jnp.where(r <= s, 1.0, 0.0).astype(BF16)
    lane_nb = lax.broadcasted_iota(I32, (ne, nb), 1)

    offs_ref[...] = jnp.zeros_like(offs_ref)

    def tile(i, carry):
        run_sel, run_eq = carry
        start = pl.multiple_of(i * LANES, LANES)
        b = bits_of(aff_ref[:, pl.ds(start, LANES)])
        gt = b > thr
        eq = jnp.where(b == thr, 1.0, 0.0)
        eq_rank = _dot(eq.astype(BF16), triu) - eq + run_eq
        sel = jnp.where(gt | ((eq > 0.0) & (eq_rank < need_eq)), 1.0, 0.0)
        pos = _dot(sel.astype(BF16), triu) - sel + run_sel
        pos_ref[:, pl.ds(start, LANES)] = jnp.where(sel > 0.0, pos, -1.0).astype(I32)
        offs_ref[...] = jnp.where(lane_nb == i, run_sel.astype(I32), offs_ref[...])
        return (run_sel + jnp.sum(sel, axis=1, keepdims=True),
                run_eq + jnp.sum(eq, axis=1, keepdims=True))

    zero = jnp.zeros((ne, 1), F32)
    lax.fori_loop(0, nb, tile, (zero, zero))


def _route(aff_t, cap):
    ne, t = aff_t.shape
    nb = t // LANES
    return pl.pallas_call(
        functools.partial(_route_kernel, cap=cap),
        name="route",
        out_shape=[jax.ShapeDtypeStruct((ne, t), I32), jax.ShapeDtypeStruct((ne, nb), I32)],
        compiler_params=pltpu.CompilerParams(vmem_limit_bytes=VMEM_LIMIT),
    )(aff_t)


MOE_SUB = 128
MOE_ALIGN = 16
MOE_WIN_SMALL = 32 + MOE_ALIGN
MOE_WIN_FULL = MOE_SUB + MOE_ALIGN
MOE_TILE = 288
MOE_LAST_TILES = (MOE_TILE // 2, MOE_TILE - 32, MOE_TILE)
MOE_FC = 1024
MOE_BLOCK = 2048


def _moe_kernel(offs_ref, h_ref, pos_ref, gate_ref, wg_ref, wu_ref, wd_ref, o_ref,
                xe_ref, ye_ref, gb_ref, *, cap, nsub):
    sb = pl.program_id(0)
    e = pl.program_id(1)
    nsb = pl.num_programs(0)
    nblk = nsb * nsub
    start = offs_ref[e, sb * nsub]
    end = jnp.where(sb == nsb - 1, cap, offs_ref[e, jnp.minimum((sb + 1) * nsub, nblk - 1)])
    count = end - start
    nfull = count // MOE_TILE

    @pl.when((sb == 0) & (e == 0))
    def _():
        xe_ref[...] = jnp.zeros_like(xe_ref)
        ye_ref[...] = jnp.zeros_like(ye_ref)
        gb_ref[...] = jnp.zeros_like(gb_ref)

    @pl.when(e == 0)
    def _():
        o_ref[...] = jnp.zeros_like(o_ref)

    def window(j):
        lo = offs_ref[e, sb * nsub + j] - start
        hi = (offs_ref[e, sb * nsub + j + 1] - start) if j + 1 < nsub else count
        a = pl.multiple_of((lo // MOE_ALIGN) * MOE_ALIGN, MOE_ALIGN)
        return lo, hi, a

    wins = [window(j) for j in range(nsub)]
    all_small = functools.reduce(jnp.logical_and, [hi - a <= MOE_WIN_SMALL for (_, hi, a) in wins])

    def onehots(win):
        rid = lax.broadcasted_iota(I32, (win, MOE_SUB), 0)
        out = []
        for j, (_, _, a) in enumerate(wins):
            posrow = pos_ref[0, :, j * MOE_SUB:(j + 1) * MOE_SUB]
            out.append((posrow - (start + a) == rid) & (posrow >= 0))
        return out, jnp.stack([jnp.where(oh, 1.0, 0.0).astype(BF16) for oh in out])

    def for_window_size(body):
        @pl.when(all_small)
        def _():
            body(MOE_WIN_SMALL)

        @pl.when(jnp.logical_not(all_small))
        def _():
            body(MOE_WIN_FULL)

    def gather(win):
        ohs, oh_b = onehots(win)
        comp = _bdot(oh_b, h_ref[...].reshape(nsub, MOE_SUB, D_MODEL))
        rid = lax.broadcasted_iota(I32, (MOE_ALIGN, 1), 0)
        for j, (lo, _, a) in enumerate(wins):
            gate = jnp.sum(jnp.where(ohs[j], gate_ref[0, :, j * MOE_SUB:(j + 1) * MOE_SUB], 0.0),
                           axis=1, keepdims=True)
            gate = jnp.broadcast_to(gate, (win, LANES))
            keep = rid < (lo - a)
            a1 = pl.multiple_of(a + MOE_ALIGN, MOE_ALIGN)
            xe_ref[pl.ds(a, MOE_ALIGN), :] = jnp.where(keep, xe_ref[pl.ds(a, MOE_ALIGN), :].astype(F32),
                                                       comp[j, :MOE_ALIGN]).astype(BF16)
            xe_ref[pl.ds(a1, win - MOE_ALIGN), :] = comp[j, MOE_ALIGN:].astype(BF16)
            gb_ref[pl.ds(a, MOE_ALIGN), :] = jnp.where(keep, gb_ref[pl.ds(a, MOE_ALIGN), :], gate[:MOE_ALIGN])
            gb_ref[pl.ds(a1, win - MOE_ALIGN), :] = gate[MOE_ALIGN:]

    def scatter(win):
        _, oh_b = onehots(win)
        yw = jnp.stack([(ye_ref[pl.ds(a, win), :] * gb_ref[pl.ds(a, win), 0:1]).astype(BF16) for (_, _, a) in wins])
        o_ref[...] += _bdot_tn(oh_b, yw).reshape(nsub * MOE_SUB, D_MODEL)

    for_window_size(gather)

    nfc = wg_ref.shape[3] // MOE_FC

    def ffn_rows(r0, rows):
        x = xe_ref[pl.ds(r0, rows), :]
        y = None
        for fc in range(nfc):
            cols = slice(fc * MOE_FC, (fc + 1) * MOE_FC)
            hid = (_silu(_dot(x, wg_ref[0, 0, :, cols])) * _dot(x, wu_ref[0, 0, :, cols])).astype(BF16)
            part = _dot(hid, wd_ref[0, 0, cols, :])
            y = part if y is None else y + part
        ye_ref[pl.ds(r0, rows), :] = y

    def ffn_tile(i, carry):
        ffn_rows(pl.multiple_of(i * MOE_TILE, MOE_ALIGN), MOE_TILE)
        return carry

    lax.fori_loop(0, nfull, ffn_tile, 0)
    rem = count - nfull * MOE_TILE
    rem0 = pl.multiple_of(nfull * MOE_TILE, MOE_ALIGN)

    lower = 0
    for height in MOE_LAST_TILES:
        @pl.when((rem > lower) & (rem <= height))
        def _():
            ffn_rows(rem0, height)
        lower = height

    for_window_size(scatter)


def _moe(offs, h2, pos, aff_t, wg, wu, wd, layer, cap, sblk):
    t = h2.shape[0]
    nsb = t // sblk
    nsub = sblk // MOE_SUB
    rows = sblk + MOE_TILE + MOE_WIN_FULL
    once = pl.Buffered(1)
    grid_spec = pltpu.PrefetchScalarGridSpec(
        num_scalar_prefetch=1,
        grid=(nsb, N_EXPERTS),
        in_specs=[pl.BlockSpec((sblk, D_MODEL), lambda s, e, o: (s, 0), pipeline_mode=once),
                  pl.BlockSpec((1, 1, sblk), lambda s, e, o: (e, 0, s)),
                  pl.BlockSpec((1, 1, sblk), lambda s, e, o: (e, 0, s)),
                  pl.BlockSpec((1, 1, D_MODEL, EXPERT_FF), lambda s, e, o: (layer, e, 0, 0)),
                  pl.BlockSpec((1, 1, D_MODEL, EXPERT_FF), lambda s, e, o: (layer, e, 0, 0)),
                  pl.BlockSpec((1, 1, EXPERT_FF, D_MODEL), lambda s, e, o: (layer, e, 0, 0))],
        out_specs=pl.BlockSpec((sblk, D_MODEL), lambda s, e, o: (s, 0), pipeline_mode=once),
        scratch_shapes=[pltpu.VMEM((rows, D_MODEL), BF16), pltpu.VMEM((rows, D_MODEL), F32),
                        pltpu.VMEM((rows, LANES), F32)],
    )
    return pl.pallas_call(
        functools.partial(_moe_kernel, cap=cap, nsub=nsub),
        name="moe",
        grid_spec=grid_spec,
        out_shape=jax.ShapeDtypeStruct((t, D_MODEL), F32),
        compiler_params=pltpu.CompilerParams(dimension_semantics=("arbitrary",) * 2,
                                             vmem_limit_bytes=MOE_VMEM_LIMIT),
    )(offs, h2, pos.reshape(N_EXPERTS, 1, t), aff_t.reshape(N_EXPERTS, 1, t), wg, wu, wd)


def _post_kernel(x_ref, f_ref, mod_ref, nw_ref, o_ref):
    mod = mod_ref[0]
    o_ref[...] = x_ref[...] + mod[5:6, :] * (_rms(f_ref[...]) * nw_ref[...])


def _post(x1, f, mod, nw, seq_len):
    t = x1.shape[0]
    tm = min(512, seq_len)
    per_seq = seq_len // tm
    row = lambda i: (i, 0)
    return pl.pallas_call(
        _post_kernel,
        name="post",
        grid=(t // tm,),
        in_specs=[pl.BlockSpec((tm, D_MODEL), row), pl.BlockSpec((tm, D_MODEL), row),
                  pl.BlockSpec((1, 6, D_MODEL), lambda i: (i // per_seq, 0, 0)),
                  pl.BlockSpec((1, D_MODEL), lambda i: (0, 0))],
        out_specs=pl.BlockSpec((tm, D_MODEL), row),
        out_shape=jax.ShapeDtypeStruct((t, D_MODEL), F32),
        compiler_params=_params(("arbitrary",)),
    )(x1, f, mod, nw.reshape(1, D_MODEL))


def _pad_lanes(v, offset=0):
    out = jnp.zeros((1, LANES), F32)
    return out.at[0, offset:offset + v.shape[0]].set(v.astype(F32))


def _split_bf16(w):
    hi = w.astype(BF16)
    return jnp.stack([hi, (w - hi.astype(F32)).astype(BF16)])


def _prep_layer(p, l):
    w_in = p["w_in"][l]
    cols, off = [], 0
    for (_, width, stored, _) in _IN_SEGS:
        seg = w_in[:, off:off + width]
        if stored != width:
            seg = jnp.pad(seg, ((0, 0), (0, stored - width)))
        cols.append(seg)
        off += width
    q = {
        "w_ada": p["w_ada"][l].astype(BF16), "b_ada": p["b_ada"][l],
        "w_in": jnp.concatenate(cols, axis=1).astype(BF16),
        "w_out": p["w_out"][l].astype(BF16),
        "w_router": _split_bf16(jnp.pad(p["w_router"][l], ((0, 0), (0, LANES - N_EXPERTS)))),
        "ssd_bias": jnp.concatenate([_pad_lanes(p["ssd_dt_bias"][l, d]) for d in range(2)], axis=0),
        "ssd_alog": jnp.concatenate([_pad_lanes(p["ssd_a_log"][l, d]) for d in range(2)], axis=0),
        "ssd_d": jnp.repeat(p["ssd_d"][l], SSD_P).reshape(1, SSD_W),
        "ssd_norm": p["ssd_norm"][l].reshape(1, SSD_W),
        "dn_bias": jnp.concatenate([_pad_lanes(p["dn_dt_bias"][l, d], DN_H) for d in range(2)], axis=0),
        "dn_alog": jnp.concatenate([_pad_lanes(p["dn_a_log"][l, d], DN_H) for d in range(2)], axis=0),
        "dn_norm": p["dn_norm"][l].reshape(1, DN_HD),
    }
    q["conv_w"] = jnp.stack([jnp.pad(p[name][l], ((0, SUBLANES - CONV_K), (0, 0))) for name in ("conv_ssd_w", "conv_dn_w")])
    q["conv_b"] = jnp.stack([p["conv_ssd_b"][l], jnp.zeros((DN_CONV_DIM,), F32)])
    for name in ("norm_mix_pre", "norm_mix_post", "norm_ffn_pre", "norm_ffn_post", "q_norm", "k_norm"):
        q[name] = p[name][l]
    return q


def _trunk(x, c, layers, experts, sblk):
    batch, seq_len, _ = x.shape
    t = batch * seq_len
    cap = EC_CAPACITY * t // N_EXPERTS
    cos_t, sin_t = _rope_tables(seq_len)
    x = x.reshape(t, D_MODEL)
    for l, q in enumerate(layers):
        mod = _ada(c, q["w_ada"], q["b_ada"])
        z_ssd, xact, dtp, qkvact, z_dn, ba_dn, qa, ka, va = _inproj(x, mod, q["norm_mix_pre"], q["w_in"],
                                                                    q["conv_w"], q["conv_b"], seq_len)
        s_f, s_b = _ssd(xact, dtp, q["ssd_bias"], q["ssd_alog"], batch, seq_len)
        o_f, o_b = _dn(qkvact, ba_dn, q["dn_bias"], q["dn_alog"], batch, seq_len)
        qh, kh = _qkprep(qa, ka, cos_t, sin_t, q["q_norm"], q["k_norm"], seq_len)
        y_att = _flash(qh, kh, va, batch, seq_len)
        x1, h2, aff_t = _outproj(s_f, s_b, xact, z_ssd, q["ssd_d"], q["ssd_norm"], o_f, o_b, z_dn, y_att, x, mod,
                                 q["dn_norm"], q["norm_mix_post"], q["norm_ffn_pre"], q["w_out"], q["w_router"],
                                 seq_len)
        pos, offs = _route(aff_t, cap)
        f = _moe(offs, h2, pos, aff_t, *experts, l, cap, sblk)
        x = _post(x1, f, mod, q["norm_ffn_post"], seq_len)
    return x.reshape(batch, seq_len, D_MODEL)


def kernel(x_prompt, x_sample, c_prompt, c_sample, w_ada, b_ada, norm_mix_pre, norm_mix_post, w_in, conv_ssd_w, conv_ssd_b, ssd_dt_bias, ssd_a_log, ssd_d, ssd_norm, conv_dn_w, dn_dt_bias, dn_a_log, dn_norm, q_norm, k_norm, w_out, norm_ffn_pre, norm_ffn_post, w_router, w_gate, w_up, w_down):
    p = dict(w_ada=w_ada, b_ada=b_ada, norm_mix_pre=norm_mix_pre, norm_mix_post=norm_mix_post,
             w_in=w_in, conv_ssd_w=conv_ssd_w, conv_ssd_b=conv_ssd_b, ssd_dt_bias=ssd_dt_bias,
             ssd_a_log=ssd_a_log, ssd_d=ssd_d, ssd_norm=ssd_norm, conv_dn_w=conv_dn_w,
             dn_dt_bias=dn_dt_bias, dn_a_log=dn_a_log, dn_norm=dn_norm, q_norm=q_norm, k_norm=k_norm,
             w_out=w_out, norm_ffn_pre=norm_ffn_pre, norm_ffn_post=norm_ffn_post,
             w_router=w_router, w_gate=w_gate, w_up=w_up, w_down=w_down)
    layers = [_prep_layer(p, l) for l in range(w_in.shape[0])]
    experts = (w_gate.astype(BF16), w_up.astype(BF16), w_down.astype(BF16))
    y_prompt = _trunk(x_prompt, c_prompt, layers, experts, MOE_BLOCK)
    y_sample = _trunk(x_sample, c_sample, layers, experts, MOE_BLOCK)
    return (y_prompt, y_sample)
```
